```python
import jax, jax.numpy as jnp
from jax import lax
import numpy as np

D_MODEL = 1024
BATCH = 8
SEQ = 4096
DEPTH = 1

MIX_WIDTH = D_MODEL
ATTN_HEADS = 8
HEAD_DIM = 64
ATTN_WIDTH = ATTN_HEADS * HEAD_DIM
CONV_WIDTH = MIX_WIDTH - ATTN_WIDTH
CONV_KERNEL = 31
IN_COLS = 3 * ATTN_WIDTH + 2 * CONV_WIDTH
Q_BLOCK = 128

PEER_HEADS = 8
PEER_KEY_DIM = 128
PEER_HALF = PEER_KEY_DIM // 2
N_KEYS = 128
N_EXPERTS = N_KEYS * N_KEYS
PEER_TOPK = 16
PEER_CHUNK = 128

EPS = 1e-6

kernel_name = "hymba_sbattn_conformer_peer"


def rms_norm(x, g):
    xf = x.astype(jnp.float32)
    y = xf * lax.rsqrt(jnp.mean(xf * xf, axis=-1, keepdims=True) + EPS)
    return (y * g.astype(jnp.float32)).astype(x.dtype)


def layer_norm(x, g, b):
    xf = x.astype(jnp.float32)
    mu = jnp.mean(xf, axis=-1, keepdims=True)
    xc = xf - mu
    var = jnp.mean(xc * xc, axis=-1, keepdims=True)
    y = xc * lax.rsqrt(var + EPS) * g.astype(jnp.float32) + b.astype(jnp.float32)
    return y.astype(x.dtype)


def stick_breaking_attention(q, k, v):
    B, H, S, hd = q.shape
    qf, kf, vf = q.astype(jnp.float32), k.astype(jnp.float32), v.astype(jnp.float32)
    scale = hd ** -0.5
    key_pos = jnp.arange(S)
    n_blocks = S // Q_BLOCK

    def block(i):
        start = i * Q_BLOCK
        qb = lax.dynamic_slice_in_dim(qf, start, Q_BLOCK, axis=2)
        z = jnp.einsum('bhqd,bhkd->bhqk', qb, kf) * scale
        q_pos = start + jnp.arange(Q_BLOCK)
        mask = key_pos[None, :] < q_pos[:, None]
        log_keep = jnp.where(mask, jax.nn.log_sigmoid(-z), 0.0)
        later = lax.cumsum(log_keep, axis=3, reverse=True) - log_keep
        att = jnp.where(mask, jnp.exp(jax.nn.log_sigmoid(z) + later), 0.0)
        return jnp.einsum('bhqk,bhkd->bhqd', att, vf)

    out = lax.map(block, jnp.arange(n_blocks))
    out = out.transpose(1, 2, 0, 3, 4).reshape(B, H, S, hd)
    return out.astype(q.dtype)


def conformer_conv(u, conv_w, conv_b, ln_g, ln_b):
    a, gate = jnp.split(u, 2, axis=-1)
    h = a * jax.nn.sigmoid(gate)
    h = lax.conv_general_dilated(
        h, conv_w[:, None, :], window_strides=(1,),
        padding=[(CONV_KERNEL - 1, 0)],
        dimension_numbers=('NWC', 'WIO', 'NWC'),
        feature_group_count=CONV_WIDTH) + conv_b
    h = layer_norm(h, ln_g, ln_b)
    return jax.nn.silu(h)


def hybrid_mixer(x, norm_g, w_in, q_norm_g, k_norm_g, attn_out_g, conv_w, conv_b,
                 conv_ln_g, conv_ln_b, w_out):
    B, S, _ = x.shape
    xn = rms_norm(x, norm_g)
    proj = jnp.einsum('bsd,de->bse', xn, w_in)
    q, k, v, c = jnp.split(proj, [ATTN_WIDTH, 2 * ATTN_WIDTH, 3 * ATTN_WIDTH], axis=-1)
    q = rms_norm(q.reshape(B, S, ATTN_HEADS, HEAD_DIM), q_norm_g).transpose(0, 2, 1, 3)
    k = rms_norm(k.reshape(B, S, ATTN_HEADS, HEAD_DIM), k_norm_g).transpose(0, 2, 1, 3)
    v = v.reshape(B, S, ATTN_HEADS, HEAD_DIM).transpose(0, 2, 1, 3)
    o_attn = stick_breaking_attention(q, k, v).transpose(0, 2, 1, 3).reshape(B, S, ATTN_WIDTH)
    o_attn = rms_norm(o_attn, attn_out_g)
    o_conv = conformer_conv(c, conv_w, conv_b, conv_ln_g, conv_ln_b)
    mixed = jnp.concatenate([o_attn, o_conv], axis=-1)
    return x + jnp.einsum('bse,ed->bsd', mixed, w_out)


def peer_ffn(xn, wq, sub_k1, sub_k2, u_tab, v_tab):
    B, S, D = xn.shape
    xt = xn.reshape(-1, PEER_CHUNK, D)
    k1 = sub_k1.astype(jnp.float32)
    k2 = sub_k2.astype(jnp.float32)

    def chunk(xc):
        qh = jnp.einsum('cd,de->ce', xc, wq).reshape(-1, PEER_HEADS, PEER_KEY_DIM).astype(jnp.float32)
        s1 = jnp.einsum('chd,hnd->chn', qh[..., :PEER_HALF], k1)
        s2 = jnp.einsum('chd,hnd->chn', qh[..., PEER_HALF:], k2)
        v1, i1 = lax.top_k(s1, PEER_TOPK)
        v2, i2 = lax.top_k(s2, PEER_TOPK)
        cand = (v1[..., :, None] + v2[..., None, :]).reshape(-1, PEER_HEADS, PEER_TOPK * PEER_TOPK)
        cidx = (i1[..., :, None] * N_KEYS + i2[..., None, :]).reshape(-1, PEER_HEADS, PEER_TOPK * PEER_TOPK)
        top_s, pos = lax.top_k(cand, PEER_TOPK)
        experts = jnp.take_along_axis(cidx, pos, axis=-1)
        gates = jax.nn.softmax(top_s, axis=-1)
        u_sel = u_tab[experts]
        act = jax.nn.gelu(jnp.einsum('chkd,cd->chk', u_sel, xc).astype(jnp.float32), approximate=False)
        w = (gates * act).astype(xc.dtype)
        v_sel = v_tab[experts]
        return jnp.einsum('chk,chkd->cd', w, v_sel)

    y = lax.map(chunk, xt)
    return y.reshape(B, S, D)


def setup_inputs(seed: int = 0) -> dict:
    key = jax.random.key(seed)
    ks = jax.random.split(key, 16)
    f32 = jnp.float32
    n = lambda k, shape, s: (jax.random.normal(k, shape, f32) * s)
    gain = lambda k, shape: 1.0 + 0.01 * jax.random.normal(k, shape, f32)
    return {
        "x": jax.random.normal(ks[0], (BATCH, SEQ, D_MODEL), f32),
        "norm_mix_g": gain(ks[1], (DEPTH, D_MODEL)),
        "w_in": n(ks[2], (DEPTH, D_MODEL, IN_COLS), D_MODEL ** -0.5),
        "q_norm_g": gain(ks[3], (DEPTH, HEAD_DIM)),
        "k_norm_g": gain(ks[4], (DEPTH, HEAD_DIM)),
        "attn_out_g": gain(ks[5], (DEPTH, ATTN_WIDTH)),
        "conv_w": n(ks[6], (DEPTH, CONV_KERNEL, CONV_WIDTH), CONV_KERNEL ** -0.5),
        "conv_b": n(ks[7], (DEPTH, CONV_WIDTH), 0.01),
        "conv_ln_g": gain(ks[8], (DEPTH, CONV_WIDTH)),
        "conv_ln_b": n(ks[9], (DEPTH, CONV_WIDTH), 0.01),
        "w_out": n(ks[10], (DEPTH, MIX_WIDTH, D_MODEL), MIX_WIDTH ** -0.5),
        "norm_ffn_g": gain(ks[11], (DEPTH, D_MODEL)),
        "peer_wq": n(ks[12], (DEPTH, D_MODEL, PEER_HEADS * PEER_KEY_DIM), D_MODEL ** -0.5),
        "peer_k1": n(ks[13], (DEPTH, PEER_HEADS, N_KEYS, PEER_HALF), PEER_HALF ** -0.5),
        "peer_k2": n(ks[14], (DEPTH, PEER_HEADS, N_KEYS, PEER_HALF), PEER_HALF ** -0.5),
        "peer_u": n(jax.random.fold_in(ks[15], 0), (DEPTH, N_EXPERTS, D_MODEL), D_MODEL ** -0.5),
        "peer_v": n(jax.random.fold_in(ks[15], 1), (DEPTH, N_EXPERTS, D_MODEL), PEER_HEADS ** -0.5),
    }


def reference(x, norm_mix_g, w_in, q_norm_g, k_norm_g, attn_out_g, conv_w, conv_b,
              conv_ln_g, conv_ln_b, w_out, norm_ffn_g, peer_wq, peer_k1, peer_k2,
              peer_u, peer_v):
    h = x
    for l in range(DEPTH):
        h = hybrid_mixer(h, norm_mix_g[l], w_in[l], q_norm_g[l], k_norm_g[l], attn_out_g[l],
                         conv_w[l], conv_b[l], conv_ln_g[l], conv_ln_b[l], w_out[l])
        h = h + peer_ffn(rms_norm(h, norm_ffn_g[l]), peer_wq[l], peer_k1[l], peer_k2[l],
                         peer_u[l], peer_v[l])
    return h
```

```python
import functools

import jax
import jax.numpy as jnp
from jax import lax
from jax.experimental import pallas as pl
from jax.experimental.pallas import tpu as pltpu

F32 = jnp.float32
BF16 = jnp.bfloat16
I32 = jnp.int32

EPS = 1e-6
HEAD_DIM = 64
ATTN_HEADS = 8
ATTN_WIDTH = ATTN_HEADS * HEAD_DIM
CONV_KERNEL = 31
CONV_HALO = 32
PEER_HEADS = 8
PEER_HALF = 64
N_KEYS = 128
PEER_TOPK = 16
SEL = PEER_HEADS * PEER_TOPK
EXP_UNDERFLOW = -88.0
HI_MASK = -65536

VMEM_LIMIT = 56 * 1024 * 1024


def _split_bf16(a):
    hi = a.astype(BF16)
    lo = (a - hi.astype(F32)).astype(BF16)
    return hi, lo


def _dot(a, b):
    return jnp.dot(a, b, preferred_element_type=F32)


def _dot_nt(a, b):
    return lax.dot_general(a, b, (((1,), (1,)), ((), ())), preferred_element_type=F32)


def _sigmoid(x):
    return 1.0 / (1.0 + jnp.exp(-x))


def _inproj_kernel(x_ref, g_ref, w_ref, gq_ref, gk_ref, q_ref, k_ref, v_ref, hg_ref):
    x = x_ref[...]
    ms = jnp.mean(x * x, axis=-1, keepdims=True)
    xn = (x * lax.rsqrt(ms + EPS) * g_ref[...]).astype(BF16)
    proj = _dot(xn, w_ref[...])

    r = lax.broadcasted_iota(I32, (ATTN_WIDTH, ATTN_WIDTH), 0) // HEAD_DIM
    c = lax.broadcasted_iota(I32, (ATTN_WIDTH, ATTN_WIDTH), 1) // HEAD_DIM
    same_head = (r == c).astype(BF16)

    def head_norm(t, g):
        hi, lo = _split_bf16(t * t)
        msq = (_dot(hi, same_head) + _dot(lo, same_head)) * (1.0 / HEAD_DIM)
        return t * lax.rsqrt(msq + EPS) * g

    q = head_norm(proj[:, :ATTN_WIDTH], gq_ref[...]) * (HEAD_DIM ** -0.5)
    k = head_norm(proj[:, ATTN_WIDTH:2 * ATTN_WIDTH], gk_ref[...])
    v = proj[:, 2 * ATTN_WIDTH:3 * ATTN_WIDTH]
    for h in range(ATTN_HEADS):
        sl = slice(h * HEAD_DIM, (h + 1) * HEAD_DIM)
        q_ref[h] = q[:, sl].astype(BF16)
        k_ref[h] = k[:, sl].astype(BF16)
        v_ref[h] = v[:, sl].astype(BF16)
    cw = (proj.shape[1] - 3 * ATTN_WIDTH) // 2
    a = proj[:, 3 * ATTN_WIDTH:3 * ATTN_WIDTH + cw]
    gate = proj[:, 3 * ATTN_WIDTH + cw:]
    hg_ref[...] = a * _sigmoid(gate)


def _inproj(x2, g, w_bf, gq_t, gk_t, *, tm):
    T, D = x2.shape
    E = w_bf.shape[1]
    cw = (E - 3 * ATTN_WIDTH) // 2
    head_shape = jax.ShapeDtypeStruct((ATTN_HEADS, T, HEAD_DIM), BF16)
    head_spec = pl.BlockSpec((ATTN_HEADS, tm, HEAD_DIM), lambda i: (0, i, 0))
    return pl.pallas_call(
        _inproj_kernel,
        grid=(T // tm,),
        in_specs=[
            pl.BlockSpec((tm, D), lambda i: (i, 0)),
            pl.BlockSpec((1, D), lambda i: (0, 0)),
            pl.BlockSpec((D, E), lambda i: (0, 0)),
            pl.BlockSpec((1, ATTN_WIDTH), lambda i: (0, 0)),
            pl.BlockSpec((1, ATTN_WIDTH), lambda i: (0, 0)),
        ],
        out_specs=[head_spec, head_spec, head_spec,
                   pl.BlockSpec((tm, cw), lambda i: (i, 0))],
        out_shape=[head_shape, head_shape, head_shape,
                   jax.ShapeDtypeStruct((T, cw), F32)],
        compiler_params=pltpu.CompilerParams(
            dimension_semantics=("arbitrary",), vmem_limit_bytes=VMEM_LIMIT),
        name="inproj",
    )(x2, g, w_bf, gq_t, gk_t)


def _attn_kernel(q_ref, k_ref, v_ref, o_ref, *, tq):
    i = pl.program_id(2)
    row = lax.broadcasted_iota(I32, (tq, tq), 0)
    col = lax.broadcasted_iota(I32, (tq, tq), 1)
    rr = lax.broadcasted_iota(I32, (tq, 2 * tq), 0)
    cc = lax.broadcasted_iota(I32, (tq, 2 * tq), 1)
    suffix = ((rr > cc) | (cc >= tq)).astype(BF16)

    outs = []
    for hh in range(q_ref.shape[0]):
        q = q_ref[hh]

        def cond(st):
            j, carry, _ = st
            return jnp.logical_and(j >= 0, jnp.max(carry) > EXP_UNDERFLOW)

        def body(st, hh=hh, q=q):
            j, carry, acc = st
            start = pl.multiple_of(j * tq, tq)
            k = k_ref[hh, pl.ds(start, tq), :]
            v = v_ref[hh, pl.ds(start, tq), :]
            z = _dot_nt(q, k)
            sp = jnp.maximum(z, 0.0) + jnp.log1p(jnp.exp(-jnp.abs(z)))
            mask = (col + j * tq) < (row + i * tq)
            log_keep = jnp.where(mask, -sp, 0.0)
            hi, lo = _split_bf16(log_keep)
            cum = _dot(hi, suffix) + _dot(lo, suffix)
            later = carry + cum[:, :tq]
            att = jnp.where(mask, jnp.exp(z - sp + later), 0.0)
            acc = acc + _dot(att.astype(BF16), v)
            return j - 1, carry + cum[:, tq:], acc

        init = (i, jnp.zeros((tq, tq), F32), jnp.zeros((tq, HEAD_DIM), F32))
        _, _, acc = lax.while_loop(cond, body, init)
        outs.append(acc)
    o_ref[...] = jnp.concatenate(outs, axis=-1)


def _attention(qh, kh, vh, *, batch, seq, tq, heads_per_step=2):
    H, T, hd = qh.shape
    nq = seq // tq
    hp = heads_per_step
    return pl.pallas_call(
        functools.partial(_attn_kernel, tq=tq),
        grid=(H // hp, batch, nq),
        in_specs=[
            pl.BlockSpec((hp, tq, hd), lambda h, b, i: (h, b * nq + i, 0)),
            pl.BlockSpec((hp, seq, hd), lambda h, b, i: (h, b, 0)),
            pl.BlockSpec((hp, seq, hd), lambda h, b, i: (h, b, 0)),
        ],
        out_specs=pl.BlockSpec((tq, hp * hd), lambda h, b, i: (b * nq + i, h)),
        out_shape=jax.ShapeDtypeStruct((T, H * hd), F32),
        compiler_params=pltpu.CompilerParams(
            dimension_semantics=("arbitrary", "arbitrary", "arbitrary"),
            vmem_limit_bytes=VMEM_LIMIT),
        name="sb_attention",
    )(qh, kh, vh)


def _mix_kernel(x_ref, o_ref, hg_ref, hgp_ref, cw_ref, cb_ref, lg_ref, lb_ref, ag_ref,
                wo_ref, g2_ref, wqh_ref, wql_ref, h_ref, xn_ref, qh_ref, ext_ref,
                *, blocks_per_seq):
    tm = x_ref.shape[0]
    first = (pl.program_id(0) % blocks_per_seq) == 0
    ext_ref[0:CONV_HALO, :] = jnp.where(first, 0.0, hgp_ref[...])
    ext_ref[CONV_HALO:, :] = hg_ref[...]
    conv = jnp.zeros(hg_ref.shape, F32)
    for j in range(CONV_KERNEL):
        off = CONV_HALO - (CONV_KERNEL - 1) + j
        conv = conv + cw_ref[j:j + 1, :] * ext_ref[pl.ds(off, tm), :]
    conv = conv + cb_ref[...]
    mu = jnp.mean(conv, axis=-1, keepdims=True)
    xc = conv - mu
    var = jnp.mean(xc * xc, axis=-1, keepdims=True)
    y = xc * lax.rsqrt(var + EPS) * lg_ref[...] + lb_ref[...]
    o_conv = y * _sigmoid(y)

    o = o_ref[...]
    o_attn = o * lax.rsqrt(jnp.mean(o * o, axis=-1, keepdims=True) + EPS) * ag_ref[...]
    mixed = jnp.concatenate([o_attn, o_conv], axis=-1).astype(BF16)
    h = x_ref[...] + _dot(mixed, wo_ref[...])
    h_ref[...] = h

    xn = h * lax.rsqrt(jnp.mean(h * h, axis=-1, keepdims=True) + EPS) * g2_ref[...]
    xn_ref[...] = xn
    hi, lo = _split_bf16(xn)
    wqh = wqh_ref[...]
    qh_ref[...] = _dot(hi, wqh) + _dot(lo, wqh) + _dot(hi, wql_ref[...])


def _mix(x2, o, hg, conv_w, conv_b, ln_g, ln_b, attn_g, wo_bf, g2, wq_hi, wq_lo, *, seq, tm):
    T, D = x2.shape
    cwid = hg.shape[1]
    E = wq_hi.shape[1]
    halo_per_block = tm // CONV_HALO
    row = lambda i: (i, 0)
    const = lambda i: (0, 0)
    return pl.pallas_call(
        functools.partial(_mix_kernel, blocks_per_seq=seq // tm),
        grid=(T // tm,),
        in_specs=[
            pl.BlockSpec((tm, D), row),
            pl.BlockSpec((tm, o.shape[1]), row),
            pl.BlockSpec((tm, cwid), row),
            pl.BlockSpec((CONV_HALO, cwid),
                         lambda i: (jnp.maximum(i * halo_per_block - 1, 0), 0)),
            pl.BlockSpec((CONV_KERNEL, cwid), const),
            pl.BlockSpec((1, cwid), const),
            pl.BlockSpec((1, cwid), const),
            pl.BlockSpec((1, cwid), const),
            pl.BlockSpec((1, o.shape[1]), const),
            pl.BlockSpec(wo_bf.shape, const),
            pl.BlockSpec((1, D), const),
            pl.BlockSpec(wq_hi.shape, const),
            pl.BlockSpec(wq_lo.shape, const),
        ],
        out_specs=[pl.BlockSpec((tm, D), row), pl.BlockSpec((tm, D), row),
                   pl.BlockSpec((tm, E), row)],
        out_shape=[jax.ShapeDtypeStruct((T, D), F32), jax.ShapeDtypeStruct((T, D), F32),
                   jax.ShapeDtypeStruct((T, E), F32)],
        scratch_shapes=[pltpu.VMEM((tm + CONV_HALO, cwid), F32)],
        compiler_params=pltpu.CompilerParams(
            dimension_semantics=("arbitrary",), vmem_limit_bytes=VMEM_LIMIT),
        name="mix_outproj",
    )(x2, o, hg, hg, conv_w, conv_b, ln_g, ln_b, attn_g, wo_bf, g2, wq_hi, wq_lo)


def _top16(s, payload=None):
    n = s.shape[0]
    pos = lax.broadcasted_iota(I32, s.shape, 0).astype(F32)
    vals, idxs, pays = [], [], []
    for _ in range(PEER_TOPK):
        m = jnp.max(s, axis=0, keepdims=True)
        idx = jnp.min(jnp.where(s == m, pos, float(n)), axis=0, keepdims=True)
        sel = pos == idx
        if payload is not None:
            pays.append(jnp.sum(jnp.where(sel, payload, 0.0), axis=0, keepdims=True))
        s = jnp.where(sel, -jnp.inf, s)
        vals.append(m)
        idxs.append(idx)
    vals = jnp.concatenate(vals, axis=0)
    if payload is not None:
        return vals, jnp.concatenate(pays, axis=0)
    return vals, jnp.concatenate(idxs, axis=0)


def _dot3_nt(a, b):
    ah, al = _split_bf16(a)
    bh, bl = _split_bf16(b)
    return _dot_nt(ah, bh) + _dot_nt(al, bh) + _dot_nt(ah, bl)


def _topk_kernel(qh_ref, k1_ref, k2_ref, idx_ref, gate_ref):
    for h in range(PEER_HEADS):
        base = h * 2 * PEER_HALF
        q1 = qh_ref[:, base:base + PEER_HALF]
        q2 = qh_ref[:, base + PEER_HALF:base + 2 * PEER_HALF]
        s1 = _dot3_nt(k1_ref[h], q1)
        s2 = _dot3_nt(k2_ref[h], q2)
        v1, i1 = _top16(s1)
        v2, i2 = _top16(s2)
        cand = jnp.concatenate([v1[a:a + 1, :] + v2 for a in range(PEER_TOPK)], axis=0)
        cidx = jnp.concatenate([i1[a:a + 1, :] * float(N_KEYS) + i2
                                for a in range(PEER_TOPK)], axis=0)
        top_s, experts = _top16(cand, payload=cidx)
        e = jnp.exp(top_s - top_s[0:1, :])
        gates = e / jnp.sum(e, axis=0, keepdims=True)
        rows = slice(h * PEER_TOPK, (h + 1) * PEER_TOPK)
        idx_ref[rows, :] = experts.astype(I32) * 4
        gate_ref[rows, :] = gates


def _topk(qh, k1, k2, *, tm):
    T, E = qh.shape
    out_spec = pl.BlockSpec((SEL, tm), lambda i: (0, i))
    return pl.pallas_call(
        _topk_kernel,
        grid=(T // tm,),
        in_specs=[
            pl.BlockSpec((tm, E), lambda i: (i, 0)),
            pl.BlockSpec(k1.shape, lambda i: (0, 0, 0)),
            pl.BlockSpec(k2.shape, lambda i: (0, 0, 0)),
        ],
        out_specs=[out_spec, out_spec],
        out_shape=[jax.ShapeDtypeStruct((SEL, T), I32), jax.ShapeDtypeStruct((SEL, T), F32)],
        compiler_params=pltpu.CompilerParams(
            dimension_semantics=("arbitrary",), vmem_limit_bytes=VMEM_LIMIT),
        name="peer_topk",
    )(qh, k1, k2)


def _unpack(word):
    lo = lax.bitcast_convert_type(word << 16, F32)
    hi = lax.bitcast_convert_type(word & HI_MASK, F32)
    return lo, hi


def _peer_u_kernel(idx_ref, x_ref, gate_ref, tab_ref, w_ref, prod_ref, sum_ref):
    tb = x_ref.shape[0]

    def token(t, carry):
        xlo = x_ref[t, 0:4, :]
        xhi = x_ref[t, 4:8, :]
        base = t * SEL
        row0 = pl.multiple_of(t * (4 * SEL), 4 * SEL)
        for k in range(SEL):
            r = pl.multiple_of(idx_ref[0, 0, base + k], 4)
            lo, hi = _unpack(tab_ref[pl.ds(r, 4), :])
            prod_ref[pl.ds(row0 + 4 * k, 4), :] = lo * xlo + hi * xhi
        return carry

    lax.fori_loop(0, tb, token, 0)

    ones = jnp.ones((128, 128), BF16)
    hi, lo = _split_bf16(prod_ref[...])
    sum_ref[...] = _dot(hi, ones) + _dot(lo, ones)
    n = tb * SEL
    act_rep = (sum_ref[pl.ds(0, n, stride=4), :] + sum_ref[pl.ds(1, n, stride=4), :]
               + sum_ref[pl.ds(2, n, stride=4), :] + sum_ref[pl.ds(3, n, stride=4), :])
    eye = (lax.broadcasted_iota(I32, (SEL, 128), 0) == lax.broadcasted_iota(I32, (SEL, 128), 1))
    act = jnp.sum(jnp.where(eye[None], act_rep.reshape(tb, SEL, 128), 0.0), axis=1)
    gelu = 0.5 * act * (1.0 + lax.erf(act * (2.0 ** -0.5)))
    w_ref[...] = gate_ref[...] * gelu


def _peer_u(idx3, x3, gates, tab, *, tb):
    T = x3.shape[0]
    return pl.pallas_call(
        _peer_u_kernel,
        grid=(T // tb,),
        in_specs=[
            pl.BlockSpec((1, 1, tb * SEL), lambda i: (i, 0, 0), memory_space=pltpu.SMEM),
            pl.BlockSpec((tb, 8, 128), lambda i: (i, 0, 0)),
            pl.BlockSpec((tb, SEL), lambda i: (i, 0)),
            pl.BlockSpec(memory_space=pltpu.VMEM),
        ],
        out_specs=pl.BlockSpec((tb, SEL), lambda i: (i, 0)),
        out_shape=jax.ShapeDtypeStruct((T, SEL), F32),
        scratch_shapes=[pltpu.VMEM((tb * SEL * 4, 128), F32),
                        pltpu.VMEM((tb * SEL * 4, 128), F32)],
        compiler_params=pltpu.CompilerParams(
            dimension_semantics=("arbitrary",), vmem_limit_bytes=VMEM_LIMIT),
        name="peer_u",
    )(idx3, x3, gates, tab)


def _peer_v_kernel(idx_ref, w_ref, h_ref, tab_ref, y_ref):
    tb = h_ref.shape[0]

    def token(t, carry):
        base = t * SEL
        acc = [jnp.zeros((4, 128), F32) for _ in range(4)]
        for k in range(SEL):
            r = pl.multiple_of(idx_ref[0, 0, base + k], 4)
            lo, hi = _unpack(tab_ref[pl.ds(r, 4), :])
            wk = w_ref[0, 0, base + k]
            p = 2 * (k % 2)
            acc[p] = acc[p] + wk * lo
            acc[p + 1] = acc[p + 1] + wk * hi
        y_ref[t, 0:4, :] = h_ref[t, 0:4, :] + (acc[0] + acc[2])
        y_ref[t, 4:8, :] = h_ref[t, 4:8, :] + (acc[1] + acc[3])
        return carry

    lax.fori_loop(0, tb, token, 0)


def _peer_v(idx3, w3, h3, tab, *, tb):
    T = h3.shape[0]
    smem = lambda: pl.BlockSpec((1, 1, tb * SEL), lambda i: (i, 0, 0), memory_space=pltpu.SMEM)
    return pl.pallas_call(
        _peer_v_kernel,
        grid=(T // tb,),
        in_specs=[
            smem(), smem(),
            pl.BlockSpec((tb, 8, 128), lambda i: (i, 0, 0)),
            pl.BlockSpec(memory_space=pltpu.VMEM),
        ],
        out_specs=pl.BlockSpec((tb, 8, 128), lambda i: (i, 0, 0)),
        out_shape=jax.ShapeDtypeStruct(h3.shape, F32),
        compiler_params=pltpu.CompilerParams(
            dimension_semantics=("arbitrary",), vmem_limit_bytes=VMEM_LIMIT),
        name="peer_v",
    )(idx3, w3, h3, tab)


def _pack_table(tab):
    n, d = tab.shape
    bits = lax.bitcast_convert_type(tab.astype(BF16), jnp.uint16).astype(jnp.uint32)
    word = bits[:, :d // 2] | (bits[:, d // 2:] << 16)
    return lax.bitcast_convert_type(word, I32).reshape(n * 4, 128)


def _layer(x2, p, *, batch, seq, tm_in, tq, tm_mix, tm_topk, tb):
    T, D = x2.shape
    row = lambda a: a.reshape(1, -1)
    w_in_bf = p["w_in"].astype(BF16)
    gq_t = row(jnp.tile(p["q_norm_g"], ATTN_HEADS))
    gk_t = row(jnp.tile(p["k_norm_g"], ATTN_HEADS))
    qh, kh, vh, hg = _inproj(x2, row(p["norm_mix_g"]), w_in_bf, gq_t, gk_t, tm=tm_in)
    o = _attention(qh, kh, vh, batch=batch, seq=seq, tq=tq)

    wq = p["peer_wq"]
    wq_hi = wq.astype(BF16)
    wq_lo = (wq - wq_hi.astype(F32)).astype(BF16)
    h, xn, pq = _mix(x2, o, hg, p["conv_w"], row(p["conv_b"]), row(p["conv_ln_g"]),
                     row(p["conv_ln_b"]), row(p["attn_out_g"]), p["w_out"].astype(BF16),
                     row(p["norm_ffn_g"]), wq_hi, wq_lo, seq=seq, tm=tm_mix)

    idx_t, gate_t = _topk(pq, p["peer_k1"], p["peer_k2"], tm=tm_topk)
    idx3 = idx_t.T.reshape(T // tb, 1, tb * SEL)
    gates = gate_t.T
    w = _peer_u(idx3, xn.reshape(T, 8, 128), gates, _pack_table(p["peer_u"]), tb=tb)
    y3 = _peer_v(idx3, w.reshape(T // tb, 1, tb * SEL), h.reshape(T, 8, 128),
                 _pack_table(p["peer_v"]), tb=tb)
    return y3.reshape(T, D)


def kernel(x, norm_mix_g, w_in, q_norm_g, k_norm_g, attn_out_g, conv_w, conv_b, conv_ln_g,
           conv_ln_b, w_out, norm_ffn_g, peer_wq, peer_k1, peer_k2, peer_u, peer_v):
    batch, seq, d = x.shape
    stacked = dict(norm_mix_g=norm_mix_g, w_in=w_in, q_norm_g=q_norm_g, k_norm_g=k_norm_g,
                   attn_out_g=attn_out_g, conv_w=conv_w, conv_b=conv_b, conv_ln_g=conv_ln_g,
                   conv_ln_b=conv_ln_b, w_out=w_out, norm_ffn_g=norm_ffn_g, peer_wq=peer_wq,
                   peer_k1=peer_k1, peer_k2=peer_k2, peer_u=peer_u, peer_v=peer_v)
    h = x.reshape(batch * seq, d)
    for l in range(w_in.shape[0]):
        p = {name: a[l] for name, a in stacked.items()}
        h = _layer(h, p, batch=batch, seq=seq, tm_in=256, tq=128, tm_mix=256,
                   tm_topk=256, tb=32)
    return h.reshape(batch, seq, d)
```

```python
import functools

import jax
import jax.numpy as jnp
from jax import lax
from jax.experimental import pallas as pl
from jax.experimental.pallas import tpu as pltpu

F32 = jnp.float32
BF16 = jnp.bfloat16
I32 = jnp.int32

EPS = 1e-6
HEAD_DIM = 64
ATTN_HEADS = 8
ATTN_WIDTH = ATTN_HEADS * HEAD_DIM
CONV_KERNEL = 31
CONV_HALO = 32
PEER_HEADS = 8
PEER_HALF = 64
N_KEYS = 128
PEER_TOPK = 16
SEL = PEER_HEADS * PEER_TOPK
EXP_UNDERFLOW = -88.0
HI_MASK = -65536

VMEM_LIMIT = 56 * 1024 * 1024


def _split_bf16(a):
    hi = a.astype(BF16)
    lo = (a - hi.astype(F32)).astype(BF16)
    return hi, lo


def _dot(a, b):
    return jnp.dot(a, b, preferred_element_type=F32)


def _dot_nt(a, b):
    return lax.dot_general(a, b, (((1,), (1,)), ((), ())), preferred_element_type=F32)


def _sigmoid(x):
    return 1.0 / (1.0 + jnp.exp(-x))


def _inproj_kernel(x_ref, g_ref, w_ref, gq_ref, gk_ref, q_ref, k_ref, v_ref, hg_ref):
    x = x_ref[...]
    ms = jnp.mean(x * x, axis=-1, keepdims=True)
    xn = (x * lax.rsqrt(ms + EPS) * g_ref[...]).astype(BF16)
    proj = _dot(xn, w_ref[...])

    r = lax.broadcasted_iota(I32, (ATTN_WIDTH, ATTN_WIDTH), 0) // HEAD_DIM
    c = lax.broadcasted_iota(I32, (ATTN_WIDTH, ATTN_WIDTH), 1) // HEAD_DIM
    same_head = (r == c).astype(BF16)

    def head_norm(t, g):
        hi, lo = _split_bf16(t * t)
        msq = (_dot(hi, same_head) + _dot(lo, same_head)) * (1.0 / HEAD_DIM)
        return t * lax.rsqrt(msq + EPS) * g

    q = head_norm(proj[:, :ATTN_WIDTH], gq_ref[...]) * (HEAD_DIM ** -0.5)
    k = head_norm(proj[:, ATTN_WIDTH:2 * ATTN_WIDTH], gk_ref[...])
    v = proj[:, 2 * ATTN_WIDTH:3 * ATTN_WIDTH]
    for h in range(ATTN_HEADS):
        sl = slice(h * HEAD_DIM, (h + 1) * HEAD_DIM)
        q_ref[h] = q[:, sl].astype(BF16)
        k_ref[h] = k[:, sl].astype(BF16)
        v_ref[h] = v[:, sl].astype(BF16)
    cw = (proj.shape[1] - 3 * ATTN_WIDTH) // 2
    a = proj[:, 3 * ATTN_WIDTH:3 * ATTN_WIDTH + cw]
    gate = proj[:, 3 * ATTN_WIDTH + cw:]
    hg_ref[...] = a * _sigmoid(gate)


def _inproj(x2, g, w_bf, gq_t, gk_t, *, tm):
    T, D = x2.shape
    E = w_bf.shape[1]
    cw = (E - 3 * ATTN_WIDTH) // 2
    head_shape = jax.ShapeDtypeStruct((ATTN_HEADS, T, HEAD_DIM), BF16)
    head_spec = pl.BlockSpec((ATTN_HEADS, tm, HEAD_DIM), lambda i: (0, i, 0))
    return pl.pallas_call(
        _inproj_kernel,
        grid=(T // tm,),
        in_specs=[
            pl.BlockSpec((tm, D), lambda i: (i, 0)),
            pl.BlockSpec((1, D), lambda i: (0, 0)),
            pl.BlockSpec((D, E), lambda i: (0, 0)),
            pl.BlockSpec((1, ATTN_WIDTH), lambda i: (0, 0)),
            pl.BlockSpec((1, ATTN_WIDTH), lambda i: (0, 0)),
        ],
        out_specs=[head_spec, head_spec, head_spec,
                   pl.BlockSpec((tm, cw), lambda i: (i, 0))],
        out_shape=[head_shape, head_shape, head_shape,
                   jax.ShapeDtypeStruct((T, cw), F32)],
        compiler_params=pltpu.CompilerParams(
            dimension_semantics=("arbitrary",), vmem_limit_bytes=VMEM_LIMIT),
        name="inproj",
    )(x2, g, w_bf, gq_t, gk_t)


def _attn_span(q, k, v, offset, carry, suffix):
    tw = suffix.shape[0]
    z = _dot_nt(q, k)
    sp = jnp.maximum(z, 0.0) + jnp.log(1.0 + jnp.exp(-jnp.abs(z)))
    col_minus_row = (lax.broadcasted_iota(I32, z.shape, 1)
                     - lax.broadcasted_iota(I32, z.shape, 0))
    mask = col_minus_row < offset
    log_keep = jnp.where(mask, -sp, 0.0)
    later = []
    for s in reversed(range(z.shape[1] // tw)):
        lk = log_keep[:, s * tw:(s + 1) * tw]
        hi, lo = _split_bf16(lk)
        later.append(carry + (_dot(hi, suffix) + _dot(lo, suffix)))
        carry = carry + jnp.sum(lk, axis=-1, keepdims=True)
    later = jnp.concatenate(later[::-1], axis=-1)
    att = jnp.where(mask, jnp.exp(z - sp + later), 0.0)
    return carry, _dot(att.astype(BF16), v)


def _attn_kernel(q_ref, k_ref, v_ref, o_ref, carry_ref, acc_ref, *, rows, span, tw):
    g = pl.program_id(2)
    hp = q_ref.shape[0]
    suffix = (lax.broadcasted_iota(I32, (tw, tw), 0)
              > lax.broadcasted_iota(I32, (tw, tw), 1)).astype(BF16)

    start = pl.multiple_of(jnp.maximum(g * rows - (span - rows), 0), rows)
    cmax = None
    for hh in range(hp):
        carry, acc = _attn_span(q_ref[hh], k_ref[hh, pl.ds(start, span), :],
                                v_ref[hh, pl.ds(start, span), :], g * rows - start,
                                jnp.zeros((rows, 1), F32), suffix)
        carry_ref[hh] = carry
        acc_ref[hh] = acc
        cmax = carry if cmax is None else jnp.maximum(cmax, carry)

    @pl.when(jnp.max(cmax) > EXP_UNDERFLOW)
    def _():
        for hh in range(hp):

            def cond(st):
                j, carry, _ = st
                return jnp.logical_and(j >= 0, jnp.max(carry) > EXP_UNDERFLOW)

            def body(st, hh=hh):
                j, carry, acc = st
                ks = pl.multiple_of(j * tw, tw)
                carry, out = _attn_span(q_ref[hh], k_ref[hh, pl.ds(ks, tw), :],
                                        v_ref[hh, pl.ds(ks, tw), :], rows + tw, carry, suffix)
                return j - 1, carry, acc + out

            init = (start // tw - 1, carry_ref[hh], acc_ref[hh])
            acc_ref[hh] = lax.while_loop(cond, body, init)[2]

    o_ref[...] = jnp.concatenate([acc_ref[hh] for hh in range(hp)], axis=-1)


def _attention(qh, kh, vh, *, batch, seq, rows=256, span=512, tw=256, heads_per_step=2):
    H, T, hd = qh.shape
    ng = seq // rows
    hp = heads_per_step
    return pl.pallas_call(
        functools.partial(_attn_kernel, rows=rows, span=span, tw=tw),
        grid=(H // hp, batch, ng),
        in_specs=[
            pl.BlockSpec((hp, rows, hd), lambda h, b, i: (h, b * ng + i, 0)),
            pl.BlockSpec((hp, seq, hd), lambda h, b, i: (h, b, 0)),
            pl.BlockSpec((hp, seq, hd), lambda h, b, i: (h, b, 0)),
        ],
        out_specs=pl.BlockSpec((rows, hp * hd), lambda h, b, i: (b * ng + i, h)),
        out_shape=jax.ShapeDtypeStruct((T, H * hd), F32),
        scratch_shapes=[pltpu.VMEM((hp, rows, 1), F32),
                        pltpu.VMEM((hp, rows, hd), F32)],
        compiler_params=pltpu.CompilerParams(
            dimension_semantics=("arbitrary", "arbitrary", "arbitrary"),
            vmem_limit_bytes=VMEM_LIMIT),
        name="sb_attention",
    )(qh, kh, vh)


def _mix_kernel(x_ref, o_ref, hg_ref, hgp_ref, cw_ref, cb_ref, lg_ref, lb_ref, ag_ref,
                wo_ref, g2_ref, wqh_ref, wql_ref, h_ref, xn_ref, qh_ref, ext_ref,
                *, blocks_per_seq):
    tm = x_ref.shape[0]
    first = (pl.program_id(0) % blocks_per_seq) == 0
    ext_ref[0:CONV_HALO, :] = jnp.where(first, 0.0, hgp_ref[...])
    ext_ref[CONV_HALO:, :] = hg_ref[...]
    conv = jnp.zeros(hg_ref.shape, F32)
    for j in range(CONV_KERNEL):
        off = CONV_HALO - (CONV_KERNEL - 1) + j
        conv = conv + cw_ref[j:j + 1, :] * ext_ref[pl.ds(off, tm), :]
    conv = conv + cb_ref[...]
    mu = jnp.mean(conv, axis=-1, keepdims=True)
    xc = conv - mu
    var = jnp.mean(xc * xc, axis=-1, keepdims=True)
    y = xc * lax.rsqrt(var + EPS) * lg_ref[...] + lb_ref[...]
    o_conv = y * _sigmoid(y)

    o = o_ref[...]
    o_attn = o * lax.rsqrt(jnp.mean(o * o, axis=-1, keepdims=True) + EPS) * ag_ref[...]
    mixed = jnp.concatenate([o_attn, o_conv], axis=-1).astype(BF16)
    h = x_ref[...] + _dot(mixed, wo_ref[...])
    h_ref[...] = h

    xn = h * lax.rsqrt(jnp.mean(h * h, axis=-1, keepdims=True) + EPS) * g2_ref[...]
    xn_ref[...] = xn
    hi, lo = _split_bf16(xn)
    wqh = wqh_ref[...]
    qh_ref[...] = _dot(hi, wqh) + _dot(lo, wqh) + _dot(hi, wql_ref[...])


def _mix(x2, o, hg, conv_w, conv_b, ln_g, ln_b, attn_g, wo_bf, g2, wq_hi, wq_lo, *, seq, tm):
    T, D = x2.shape
    cwid = hg.shape[1]
    E = wq_hi.shape[1]
    halo_per_block = tm // CONV_HALO
    row = lambda i: (i, 0)
    const = lambda i: (0, 0)
    return pl.pallas_call(
        functools.partial(_mix_kernel, blocks_per_seq=seq // tm),
        grid=(T // tm,),
        in_specs=[
            pl.BlockSpec((tm, D), row),
            pl.BlockSpec((tm, o.shape[1]), row),
            pl.BlockSpec((tm, cwid), row),
            pl.BlockSpec((CONV_HALO, cwid),
                         lambda i: (jnp.maximum(i * halo_per_block - 1, 0), 0)),
            pl.BlockSpec((CONV_KERNEL, cwid), const),
            pl.BlockSpec((1, cwid), const),
            pl.BlockSpec((1, cwid), const),
            pl.BlockSpec((1, cwid), const),
            pl.BlockSpec((1, o.shape[1]), const),
            pl.BlockSpec(wo_bf.shape, const),
            pl.BlockSpec((1, D), const),
            pl.BlockSpec(wq_hi.shape, const),
            pl.BlockSpec(wq_lo.shape, const),
        ],
        out_specs=[pl.BlockSpec((tm, D), row), pl.BlockSpec((tm, D), row),
                   pl.BlockSpec((tm, E), row)],
        out_shape=[jax.ShapeDtypeStruct((T, D), F32), jax.ShapeDtypeStruct((T, D), F32),
                   jax.ShapeDtypeStruct((T, E), F32)],
        scratch_shapes=[pltpu.VMEM((tm + CONV_HALO, cwid), F32)],
        compiler_params=pltpu.CompilerParams(
            dimension_semantics=("arbitrary",), vmem_limit_bytes=VMEM_LIMIT),
        name="mix_outproj",
    )(x2, o, hg, hg, conv_w, conv_b, ln_g, ln_b, attn_g, wo_bf, g2, wq_hi, wq_lo)


def _top16(s, pos=None, payload=None):
    if pos is None:
        pos = lax.broadcasted_iota(I32, s.shape, 0).astype(F32)
    vals, outs = [], []
    for _ in range(PEER_TOPK):
        m = jnp.max(s, axis=0, keepdims=True)
        idx = jnp.min(jnp.where(s == m, pos, 1e9), axis=0, keepdims=True)
        sel = pos == idx
        if payload is None:
            outs.append(idx)
        else:
            outs.append(jnp.sum(jnp.where(sel, payload, 0.0), axis=0, keepdims=True))
        s = jnp.where(sel, -jnp.inf, s)
        vals.append(m)
    return jnp.concatenate(vals, axis=0), jnp.concatenate(outs, axis=0)


def _dot3_nt(a, b):
    ah, al = _split_bf16(a)
    bh, bl = _split_bf16(b)
    return _dot_nt(ah, bh) + _dot_nt(al, bh) + _dot_nt(ah, bl)


def _pair_candidates(v1, i1, v2, i2):
    r8 = lax.broadcasted_iota(I32, (8, v1.shape[1]), 0).astype(F32)
    sc, ex, ps = [], [], []

    def add(s1, e1, s2, e2, p, keep=None):
        s = s1 + s2
        sc.append(s if keep is None else jnp.where(keep, s, -jnp.inf))
        ex.append(e1 * float(N_KEYS) + e2)
        ps.append(p if keep is None else jnp.where(keep, p, 2e9))

    for a, half in ((0, 0), (0, 1), (1, 0), (2, 0), (3, 0)):
        b = slice(8 * half, 8 * half + 8)
        add(v1[a:a + 1], i1[a:a + 1], v2[b], i2[b], r8 + float(a * PEER_TOPK + 8 * half))
    for b in range(3):
        add(v1[0:8], i1[0:8], v2[b:b + 1], i2[b:b + 1], r8 * float(PEER_TOPK) + float(b),
            keep=r8 >= 4.0)
    add(v1[8:16], i1[8:16], v2[0:1], i2[0:1], (r8 + 8.0) * float(PEER_TOPK))
    return (jnp.concatenate(sc, axis=0), jnp.concatenate(ex, axis=0),
            jnp.concatenate(ps, axis=0))


def _topk_kernel(qh_ref, k1_ref, k2_ref, idx_ref, gate_ref):
    for h in range(PEER_HEADS):
        base = h * 2 * PEER_HALF
        q1 = qh_ref[:, base:base + PEER_HALF]
        q2 = qh_ref[:, base + PEER_HALF:base + 2 * PEER_HALF]
        s1 = _dot3_nt(k1_ref[h], q1)
        s2 = _dot3_nt(k2_ref[h], q2)
        v1, i1 = _top16(s1)
        v2, i2 = _top16(s2)
        cand, cexp, cpos = _pair_candidates(v1, i1, v2, i2)
        top_s, experts = _top16(cand, pos=cpos, payload=cexp)
        e = jnp.exp(top_s - top_s[0:1, :])
        gates = e / jnp.sum(e, axis=0, keepdims=True)
        rows = slice(h * PEER_TOPK, (h + 1) * PEER_TOPK)
        idx_ref[rows, :] = experts.astype(I32) * 4
        gate_ref[rows, :] = gates


def _topk(qh, k1, k2, *, tm):
    T, E = qh.shape
    out_spec = pl.BlockSpec((SEL, tm), lambda i: (0, i))
    return pl.pallas_call(
        _topk_kernel,
        grid=(T // tm,),
        in_specs=[
            pl.BlockSpec((tm, E), lambda i: (i, 0)),
            pl.BlockSpec(k1.shape, lambda i: (0, 0, 0)),
            pl.BlockSpec(k2.shape, lambda i: (0, 0, 0)),
        ],
        out_specs=[out_spec, out_spec],
        out_shape=[jax.ShapeDtypeStruct((SEL, T), I32), jax.ShapeDtypeStruct((SEL, T), F32)],
        compiler_params=pltpu.CompilerParams(
            dimension_semantics=("arbitrary",), vmem_limit_bytes=VMEM_LIMIT),
        name="peer_topk",
    )(qh, k1, k2)


def _unpack(word):
    lo = lax.bitcast_convert_type(word << 16, F32)
    hi = lax.bitcast_convert_type(word & HI_MASK, F32)
    return lo, hi


def _peer_u_kernel(idx_ref, x_ref, gate_ref, tab_ref, w_ref, prod_ref, sum_ref):
    tb = x_ref.shape[0]

    def token(t, carry):
        xlo = x_ref[t, 0:4, :]
        xhi = x_ref[t, 4:8, :]
        ids = idx_ref.at[0, 0, pl.ds(t * SEL, SEL)]
        row0 = pl.multiple_of(t * (4 * SEL), 4 * SEL)
        for k in range(SEL):
            r = pl.multiple_of(ids[k], 4)
            lo, hi = _unpack(tab_ref[pl.ds(r, 4), :])
            prod_ref[pl.ds(row0 + 4 * k, 4), :] = lo * xlo + hi * xhi
        return carry

    lax.fori_loop(0, tb, token, 0)

    ones = jnp.ones((128, 128), BF16)
    hi, lo = _split_bf16(prod_ref[...])
    sum_ref[...] = _dot(hi, ones) + _dot(lo, ones)
    n = tb * SEL
    act_rep = (sum_ref[pl.ds(0, n, stride=4), :] + sum_ref[pl.ds(1, n, stride=4), :]
               + sum_ref[pl.ds(2, n, stride=4), :] + sum_ref[pl.ds(3, n, stride=4), :])
    eye = (lax.broadcasted_iota(I32, (SEL, 128), 0) == lax.broadcasted_iota(I32, (SEL, 128), 1))
    act = jnp.sum(jnp.where(eye[None], act_rep.reshape(tb, SEL, 128), 0.0), axis=1)
    gelu = 0.5 * act * (1.0 + lax.erf(act * (2.0 ** -0.5)))
    w_ref[...] = gate_ref[...] * gelu


def _peer_u(idx3, x3, gates, tab, *, tb):
    T = x3.shape[0]
    return pl.pallas_call(
        _peer_u_kernel,
        grid=(T // tb,),
        in_specs=[
            pl.BlockSpec((1, 1, tb * SEL), lambda i: (i, 0, 0), memory_space=pltpu.SMEM),
            pl.BlockSpec((tb, 8, 128), lambda i: (i, 0, 0)),
            pl.BlockSpec((tb, SEL), lambda i: (i, 0)),
            pl.BlockSpec(memory_space=pltpu.VMEM),
        ],
        out_specs=pl.BlockSpec((tb, SEL), lambda i: (i, 0)),
        out_shape=jax.ShapeDtypeStruct((T, SEL), F32),
        scratch_shapes=[pltpu.VMEM((tb * SEL * 4, 128), F32),
                        pltpu.VMEM((tb * SEL * 4, 128), F32)],
        compiler_params=pltpu.CompilerParams(
            dimension_semantics=("arbitrary",), vmem_limit_bytes=VMEM_LIMIT),
        name="peer_u",
    )(idx3, x3, gates, tab)


def _peer_v_kernel(idx_ref, w_ref, h_ref, tab_ref, y_ref):
    tb = h_ref.shape[0]

    def token(t, carry):
        ids = idx_ref.at[0, 0, pl.ds(t * SEL, SEL)]
        ws = w_ref.at[0, 0, pl.ds(t * SEL, SEL)]
        acc = [jnp.zeros((4, 128), F32) for _ in range(4)]
        for k in range(SEL):
            r = pl.multiple_of(ids[k], 4)
            lo, hi = _unpack(tab_ref[pl.ds(r, 4), :])
            wk = ws[k]
            p = 2 * (k % 2)
            acc[p] = acc[p] + wk * lo
            acc[p + 1] = acc[p + 1] + wk * hi
        y_ref[t, 0:4, :] = h_ref[t, 0:4, :] + (acc[0] + acc[2])
        y_ref[t, 4:8, :] = h_ref[t, 4:8, :] + (acc[1] + acc[3])
        return carry

    lax.fori_loop(0, tb, token, 0)


def _peer_v(idx3, w3, h3, tab, *, tb):
    T = h3.shape[0]
    smem = lambda: pl.BlockSpec((1, 1, tb * SEL), lambda i: (i, 0, 0), memory_space=pltpu.SMEM)
    return pl.pallas_call(
        _peer_v_kernel,
        grid=(T // tb,),
        in_specs=[
            smem(), smem(),
            pl.BlockSpec((tb, 8, 128), lambda i: (i, 0, 0)),
            pl.BlockSpec(memory_space=pltpu.VMEM),
        ],
        out_specs=pl.BlockSpec((tb, 8, 128), lambda i: (i, 0, 0)),
        out_shape=jax.ShapeDtypeStruct(h3.shape, F32),
        compiler_params=pltpu.CompilerParams(
            dimension_semantics=("arbitrary",), vmem_limit_bytes=VMEM_LIMIT),
        name="peer_v",
    )(idx3, w3, h3, tab)


def _pack_table(tab):
    n, d = tab.shape
    bits = lax.bitcast_convert_type(tab.astype(BF16), jnp.uint16).astype(jnp.uint32)
    word = bits[:, :d // 2] | (bits[:, d // 2:] << 16)
    return lax.bitcast_convert_type(word, I32).reshape(n * 4, 128)


def _layer(x2, p, *, batch, seq, tm_in, tm_mix, tm_topk, tb):
    T, D = x2.shape
    row = lambda a: a.reshape(1, -1)
    w_in_bf = p["w_in"].astype(BF16)
    gq_t = row(jnp.tile(p["q_norm_g"], ATTN_HEADS))
    gk_t = row(jnp.tile(p["k_norm_g"], ATTN_HEADS))
    qh, kh, vh, hg = _inproj(x2, row(p["norm_mix_g"]), w_in_bf, gq_t, gk_t, tm=tm_in)
    o = _attention(qh, kh, vh, batch=batch, seq=seq)

    wq = p["peer_wq"]
    wq_hi = wq.astype(BF16)
    wq_lo = (wq - wq_hi.astype(F32)).astype(BF16)
    h, xn, pq = _mix(x2, o, hg, p["conv_w"], row(p["conv_b"]), row(p["conv_ln_g"]),
                     row(p["conv_ln_b"]), row(p["attn_out_g"]), p["w_out"].astype(BF16),
                     row(p["norm_ffn_g"]), wq_hi, wq_lo, seq=seq, tm=tm_mix)

    idx_t, gate_t = _topk(pq, p["peer_k1"], p["peer_k2"], tm=tm_topk)
    idx3 = idx_t.T.reshape(T // tb, 1, tb * SEL)
    gates = gate_t.T
    w = _peer_u(idx3, xn.reshape(T, 8, 128), gates, _pack_table(p["peer_u"]), tb=tb)
    y3 = _peer_v(idx3, w.reshape(T // tb, 1, tb * SEL), h.reshape(T, 8, 128),
                 _pack_table(p["peer_v"]), tb=tb)
    return y3.reshape(T, D)


def kernel(x, norm_mix_g, w_in, q_norm_g, k_norm_g, attn_out_g, conv_w, conv_b, conv_ln_g,
           conv_ln_b, w_out, norm_ffn_g, peer_wq, peer_k1, peer_k2, peer_u, peer_v):
    batch, seq, d = x.shape
    stacked = dict(norm_mix_g=norm_mix_g, w_in=w_in, q_norm_g=q_norm_g, k_norm_g=k_norm_g,
                   attn_out_g=attn_out_g, conv_w=conv_w, conv_b=conv_b, conv_ln_g=conv_ln_g,
                   conv_ln_b=conv_ln_b, w_out=w_out, norm_ffn_g=norm_ffn_g, peer_wq=peer_wq,
                   peer_k1=peer_k1, peer_k2=peer_k2, peer_u=peer_u, peer_v=peer_v)
    h = x.reshape(batch * seq, d)
    for l in range(w_in.shape[0]):
        p = {name: a[l] for name, a in stacked.items()}
        h = _layer(h, p, batch=batch, seq=seq, tm_in=256, tm_mix=256,
                   tm_topk=256, tb=32)
    return h.reshape(batch, seq, d)
```

```python
import functools

import jax
import jax.numpy as jnp
from jax import lax
from jax.experimental import pallas as pl
from jax.experimental.pallas import tpu as pltpu

F32 = jnp.float32
BF16 = jnp.bfloat16
I32 = jnp.int32

EPS = 1e-6
HEAD_DIM = 64
ATTN_HEADS = 8
ATTN_WIDTH = ATTN_HEADS * HEAD_DIM
CONV_KERNEL = 31
CONV_HALO = 32
PEER_HEADS = 8
PEER_HALF = 64
N_KEYS = 128
PEER_TOPK = 16
SEL = PEER_HEADS * PEER_TOPK
EXP_UNDERFLOW = -88.0
HI_MASK = -65536

VMEM_LIMIT = 56 * 1024 * 1024


def _split_bf16(a):
    hi = a.astype(BF16)
    lo = (a - hi.astype(F32)).astype(BF16)
    return hi, lo


def _dot(a, b):
    return jnp.dot(a, b, preferred_element_type=F32)


def _dot_nt(a, b):
    return lax.dot_general(a, b, (((1,), (1,)), ((), ())), preferred_element_type=F32)


def _sigmoid(x):
    return 1.0 / (1.0 + jnp.exp(-x))


def _inproj_kernel(x_ref, g_ref, w_ref, gq_ref, gk_ref, q_ref, k_ref, v_ref, hg_ref):
    x = x_ref[...]
    ms = jnp.mean(x * x, axis=-1, keepdims=True)
    xn = (x * lax.rsqrt(ms + EPS) * g_ref[...]).astype(BF16)
    proj = _dot(xn, w_ref[...])

    r = lax.broadcasted_iota(I32, (ATTN_WIDTH, ATTN_WIDTH), 0) // HEAD_DIM
    c = lax.broadcasted_iota(I32, (ATTN_WIDTH, ATTN_WIDTH), 1) // HEAD_DIM
    same_head = (r == c).astype(BF16)

    def head_norm(t, g):
        hi, lo = _split_bf16(t * t)
        msq = (_dot(hi, same_head) + _dot(lo, same_head)) * (1.0 / HEAD_DIM)
        return t * lax.rsqrt(msq + EPS) * g

    q = head_norm(proj[:, :ATTN_WIDTH], gq_ref[...]) * (HEAD_DIM ** -0.5)
    k = head_norm(proj[:, ATTN_WIDTH:2 * ATTN_WIDTH], gk_ref[...])
    v = proj[:, 2 * ATTN_WIDTH:3 * ATTN_WIDTH]
    for h in range(ATTN_HEADS):
        sl = slice(h * HEAD_DIM, (h + 1) * HEAD_DIM)
        q_ref[h] = q[:, sl].astype(BF16)
        k_ref[h] = k[:, sl].astype(BF16)
        v_ref[h] = v[:, sl].astype(BF16)
    cw = (proj.shape[1] - 3 * ATTN_WIDTH) // 2
    a = proj[:, 3 * ATTN_WIDTH:3 * ATTN_WIDTH + cw]
    gate = proj[:, 3 * ATTN_WIDTH + cw:]
    hg_ref[...] = a * _sigmoid(gate)


def _inproj(x2, g, w_bf, gq_t, gk_t, *, tm):
    T, D = x2.shape
    E = w_bf.shape[1]
    cw = (E - 3 * ATTN_WIDTH) // 2
    head_shape = jax.ShapeDtypeStruct((ATTN_HEADS, T, HEAD_DIM), BF16)
    head_spec = pl.BlockSpec((ATTN_HEADS, tm, HEAD_DIM), lambda i: (0, i, 0))
    return pl.pallas_call(
        _inproj_kernel,
        grid=(T // tm,),
        in_specs=[
            pl.BlockSpec((tm, D), lambda i: (i, 0)),
            pl.BlockSpec((1, D), lambda i: (0, 0)),
            pl.BlockSpec((D, E), lambda i: (0, 0)),
            pl.BlockSpec((1, ATTN_WIDTH), lambda i: (0, 0)),
            pl.BlockSpec((1, ATTN_WIDTH), lambda i: (0, 0)),
        ],
        out_specs=[head_spec, head_spec, head_spec,
                   pl.BlockSpec((tm, cw), lambda i: (i, 0))],
        out_shape=[head_shape, head_shape, head_shape,
                   jax.ShapeDtypeStruct((T, cw), F32)],
        compiler_params=pltpu.CompilerParams(
            dimension_semantics=("arbitrary",), vmem_limit_bytes=VMEM_LIMIT),
        name="inproj",
    )(x2, g, w_bf, gq_t, gk_t)


def _attn_span(q, k, v, offset, carry, suffix):
    tw = suffix.shape[0]
    z = _dot_nt(q, k)
    sp = jnp.maximum(z, 0.0) + jnp.log(1.0 + jnp.exp(-jnp.abs(z)))
    col_minus_row = (lax.broadcasted_iota(I32, z.shape, 1)
                     - lax.broadcasted_iota(I32, z.shape, 0))
    mask = col_minus_row < offset
    log_keep = jnp.where(mask, -sp, 0.0)
    later = []
    for s in reversed(range(z.shape[1] // tw)):
        lk = log_keep[:, s * tw:(s + 1) * tw]
        hi, lo = _split_bf16(lk)
        later.append(carry + (_dot(hi, suffix) + _dot(lo, suffix)))
        carry = carry + jnp.sum(lk, axis=-1, keepdims=True)
    later = jnp.concatenate(later[::-1], axis=-1)
    att = jnp.where(mask, jnp.exp(z - sp + later), 0.0)
    return carry, _dot(att.astype(BF16), v)


def _attn_kernel(q_ref, k_ref, v_ref, o_ref, carry_ref, acc_ref, *, rows, span, tw):
    g = pl.program_id(2)
    hp = q_ref.shape[0]
    suffix = (lax.broadcasted_iota(I32, (tw, tw), 0)
              > lax.broadcasted_iota(I32, (tw, tw), 1)).astype(BF16)

    start = pl.multiple_of(jnp.maximum(g * rows - (span - rows), 0), rows)
    cmax = None
    for hh in range(hp):
        carry, acc = _attn_span(q_ref[hh], k_ref[hh, pl.ds(start, span), :],
                                v_ref[hh, pl.ds(start, span), :], g * rows - start,
                                jnp.zeros((rows, 1), F32), suffix)
        carry_ref[hh] = carry
        acc_ref[hh] = acc
        cmax = carry if cmax is None else jnp.maximum(cmax, carry)

    @pl.when(jnp.max(cmax) > EXP_UNDERFLOW)
    def _():
        for hh in range(hp):

            def cond(st):
                j, carry, _ = st
                return jnp.logical_and(j >= 0, jnp.max(carry) > EXP_UNDERFLOW)

            def body(st, hh=hh):
                j, carry, acc = st
                ks = pl.multiple_of(j * tw, tw)
                carry, out = _attn_span(q_ref[hh], k_ref[hh, pl.ds(ks, tw), :],
                                        v_ref[hh, pl.ds(ks, tw), :], rows + tw, carry, suffix)
                return j - 1, carry, acc + out

            init = (start // tw - 1, carry_ref[hh], acc_ref[hh])
            acc_ref[hh] = lax.while_loop(cond, body, init)[2]

    o_ref[...] = jnp.concatenate([acc_ref[hh] for hh in range(hp)], axis=-1)


def _attention(qh, kh, vh, *, batch, seq, rows=256, span=512, tw=256, heads_per_step=2):
    H, T, hd = qh.shape
    ng = seq // rows
    hp = heads_per_step
    return pl.pallas_call(
        functools.partial(_attn_kernel, rows=rows, span=span, tw=tw),
        grid=(H // hp, batch, ng),
        in_specs=[
            pl.BlockSpec((hp, rows, hd), lambda h, b, i: (h, b * ng + i, 0)),
            pl.BlockSpec((hp, seq, hd), lambda h, b, i: (h, b, 0)),
            pl.BlockSpec((hp, seq, hd), lambda h, b, i: (h, b, 0)),
        ],
        out_specs=pl.BlockSpec((rows, hp * hd), lambda h, b, i: (b * ng + i, h)),
        out_shape=jax.ShapeDtypeStruct((T, H * hd), F32),
        scratch_shapes=[pltpu.VMEM((hp, rows, 1), F32),
                        pltpu.VMEM((hp, rows, hd), F32)],
        compiler_params=pltpu.CompilerParams(
            dimension_semantics=("arbitrary", "arbitrary", "arbitrary"),
            vmem_limit_bytes=VMEM_LIMIT),
        name="sb_attention",
    )(qh, kh, vh)


def _mix_kernel(x_ref, o_ref, hg_ref, hgp_ref, cw_ref, cb_ref, lg_ref, lb_ref, ag_ref,
                wo_ref, g2_ref, wqh_ref, wql_ref, h_ref, xn_ref, qh_ref, ext_ref,
                *, blocks_per_seq):
    tm = x_ref.shape[0]
    first = (pl.program_id(0) % blocks_per_seq) == 0
    ext_ref[0:CONV_HALO, :] = jnp.where(first, 0.0, hgp_ref[...])
    ext_ref[CONV_HALO:, :] = hg_ref[...]
    conv = jnp.zeros(hg_ref.shape, F32)
    for j in range(CONV_KERNEL):
        off = CONV_HALO - (CONV_KERNEL - 1) + j
        conv = conv + cw_ref[j:j + 1, :] * ext_ref[pl.ds(off, tm), :]
    conv = conv + cb_ref[...]
    mu = jnp.mean(conv, axis=-1, keepdims=True)
    xc = conv - mu
    var = jnp.mean(xc * xc, axis=-1, keepdims=True)
    y = xc * lax.rsqrt(var + EPS) * lg_ref[...] + lb_ref[...]
    o_conv = y * _sigmoid(y)

    o = o_ref[...]
    o_attn = o * lax.rsqrt(jnp.mean(o * o, axis=-1, keepdims=True) + EPS) * ag_ref[...]
    mixed = jnp.concatenate([o_attn, o_conv], axis=-1).astype(BF16)
    h = x_ref[...] + _dot(mixed, wo_ref[...])
    h_ref[...] = h

    xn = h * lax.rsqrt(jnp.mean(h * h, axis=-1, keepdims=True) + EPS) * g2_ref[...]
    xn_ref[...] = xn
    hi, lo = _split_bf16(xn)
    wqh = wqh_ref[...]
    qh_ref[...] = _dot(hi, wqh) + _dot(lo, wqh) + _dot(hi, wql_ref[...])


def _mix(x2, o, hg, conv_w, conv_b, ln_g, ln_b, attn_g, wo_bf, g2, wq_hi, wq_lo, *, seq, tm):
    T, D = x2.shape
    cwid = hg.shape[1]
    E = wq_hi.shape[1]
    halo_per_block = tm // CONV_HALO
    row = lambda i: (i, 0)
    const = lambda i: (0, 0)
    return pl.pallas_call(
        functools.partial(_mix_kernel, blocks_per_seq=seq // tm),
        grid=(T // tm,),
        in_specs=[
            pl.BlockSpec((tm, D), row),
            pl.BlockSpec((tm, o.shape[1]), row),
            pl.BlockSpec((tm, cwid), row),
            pl.BlockSpec((CONV_HALO, cwid),
                         lambda i: (jnp.maximum(i * halo_per_block - 1, 0), 0)),
            pl.BlockSpec((CONV_KERNEL, cwid), const),
            pl.BlockSpec((1, cwid), const),
            pl.BlockSpec((1, cwid), const),
            pl.BlockSpec((1, cwid), const),
            pl.BlockSpec((1, o.shape[1]), const),
            pl.BlockSpec(wo_bf.shape, const),
            pl.BlockSpec((1, D), const),
            pl.BlockSpec(wq_hi.shape, const),
            pl.BlockSpec(wq_lo.shape, const),
        ],
        out_specs=[pl.BlockSpec((tm, D), row), pl.BlockSpec((tm, D), row),
                   pl.BlockSpec((tm, E), row)],
        out_shape=[jax.ShapeDtypeStruct((T, D), F32), jax.ShapeDtypeStruct((T, D), F32),
                   jax.ShapeDtypeStruct((T, E), F32)],
        scratch_shapes=[pltpu.VMEM((tm + CONV_HALO, cwid), F32)],
        compiler_params=pltpu.CompilerParams(
            dimension_semantics=("arbitrary",), vmem_limit_bytes=VMEM_LIMIT),
        name="mix_outproj",
    )(x2, o, hg, hg, conv_w, conv_b, ln_g, ln_b, attn_g, wo_bf, g2, wq_hi, wq_lo)


def _top16(s, pos=None, payload=None):
    if pos is None:
        pos = lax.broadcasted_iota(I32, s.shape, 0).astype(F32)
    vals, outs = [], []
    for _ in range(PEER_TOPK):
        m = jnp.max(s, axis=0, keepdims=True)
        idx = jnp.min(jnp.where(s == m, pos, 1e9), axis=0, keepdims=True)
        sel = pos == idx
        if payload is None:
            outs.append(idx)
        else:
            outs.append(jnp.sum(jnp.where(sel, payload, 0.0), axis=0, keepdims=True))
        s = jnp.where(sel, -jnp.inf, s)
        vals.append(m)
    return jnp.concatenate(vals, axis=0), jnp.concatenate(outs, axis=0)


def _dot3_nt(a, b):
    ah, al = _split_bf16(a)
    bh, bl = _split_bf16(b)
    return _dot_nt(ah, bh) + _dot_nt(al, bh) + _dot_nt(ah, bl)


def _pair_candidates(v1, i1, v2, i2):
    r8 = lax.broadcasted_iota(I32, (8, v1.shape[1]), 0).astype(F32)
    sc, ex, ps = [], [], []

    def add(s1, e1, s2, e2, p, keep=None):
        s = s1 + s2
        sc.append(s if keep is None else jnp.where(keep, s, -jnp.inf))
        ex.append(e1 * float(N_KEYS) + e2)
        ps.append(p if keep is None else jnp.where(keep, p, 2e9))

    for a, half in ((0, 0), (0, 1), (1, 0), (2, 0), (3, 0)):
        b = slice(8 * half, 8 * half + 8)
        add(v1[a:a + 1], i1[a:a + 1], v2[b], i2[b], r8 + float(a * PEER_TOPK + 8 * half))
    for b in range(3):
        add(v1[0:8], i1[0:8], v2[b:b + 1], i2[b:b + 1], r8 * float(PEER_TOPK) + float(b),
            keep=r8 >= 4.0)
    add(v1[8:16], i1[8:16], v2[0:1], i2[0:1], (r8 + 8.0) * float(PEER_TOPK))
    return (jnp.concatenate(sc, axis=0), jnp.concatenate(ex, axis=0),
            jnp.concatenate(ps, axis=0))


def _topk_kernel(qh_ref, k1_ref, k2_ref, idx_ref, gate_ref):
    for h in range(PEER_HEADS):
        base = h * 2 * PEER_HALF
        q1 = qh_ref[:, base:base + PEER_HALF]
        q2 = qh_ref[:, base + PEER_HALF:base + 2 * PEER_HALF]
        s1 = _dot3_nt(k1_ref[h], q1)
        s2 = _dot3_nt(k2_ref[h], q2)
        v1, i1 = _top16(s1)
        v2, i2 = _top16(s2)
        cand, cexp, cpos = _pair_candidates(v1, i1, v2, i2)
        top_s, experts = _top16(cand, pos=cpos, payload=cexp)
        e = jnp.exp(top_s - top_s[0:1, :])
        gates = e / jnp.sum(e, axis=0, keepdims=True)
        rows = slice(h * PEER_TOPK, (h + 1) * PEER_TOPK)
        idx_ref[rows, :] = experts.astype(I32) * 4
        gate_ref[rows, :] = gates


def _topk(qh, k1, k2, *, tm):
    T, E = qh.shape
    out_spec = pl.BlockSpec((SEL, tm), lambda i: (0, i))
    return pl.pallas_call(
        _topk_kernel,
        grid=(T // tm,),
        in_specs=[
            pl.BlockSpec((tm, E), lambda i: (i, 0)),
            pl.BlockSpec(k1.shape, lambda i: (0, 0, 0)),
            pl.BlockSpec(k2.shape, lambda i: (0, 0, 0)),
        ],
        out_specs=[out_spec, out_spec],
        out_shape=[jax.ShapeDtypeStruct((SEL, T), I32), jax.ShapeDtypeStruct((SEL, T), F32)],
        compiler_params=pltpu.CompilerParams(
            dimension_semantics=("arbitrary",), vmem_limit_bytes=VMEM_LIMIT),
        name="peer_topk",
    )(qh, k1, k2)


CHUNKS = 4
CHUNK_STRIDE = SEL + 8


def _unpack(word):
    lo = lax.bitcast_convert_type(word << 16, F32)
    hi = lax.bitcast_convert_type(word & HI_MASK, F32)
    return lo, hi


def _gather_rows(idx_ref, tab_ref, buf_ref, t):
    ids = idx_ref.at[0, 0, pl.ds(t * SEL, SEL)]
    for k in range(SEL):
        r = pl.multiple_of(ids[k], CHUNKS)
        buf_ref[pl.ds(k, CHUNKS, stride=CHUNK_STRIDE), :] = tab_ref[pl.ds(r, CHUNKS), :]


def _chunk(buf_ref, c):
    return _unpack(buf_ref[c * CHUNK_STRIDE:c * CHUNK_STRIDE + SEL, :])


def _lane_replicated_sum(a):
    ones = jnp.ones((128, 128), BF16)
    hi, lo = _split_bf16(a)
    return _dot(hi, ones) + _dot(lo, ones)


def _diag_mask():
    return (lax.broadcasted_iota(I32, (SEL, 128), 0) == lax.broadcasted_iota(I32, (SEL, 128), 1))


def _pairwise_tokens(tb, gather, reduce):
    gather(0, 0)

    def pair(i, carry):
        t = 2 * i
        gather(t + 1, 1)
        reduce(t, 0)
        gather(jnp.minimum(t + 2, tb - 1), 0)
        reduce(t + 1, 1)
        return carry

    lax.fori_loop(0, tb // 2, pair, 0)


def _peer_u_kernel(idx_ref, x_ref, gate_ref, tab_ref, w_ref, buf0_ref, buf1_ref, act_ref):
    tb = x_ref.shape[0]
    bufs = (buf0_ref, buf1_ref)
    diag = _diag_mask()

    def gather(t, slot):
        _gather_rows(idx_ref, tab_ref, bufs[slot], t)

    def reduce(t, slot):
        total = None
        for c in range(CHUNKS):
            lo, hi = _chunk(bufs[slot], c)
            term = lo * x_ref[t, c:c + 1, :] + hi * x_ref[t, c + CHUNKS:c + CHUNKS + 1, :]
            total = term if total is None else total + term
        act_rep = _lane_replicated_sum(total)
        act_ref[pl.ds(t, 1), :] = jnp.sum(jnp.where(diag, act_rep, 0.0), axis=0, keepdims=True)

    _pairwise_tokens(tb, gather, reduce)
    act = act_ref[...]
    gelu = 0.5 * act * (1.0 + lax.erf(act * (2.0 ** -0.5)))
    w_ref[...] = gate_ref[...] * gelu


def _gather_scratch():
    return pltpu.VMEM((CHUNKS * CHUNK_STRIDE, 128), I32)


def _peer_u(idx3, x3, gates, tab, *, tb):
    T = x3.shape[0]
    return pl.pallas_call(
        _peer_u_kernel,
        grid=(T // tb,),
        in_specs=[
            pl.BlockSpec((1, 1, tb * SEL), lambda i: (i, 0, 0), memory_space=pltpu.SMEM),
            pl.BlockSpec((tb, 8, 128), lambda i: (i, 0, 0)),
            pl.BlockSpec((tb, SEL), lambda i: (i, 0)),
            pl.BlockSpec(memory_space=pltpu.VMEM),
        ],
        out_specs=pl.BlockSpec((tb, SEL), lambda i: (i, 0)),
        out_shape=jax.ShapeDtypeStruct((T, SEL), F32),
        scratch_shapes=[_gather_scratch(), _gather_scratch(), pltpu.VMEM((tb, SEL), F32)],
        compiler_params=pltpu.CompilerParams(
            dimension_semantics=("arbitrary",), vmem_limit_bytes=VMEM_LIMIT),
        name="peer_u",
    )(idx3, x3, gates, tab)


def _peer_v_kernel(idx_ref, w_ref, h_ref, tab_ref, y_ref, buf0_ref, buf1_ref):
    tb = h_ref.shape[0]
    bufs = (buf0_ref, buf1_ref)
    diag = _diag_mask()

    def gather(t, slot):
        _gather_rows(idx_ref, tab_ref, bufs[slot], t)

    def reduce(t, slot):
        w_rep = _lane_replicated_sum(jnp.where(diag, w_ref[pl.ds(t, 1), :], 0.0))
        lows, highs = [], []
        for c in range(CHUNKS):
            lo, hi = _chunk(bufs[slot], c)
            lows.append(jnp.sum(lo * w_rep, axis=0, keepdims=True))
            highs.append(jnp.sum(hi * w_rep, axis=0, keepdims=True))
        y_ref[t] = h_ref[t] + jnp.concatenate(lows + highs, axis=0)

    _pairwise_tokens(tb, gather, reduce)


def _peer_v(idx3, w, h3, tab, *, tb):
    T = h3.shape[0]
    return pl.pallas_call(
        _peer_v_kernel,
        grid=(T // tb,),
        in_specs=[
            pl.BlockSpec((1, 1, tb * SEL), lambda i: (i, 0, 0), memory_space=pltpu.SMEM),
            pl.BlockSpec((tb, SEL), lambda i: (i, 0)),
            pl.BlockSpec((tb, 8, 128), lambda i: (i, 0, 0)),
            pl.BlockSpec(memory_space=pltpu.VMEM),
        ],
        out_specs=pl.BlockSpec((tb, 8, 128), lambda i: (i, 0, 0)),
        out_shape=jax.ShapeDtypeStruct(h3.shape, F32),
        scratch_shapes=[_gather_scratch(), _gather_scratch()],
        compiler_params=pltpu.CompilerParams(
            dimension_semantics=("arbitrary",), vmem_limit_bytes=VMEM_LIMIT),
        name="peer_v",
    )(idx3, w, h3, tab)


def _pack_table(tab):
    n, d = tab.shape
    bits = lax.bitcast_convert_type(tab.astype(BF16), jnp.uint16).astype(jnp.uint32)
    word = bits[:, :d // 2] | (bits[:, d // 2:] << 16)
    return lax.bitcast_convert_type(word, I32).reshape(n * 4, 128)


def _layer(x2, p, *, batch, seq, tm_in, tm_mix, tm_topk, tb):
    T, D = x2.shape
    row = lambda a: a.reshape(1, -1)
    w_in_bf = p["w_in"].astype(BF16)
    gq_t = row(jnp.tile(p["q_norm_g"], ATTN_HEADS))
    gk_t = row(jnp.tile(p["k_norm_g"], ATTN_HEADS))
    qh, kh, vh, hg = _inproj(x2, row(p["norm_mix_g"]), w_in_bf, gq_t, gk_t, tm=tm_in)
    o = _attention(qh, kh, vh, batch=batch, seq=seq)

    wq = p["peer_wq"]
    wq_hi = wq.astype(BF16)
    wq_lo = (wq - wq_hi.astype(F32)).astype(BF16)
    h, xn, pq = _mix(x2, o, hg, p["conv_w"], row(p["conv_b"]), row(p["conv_ln_g"]),
                     row(p["conv_ln_b"]), row(p["attn_out_g"]), p["w_out"].astype(BF16),
                     row(p["norm_ffn_g"]), wq_hi, wq_lo, seq=seq, tm=tm_mix)

    idx_t, gate_t = _topk(pq, p["peer_k1"], p["peer_k2"], tm=tm_topk)
    idx3 = idx_t.T.reshape(T // tb, 1, tb * SEL)
    gates = gate_t.T
    w = _peer_u(idx3, xn.reshape(T, 8, 128), gates, _pack_table(p["peer_u"]), tb=tb)
    y3 = _peer_v(idx3, w, h.reshape(T, 8, 128), _pack_table(p["peer_v"]), tb=tb)
    return y3.reshape(T, D)


def kernel(x, norm_mix_g, w_in, q_norm_g, k_norm_g, attn_out_g, conv_w, conv_b, conv_ln_g,
           conv_ln_b, w_out, norm_ffn_g, peer_wq, peer_k1, peer_k2, peer_u, peer_v):
    batch, seq, d = x.shape
    stacked = dict(norm_mix_g=norm_mix_g, w_in=w_in, q_norm_g=q_norm_g, k_norm_g=k_norm_g,
                   attn_out_g=attn_out_g, conv_w=conv_w, conv_b=conv_b, conv_ln_g=conv_ln_g,
                   conv_ln_b=conv_ln_b, w_out=w_out, norm_ffn_g=norm_ffn_g, peer_wq=peer_wq,
                   peer_k1=peer_k1, peer_k2=peer_k2, peer_u=peer_u, peer_v=peer_v)
    h = x.reshape(batch * seq, d)
    for l in range(w_in.shape[0]):
        p = {name: a[l] for name, a in stacked.items()}
        h = _layer(h, p, batch=batch, seq=seq, tm_in=256, tm_mix=256,
                   tm_topk=256, tb=32)
    return h.reshape(batch, seq, d)
```

```python
import functools

import jax
import jax.numpy as jnp
from jax import lax
from jax.experimental import pallas as pl
from jax.experimental.pallas import tpu as pltpu

F32 = jnp.float32
BF16 = jnp.bfloat16
I32 = jnp.int32

EPS = 1e-6
HEAD_DIM = 64
ATTN_HEADS = 8
ATTN_WIDTH = ATTN_HEADS * HEAD_DIM
CONV_KERNEL = 31
CONV_HALO = 32
PEER_HEADS = 8
PEER_HALF = 64
N_KEYS = 128
PEER_TOPK = 16
SEL = PEER_HEADS * PEER_TOPK
EXP_UNDERFLOW = -88.0
HI_MASK = -65536

VMEM_LIMIT = 56 * 1024 * 1024


def _split_bf16(a):
    hi = a.astype(BF16)
    lo = (a - hi.astype(F32)).astype(BF16)
    return hi, lo


def _dot(a, b):
    return jnp.dot(a, b, preferred_element_type=F32)


def _dot_nt(a, b):
    return lax.dot_general(a, b, (((1,), (1,)), ((), ())), preferred_element_type=F32)


def _sigmoid(x):
    return 1.0 / (1.0 + jnp.exp(-x))


def _inproj_kernel(x_ref, g_ref, w_ref, gq_ref, gk_ref, q_ref, k_ref, v_ref, hg_ref):
    x = x_ref[...]
    ms = jnp.mean(x * x, axis=-1, keepdims=True)
    xn = (x * lax.rsqrt(ms + EPS) * g_ref[...]).astype(BF16)
    proj = _dot(xn, w_ref[...])

    r = lax.broadcasted_iota(I32, (ATTN_WIDTH, ATTN_WIDTH), 0) // HEAD_DIM
    c = lax.broadcasted_iota(I32, (ATTN_WIDTH, ATTN_WIDTH), 1) // HEAD_DIM
    same_head = (r == c).astype(BF16)

    def head_norm(t, g):
        hi, lo = _split_bf16(t * t)
        msq = (_dot(hi, same_head) + _dot(lo, same_head)) * (1.0 / HEAD_DIM)
        return t * lax.rsqrt(msq + EPS) * g

    q = head_norm(proj[:, :ATTN_WIDTH], gq_ref[...]) * (HEAD_DIM ** -0.5)
    k = head_norm(proj[:, ATTN_WIDTH:2 * ATTN_WIDTH], gk_ref[...])
    v = proj[:, 2 * ATTN_WIDTH:3 * ATTN_WIDTH]
    for h in range(ATTN_HEADS):
        sl = slice(h * HEAD_DIM, (h + 1) * HEAD_DIM)
        q_ref[h] = q[:, sl].astype(BF16)
        k_ref[h] = k[:, sl].astype(BF16)
        v_ref[h] = v[:, sl].astype(BF16)
    cw = (proj.shape[1] - 3 * ATTN_WIDTH) // 2
    a = proj[:, 3 * ATTN_WIDTH:3 * ATTN_WIDTH + cw]
    gate = proj[:, 3 * ATTN_WIDTH + cw:]
    hg_ref[...] = a * _sigmoid(gate)


def _inproj(x2, g, w_bf, gq_t, gk_t, *, tm):
    T, D = x2.shape
    E = w_bf.shape[1]
    cw = (E - 3 * ATTN_WIDTH) // 2
    head_shape = jax.ShapeDtypeStruct((ATTN_HEADS, T, HEAD_DIM), BF16)
    head_spec = pl.BlockSpec((ATTN_HEADS, tm, HEAD_DIM), lambda i: (0, i, 0))
    return pl.pallas_call(
        _inproj_kernel,
        grid=(T // tm,),
        in_specs=[
            pl.BlockSpec((tm, D), lambda i: (i, 0)),
            pl.BlockSpec((1, D), lambda i: (0, 0)),
            pl.BlockSpec((D, E), lambda i: (0, 0)),
            pl.BlockSpec((1, ATTN_WIDTH), lambda i: (0, 0)),
            pl.BlockSpec((1, ATTN_WIDTH), lambda i: (0, 0)),
        ],
        out_specs=[head_spec, head_spec, head_spec,
                   pl.BlockSpec((tm, cw), lambda i: (i, 0))],
        out_shape=[head_shape, head_shape, head_shape,
                   jax.ShapeDtypeStruct((T, cw), F32)],
        compiler_params=pltpu.CompilerParams(
            dimension_semantics=("arbitrary",), vmem_limit_bytes=VMEM_LIMIT),
        name="inproj",
    )(x2, g, w_bf, gq_t, gk_t)


def _attn_span(q, k, v, offset, carry, suffix):
    tw = suffix.shape[0]
    z = _dot_nt(q, k)
    sp = jnp.maximum(z, 0.0) + jnp.log(1.0 + jnp.exp(-jnp.abs(z)))
    col_minus_row = (lax.broadcasted_iota(I32, z.shape, 1)
                     - lax.broadcasted_iota(I32, z.shape, 0))
    mask = col_minus_row < offset
    log_keep = jnp.where(mask, -sp, 0.0)
    later = []
    for s in reversed(range(z.shape[1] // tw)):
        lk = log_keep[:, s * tw:(s + 1) * tw]
        hi, lo = _split_bf16(lk)
        later.append(carry + (_dot(hi, suffix) + _dot(lo, suffix)))
        carry = carry + jnp.sum(lk, axis=-1, keepdims=True)
    later = jnp.concatenate(later[::-1], axis=-1)
    att = jnp.where(mask, jnp.exp(z - sp + later), 0.0)
    return carry, _dot(att.astype(BF16), v)


def _attn_kernel(q_ref, k_ref, v_ref, o_ref, carry_ref, acc_ref, *, rows, span, tw):
    g = pl.program_id(2)
    hp = q_ref.shape[0]
    suffix = (lax.broadcasted_iota(I32, (tw, tw), 0)
              > lax.broadcasted_iota(I32, (tw, tw), 1)).astype(BF16)

    start = pl.multiple_of(jnp.maximum(g * rows - (span - rows), 0), rows)
    cmax = None
    for hh in range(hp):
        carry, acc = _attn_span(q_ref[hh], k_ref[hh, pl.ds(start, span), :],
                                v_ref[hh, pl.ds(start, span), :], g * rows - start,
                                jnp.zeros((rows, 1), F32), suffix)
        carry_ref[hh] = carry
        acc_ref[hh] = acc
        cmax = carry if cmax is None else jnp.maximum(cmax, carry)

    @pl.when(jnp.max(cmax) > EXP_UNDERFLOW)
    def _():
        for hh in range(hp):

            def cond(st):
                j, carry, _ = st
                return jnp.logical_and(j >= 0, jnp.max(carry) > EXP_UNDERFLOW)

            def body(st, hh=hh):
                j, carry, acc = st
                ks = pl.multiple_of(j * tw, tw)
                carry, out = _attn_span(q_ref[hh], k_ref[hh, pl.ds(ks, tw), :],
                                        v_ref[hh, pl.ds(ks, tw), :], rows + tw, carry, suffix)
                return j - 1, carry, acc + out

            init = (start // tw - 1, carry_ref[hh], acc_ref[hh])
            acc_ref[hh] = lax.while_loop(cond, body, init)[2]

    o_ref[...] = jnp.concatenate([acc_ref[hh] for hh in range(hp)], axis=-1)


def _attention(qh, kh, vh, *, batch, seq, rows=256, span=512, tw=256, heads_per_step=2):
    H, T, hd = qh.shape
    ng = seq // rows
    hp = heads_per_step
    return pl.pallas_call(
        functools.partial(_attn_kernel, rows=rows, span=span, tw=tw),
        grid=(H // hp, batch, ng),
        in_specs=[
            pl.BlockSpec((hp, rows, hd), lambda h, b, i: (h, b * ng + i, 0)),
            pl.BlockSpec((hp, seq, hd), lambda h, b, i: (h, b, 0)),
            pl.BlockSpec((hp, seq, hd), lambda h, b, i: (h, b, 0)),
        ],
        out_specs=pl.BlockSpec((rows, hp * hd), lambda h, b, i: (b * ng + i, h)),
        out_shape=jax.ShapeDtypeStruct((T, H * hd), F32),
        scratch_shapes=[pltpu.VMEM((hp, rows, 1), F32),
                        pltpu.VMEM((hp, rows, hd), F32)],
        compiler_params=pltpu.CompilerParams(
            dimension_semantics=("arbitrary", "arbitrary", "arbitrary"),
            vmem_limit_bytes=VMEM_LIMIT),
        name="sb_attention",
    )(qh, kh, vh)


def _mix_kernel(x_ref, o_ref, hg_ref, hgp_ref, cw_ref, cb_ref, lg_ref, lb_ref, ag_ref,
                wo_ref, g2_ref, wqh_ref, wql_ref, h_ref, xn_ref, qh_ref, ext_ref,
                *, blocks_per_seq):
    tm = x_ref.shape[0]
    first = (pl.program_id(0) % blocks_per_seq) == 0
    ext_ref[0:CONV_HALO, :] = jnp.where(first, 0.0, hgp_ref[...])
    ext_ref[CONV_HALO:, :] = hg_ref[...]
    conv = jnp.zeros(hg_ref.shape, F32)
    for j in range(CONV_KERNEL):
        off = CONV_HALO - (CONV_KERNEL - 1) + j
        conv = conv + cw_ref[j:j + 1, :] * ext_ref[pl.ds(off, tm), :]
    conv = conv + cb_ref[...]
    mu = jnp.mean(conv, axis=-1, keepdims=True)
    xc = conv - mu
    var = jnp.mean(xc * xc, axis=-1, keepdims=True)
    y = xc * lax.rsqrt(var + EPS) * lg_ref[...] + lb_ref[...]
    o_conv = y * _sigmoid(y)

    o = o_ref[...]
    o_attn = o * lax.rsqrt(jnp.mean(o * o, axis=-1, keepdims=True) + EPS) * ag_ref[...]
    mixed = jnp.concatenate([o_attn, o_conv], axis=-1).astype(BF16)
    h = x_ref[...] + _dot(mixed, wo_ref[...])
    h_ref[...] = h

    xn = h * lax.rsqrt(jnp.mean(h * h, axis=-1, keepdims=True) + EPS) * g2_ref[...]
    xn_ref[...] = xn
    hi, lo = _split_bf16(xn)
    wqh = wqh_ref[...]
    qh_ref[...] = _dot(hi, wqh) + _dot(lo, wqh) + _dot(hi, wql_ref[...])


def _mix(x2, o, hg, conv_w, conv_b, ln_g, ln_b, attn_g, wo_bf, g2, wq_hi, wq_lo, *, seq, tm):
    T, D = x2.shape
    cwid = hg.shape[1]
    E = wq_hi.shape[1]
    halo_per_block = tm // CONV_HALO
    row = lambda i: (i, 0)
    const = lambda i: (0, 0)
    return pl.pallas_call(
        functools.partial(_mix_kernel, blocks_per_seq=seq // tm),
        grid=(T // tm,),
        in_specs=[
            pl.BlockSpec((tm, D), row),
            pl.BlockSpec((tm, o.shape[1]), row),
            pl.BlockSpec((tm, cwid), row),
            pl.BlockSpec((CONV_HALO, cwid),
                         lambda i: (jnp.maximum(i * halo_per_block - 1, 0), 0)),
            pl.BlockSpec((CONV_KERNEL, cwid), const),
            pl.BlockSpec((1, cwid), const),
            pl.BlockSpec((1, cwid), const),
            pl.BlockSpec((1, cwid), const),
            pl.BlockSpec((1, o.shape[1]), const),
            pl.BlockSpec(wo_bf.shape, const),
            pl.BlockSpec((1, D), const),
            pl.BlockSpec(wq_hi.shape, const),
            pl.BlockSpec(wq_lo.shape, const),
        ],
        out_specs=[pl.BlockSpec((tm, D), row), pl.BlockSpec((tm, D), row),
                   pl.BlockSpec((tm, E), row)],
        out_shape=[jax.ShapeDtypeStruct((T, D), F32), jax.ShapeDtypeStruct((T, D), F32),
                   jax.ShapeDtypeStruct((T, E), F32)],
        scratch_shapes=[pltpu.VMEM((tm + CONV_HALO, cwid), F32)],
        compiler_params=pltpu.CompilerParams(
            dimension_semantics=("arbitrary",), vmem_limit_bytes=VMEM_LIMIT),
        name="mix_outproj",
    )(x2, o, hg, hg, conv_w, conv_b, ln_g, ln_b, attn_g, wo_bf, g2, wq_hi, wq_lo)


def _top16(s, pos=None, payload=None):
    if pos is None:
        pos = lax.broadcasted_iota(I32, s.shape, 0).astype(F32)
    vals, outs = [], []
    for _ in range(PEER_TOPK):
        m = jnp.max(s, axis=0, keepdims=True)
        idx = jnp.min(jnp.where(s == m, pos, 1e9), axis=0, keepdims=True)
        sel = pos == idx
        if payload is None:
            outs.append(idx)
        else:
            outs.append(jnp.sum(jnp.where(sel, payload, 0.0), axis=0, keepdims=True))
        s = jnp.where(sel, -jnp.inf, s)
        vals.append(m)
    return jnp.concatenate(vals, axis=0), jnp.concatenate(outs, axis=0)


def _dot3_nt(a, b):
    ah, al = _split_bf16(a)
    bh, bl = _split_bf16(b)
    return _dot_nt(ah, bh) + _dot_nt(al, bh) + _dot_nt(ah, bl)


def _pair_candidates(v1, i1, v2, i2):
    r8 = lax.broadcasted_iota(I32, (8, v1.shape[1]), 0).astype(F32)
    sc, ex, ps = [], [], []

    def add(s1, e1, s2, e2, p, keep=None):
        s = s1 + s2
        sc.append(s if keep is None else jnp.where(keep, s, -jnp.inf))
        ex.append(e1 * float(N_KEYS) + e2)
        ps.append(p if keep is None else jnp.where(keep, p, 2e9))

    for a, half in ((0, 0), (0, 1), (1, 0), (2, 0), (3, 0)):
        b = slice(8 * half, 8 * half + 8)
        add(v1[a:a + 1], i1[a:a + 1], v2[b], i2[b], r8 + float(a * PEER_TOPK + 8 * half))
    for b in range(3):
        add(v1[0:8], i1[0:8], v2[b:b + 1], i2[b:b + 1], r8 * float(PEER_TOPK) + float(b),
            keep=r8 >= 4.0)
    add(v1[8:16], i1[8:16], v2[0:1], i2[0:1], (r8 + 8.0) * float(PEER_TOPK))
    return (jnp.concatenate(sc, axis=0), jnp.concatenate(ex, axis=0),
            jnp.concatenate(ps, axis=0))


def _topk_kernel(qh_ref, k1_ref, k2_ref, idx_ref, gate_ref):
    all_experts, all_gates = [], []
    for h in range(PEER_HEADS):
        base = h * 2 * PEER_HALF
        q1 = qh_ref[:, base:base + PEER_HALF]
        q2 = qh_ref[:, base + PEER_HALF:base + 2 * PEER_HALF]
        s1 = _dot3_nt(k1_ref[h], q1)
        s2 = _dot3_nt(k2_ref[h], q2)
        v1, i1 = _top16(s1)
        v2, i2 = _top16(s2)
        cand, cexp, cpos = _pair_candidates(v1, i1, v2, i2)
        top_s, experts = _top16(cand, pos=cpos, payload=cexp)
        e = jnp.exp(top_s - top_s[0:1, :])
        all_experts.append(experts)
        all_gates.append(e / jnp.sum(e, axis=0, keepdims=True))
    idx_ref[...] = jnp.concatenate(all_experts, axis=0).T.astype(I32) * CHUNKS
    gate_ref[...] = jnp.concatenate(all_gates, axis=0).T


def _topk(qh, k1, k2, *, tm):
    T, E = qh.shape
    out_spec = pl.BlockSpec((tm, SEL), lambda i: (i, 0))
    return pl.pallas_call(
        _topk_kernel,
        grid=(T // tm,),
        in_specs=[
            pl.BlockSpec((tm, E), lambda i: (i, 0)),
            pl.BlockSpec(k1.shape, lambda i: (0, 0, 0)),
            pl.BlockSpec(k2.shape, lambda i: (0, 0, 0)),
        ],
        out_specs=[out_spec, out_spec],
        out_shape=[jax.ShapeDtypeStruct((T, SEL), I32), jax.ShapeDtypeStruct((T, SEL), F32)],
        compiler_params=pltpu.CompilerParams(
            dimension_semantics=("arbitrary",), vmem_limit_bytes=VMEM_LIMIT),
        name="peer_topk",
    )(qh, k1, k2)


CHUNKS = 4
CHUNK_STRIDE = SEL + 8


def _unpack(word):
    lo = lax.bitcast_convert_type(word << 16, F32)
    hi = lax.bitcast_convert_type(word & HI_MASK, F32)
    return lo, hi


def _gather_group(idx_ref, tab_ref, buf_ref, t0, slots, part=0, parts=1):
    ids = [idx_ref.at[0, 0, pl.ds((t0 + j) * SEL, SEL)] for j in range(len(slots))]
    for k in range(part * SEL // parts, (part + 1) * SEL // parts):
        for j, s in enumerate(slots):
            r = pl.multiple_of(ids[j][k], CHUNKS)
            buf_ref[s, pl.ds(k, CHUNKS, stride=CHUNK_STRIDE), :] = tab_ref[pl.ds(r, CHUNKS), :]


def _cols(c):
    return slice(c * 128, (c + 1) * 128)


def _chunk(buf_ref, s, c):
    return _unpack(buf_ref[s, c * CHUNK_STRIDE:c * CHUNK_STRIDE + SEL, :])


def _lane_replicated_sum(a):
    ones = jnp.ones((128, 128), BF16)
    hi, lo = _split_bf16(a)
    return _dot(hi, ones) + _dot(lo, ones)


def _diag_mask():
    return (lax.broadcasted_iota(I32, (SEL, 128), 0) == lax.broadcasted_iota(I32, (SEL, 128), 1))


GROUP = 4


def _grouped_tokens(tb, gather, reduce):
    half = (tuple(range(GROUP)), tuple(range(GROUP, 2 * GROUP)))
    gather(0, half[0], 0, 1)

    def trip(i, carry):
        t = 2 * GROUP * i
        for j in range(GROUP):
            gather(t + GROUP, half[1], j, GROUP)
            reduce(t + j, half[0][j])
        nxt = jnp.minimum(t + 2 * GROUP, tb - GROUP)
        for j in range(GROUP):
            gather(nxt, half[0], j, GROUP)
            reduce(t + GROUP + j, half[1][j])
        return carry

    lax.fori_loop(0, tb // (2 * GROUP), trip, 0)


def _peer_u_kernel(idx_ref, x_ref, gate_ref, tab_ref, w_ref, buf_ref, act_ref):
    tb = x_ref.shape[0]
    diag = _diag_mask()

    def gather(t0, slots, part, parts):
        _gather_group(idx_ref, tab_ref, buf_ref, t0, slots, part, parts)

    def reduce(t, slot):
        total = None
        for c in range(CHUNKS):
            lo, hi = _chunk(buf_ref, slot, c)
            term = lo * x_ref[t, c:c + 1, :] + hi * x_ref[t, c + CHUNKS:c + CHUNKS + 1, :]
            total = term if total is None else total + term
        act_rep = _lane_replicated_sum(total)
        act_ref[pl.ds(t, 1), :] = jnp.sum(jnp.where(diag, act_rep, 0.0), axis=0, keepdims=True)

    _grouped_tokens(tb, gather, reduce)
    act = act_ref[...]
    gelu = 0.5 * act * (1.0 + lax.erf(act * (2.0 ** -0.5)))
    w_ref[...] = gate_ref[...] * gelu


def _gather_scratch():
    return pltpu.VMEM((2 * GROUP, CHUNKS * CHUNK_STRIDE, 128), I32)


def _peer_u(idx3, x3, gates, tab, *, tb):
    T = x3.shape[0]
    return pl.pallas_call(
        _peer_u_kernel,
        grid=(T // tb,),
        in_specs=[
            pl.BlockSpec((1, 1, tb * SEL), lambda i: (i, 0, 0), memory_space=pltpu.SMEM),
            pl.BlockSpec((tb, 2 * CHUNKS, 128), lambda i: (i, 0, 0)),
            pl.BlockSpec((tb, SEL), lambda i: (i, 0)),
            pl.BlockSpec(memory_space=pltpu.VMEM),
        ],
        out_specs=pl.BlockSpec((tb, SEL), lambda i: (i, 0)),
        out_shape=jax.ShapeDtypeStruct((T, SEL), F32),
        scratch_shapes=[_gather_scratch(), pltpu.VMEM((tb, SEL), F32)],
        compiler_params=pltpu.CompilerParams(
            dimension_semantics=("arbitrary",), vmem_limit_bytes=VMEM_LIMIT),
        name="peer_u",
    )(idx3, x3, gates, tab)


def _peer_v_kernel(idx_ref, w_ref, h_ref, tab_ref, y_ref, buf_ref):
    tb = h_ref.shape[0]
    diag = _diag_mask()

    def gather(t0, slots, part, parts):
        _gather_group(idx_ref, tab_ref, buf_ref, t0, slots, part, parts)

    def reduce(t, slot):
        w_rep = _lane_replicated_sum(jnp.where(diag, w_ref[pl.ds(t, 1), :], 0.0))
        lows, highs = [], []
        for c in range(CHUNKS):
            lo, hi = _chunk(buf_ref, slot, c)
            lows.append(jnp.sum(lo * w_rep, axis=0, keepdims=True))
            highs.append(jnp.sum(hi * w_rep, axis=0, keepdims=True))
        y_ref[pl.ds(t, 1), :] = h_ref[pl.ds(t, 1), :] + jnp.concatenate(lows + highs, axis=-1)

    _grouped_tokens(tb, gather, reduce)


def _peer_v(idx3, w, h2, tab, *, tb):
    T, D = h2.shape
    return pl.pallas_call(
        _peer_v_kernel,
        grid=(T // tb,),
        in_specs=[
            pl.BlockSpec((1, 1, tb * SEL), lambda i: (i, 0, 0), memory_space=pltpu.SMEM),
            pl.BlockSpec((tb, SEL), lambda i: (i, 0)),
            pl.BlockSpec((tb, D), lambda i: (i, 0)),
            pl.BlockSpec(memory_space=pltpu.VMEM),
        ],
        out_specs=pl.BlockSpec((tb, D), lambda i: (i, 0)),
        out_shape=jax.ShapeDtypeStruct(h2.shape, F32),
        scratch_shapes=[_gather_scratch()],
        compiler_params=pltpu.CompilerParams(
            dimension_semantics=("arbitrary",), vmem_limit_bytes=VMEM_LIMIT),
        name="peer_v",
    )(idx3, w, h2, tab)


def _pack_table(tab):
    n, d = tab.shape
    bits = lax.bitcast_convert_type(tab.astype(BF16), jnp.uint16).astype(jnp.uint32)
    word = bits[:, :d // 2] | (bits[:, d // 2:] << 16)
    return lax.bitcast_convert_type(word, I32).reshape(n * 4, 128)


def _layer(x2, p, *, batch, seq, tm_in, tm_mix, tm_topk, tb):
    T, D = x2.shape
    row = lambda a: a.reshape(1, -1)
    w_in_bf = p["w_in"].astype(BF16)
    gq_t = row(jnp.tile(p["q_norm_g"], ATTN_HEADS))
    gk_t = row(jnp.tile(p["k_norm_g"], ATTN_HEADS))
    qh, kh, vh, hg = _inproj(x2, row(p["norm_mix_g"]), w_in_bf, gq_t, gk_t, tm=tm_in)
    o = _attention(qh, kh, vh, batch=batch, seq=seq)

    wq = p["peer_wq"]
    wq_hi = wq.astype(BF16)
    wq_lo = (wq - wq_hi.astype(F32)).astype(BF16)
    h, xn, pq = _mix(x2, o, hg, p["conv_w"], row(p["conv_b"]), row(p["conv_ln_g"]),
                     row(p["conv_ln_b"]), row(p["attn_out_g"]), p["w_out"].astype(BF16),
                     row(p["norm_ffn_g"]), wq_hi, wq_lo, seq=seq, tm=tm_mix)

    idx, gates = _topk(pq, p["peer_k1"], p["peer_k2"], tm=tm_topk)
    idx3 = idx.reshape(T // tb, 1, tb * SEL)
    w = _peer_u(idx3, xn.reshape(T, 2 * CHUNKS, 128), gates, _pack_table(p["peer_u"]), tb=tb)
    return _peer_v(idx3, w, h, _pack_table(p["peer_v"]), tb=tb)


def kernel(x, norm_mix_g, w_in, q_norm_g, k_norm_g, attn_out_g, conv_w, conv_b, conv_ln_g,
           conv_ln_b, w_out, norm_ffn_g, peer_wq, peer_k1, peer_k2, peer_u, peer_v):
    batch, seq, d = x.shape
    stacked = dict(norm_mix_g=norm_mix_g, w_in=w_in, q_norm_g=q_norm_g, k_norm_g=k_norm_g,
                   attn_out_g=attn_out_g, conv_w=conv_w, conv_b=conv_b, conv_ln_g=conv_ln_g,
                   conv_ln_b=conv_ln_b, w_out=w_out, norm_ffn_g=norm_ffn_g, peer_wq=peer_wq,
                   peer_k1=peer_k1, peer_k2=peer_k2, peer_u=peer_u, peer_v=peer_v)
    h = x.reshape(batch * seq, d)
    for l in range(w_in.shape[0]):
        p = {name: a[l] for name, a in stacked.items()}
        h = _layer(h, p, batch=batch, seq=seq, tm_in=256, tm_mix=256,
                   tm_topk=256, tb=64)
    return h.reshape(batch, seq, d)
```

```python
import functools

import jax
import jax.numpy as jnp
from jax import lax
from jax.experimental import pallas as pl
from jax.experimental.pallas import tpu as pltpu

F32 = jnp.float32
BF16 = jnp.bfloat16
I32 = jnp.int32

EPS = 1e-6
HEAD_DIM = 64
ATTN_HEADS = 8
ATTN_WIDTH = ATTN_HEADS * HEAD_DIM
CONV_KERNEL = 31
CONV_HALO = 32
PEER_HEADS = 8
PEER_HALF = 64
N_KEYS = 128
PEER_TOPK = 16
SEL = PEER_HEADS * PEER_TOPK
EXP_UNDERFLOW = -88.0
HI_MASK = -65536

VMEM_LIMIT = 56 * 1024 * 1024


def _split_bf16(a):
    hi = a.astype(BF16)
    lo = (a - hi.astype(F32)).astype(BF16)
    return hi, lo


def _dot(a, b):
    return jnp.dot(a, b, preferred_element_type=F32)


def _dot_nt(a, b):
    return lax.dot_general(a, b, (((1,), (1,)), ((), ())), preferred_element_type=F32)


def _sigmoid(x):
    return 1.0 / (1.0 + jnp.exp(-x))


def _inproj_kernel(x_ref, g_ref, w_ref, gq_ref, gk_ref, q_ref, k_ref, v_ref, hg_ref):
    x = x_ref[...]
    ms = jnp.mean(x * x, axis=-1, keepdims=True)
    xn = (x * lax.rsqrt(ms + EPS) * g_ref[...]).astype(BF16)
    proj = _dot(xn, w_ref[...])

    r = lax.broadcasted_iota(I32, (ATTN_WIDTH, ATTN_WIDTH), 0) // HEAD_DIM
    c = lax.broadcasted_iota(I32, (ATTN_WIDTH, ATTN_WIDTH), 1) // HEAD_DIM
    same_head = (r == c).astype(BF16)

    def head_norm(t, g):
        hi, lo = _split_bf16(t * t)
        msq = (_dot(hi, same_head) + _dot(lo, same_head)) * (1.0 / HEAD_DIM)
        return t * lax.rsqrt(msq + EPS) * g

    q = head_norm(proj[:, :ATTN_WIDTH], gq_ref[...]) * (HEAD_DIM ** -0.5)
    k = head_norm(proj[:, ATTN_WIDTH:2 * ATTN_WIDTH], gk_ref[...])
    v = proj[:, 2 * ATTN_WIDTH:3 * ATTN_WIDTH]
    for h in range(ATTN_HEADS):
        sl = slice(h * HEAD_DIM, (h + 1) * HEAD_DIM)
        q_ref[h] = q[:, sl].astype(BF16)
        k_ref[h] = k[:, sl].astype(BF16)
        v_ref[h] = v[:, sl].astype(BF16)
    cw = (proj.shape[1] - 3 * ATTN_WIDTH) // 2
    a = proj[:, 3 * ATTN_WIDTH:3 * ATTN_WIDTH + cw]
    gate = proj[:, 3 * ATTN_WIDTH + cw:]
    hg_ref[...] = a * _sigmoid(gate)


def _inproj(x2, g, w_bf, gq_t, gk_t, *, tm):
    T, D = x2.shape
    E = w_bf.shape[1]
    cw = (E - 3 * ATTN_WIDTH) // 2
    head_shape = jax.ShapeDtypeStruct((ATTN_HEADS, T, HEAD_DIM), BF16)
    head_spec = pl.BlockSpec((ATTN_HEADS, tm, HEAD_DIM), lambda i: (0, i, 0))
    return pl.pallas_call(
        _inproj_kernel,
        grid=(T // tm,),
        in_specs=[
            pl.BlockSpec((tm, D), lambda i: (i, 0)),
            pl.BlockSpec((1, D), lambda i: (0, 0)),
            pl.BlockSpec((D, E), lambda i: (0, 0)),
            pl.BlockSpec((1, ATTN_WIDTH), lambda i: (0, 0)),
            pl.BlockSpec((1, ATTN_WIDTH), lambda i: (0, 0)),
        ],
        out_specs=[head_spec, head_spec, head_spec,
                   pl.BlockSpec((tm, cw), lambda i: (i, 0))],
        out_shape=[head_shape, head_shape, head_shape,
                   jax.ShapeDtypeStruct((T, cw), F32)],
        compiler_params=pltpu.CompilerParams(
            dimension_semantics=("arbitrary",), vmem_limit_bytes=VMEM_LIMIT),
        name="inproj",
    )(x2, g, w_bf, gq_t, gk_t)


def _attn_span(q, k, v, offset, carry, suffix):
    tw = suffix.shape[0]
    z = _dot_nt(q, k)
    sp = jnp.maximum(z, 0.0) + jnp.log(1.0 + jnp.exp(-jnp.abs(z)))
    col_minus_row = (lax.broadcasted_iota(I32, z.shape, 1)
                     - lax.broadcasted_iota(I32, z.shape, 0))
    mask = col_minus_row < offset
    log_keep = jnp.where(mask, -sp, 0.0)
    later = []
    for s in reversed(range(z.shape[1] // tw)):
        lk = log_keep[:, s * tw:(s + 1) * tw]
        hi, lo = _split_bf16(lk)
        later.append(carry + (_dot(hi, suffix) + _dot(lo, suffix)))
        carry = carry + jnp.sum(lk, axis=-1, keepdims=True)
    later = jnp.concatenate(later[::-1], axis=-1)
    att = jnp.where(mask, jnp.exp(z - sp + later), 0.0)
    return carry, _dot(att.astype(BF16), v)


def _attn_kernel(q_ref, k_ref, v_ref, o_ref, carry_ref, acc_ref, *, rows, span, tw):
    g = pl.program_id(2)
    hp = q_ref.shape[0]
    suffix = (lax.broadcasted_iota(I32, (tw, tw), 0)
              > lax.broadcasted_iota(I32, (tw, tw), 1)).astype(BF16)

    start = pl.multiple_of(jnp.maximum(g * rows - (span - rows), 0), rows)
    cmax = None
    for hh in range(hp):
        carry, acc = _attn_span(q_ref[hh], k_ref[hh, pl.ds(start, span), :],
                                v_ref[hh, pl.ds(start, span), :], g * rows - start,
                                jnp.zeros((rows, 1), F32), suffix)
        carry_ref[hh] = carry
        acc_ref[hh] = acc
        cmax = carry if cmax is None else jnp.maximum(cmax, carry)

    @pl.when(jnp.max(cmax) > EXP_UNDERFLOW)
    def _():
        for hh in range(hp):

            def cond(st):
                j, carry, _ = st
                return jnp.logical_and(j >= 0, jnp.max(carry) > EXP_UNDERFLOW)

            def body(st, hh=hh):
                j, carry, acc = st
                ks = pl.multiple_of(j * tw, tw)
                carry, out = _attn_span(q_ref[hh], k_ref[hh, pl.ds(ks, tw), :],
                                        v_ref[hh, pl.ds(ks, tw), :], rows + tw, carry, suffix)
                return j - 1, carry, acc + out

            init = (start // tw - 1, carry_ref[hh], acc_ref[hh])
            acc_ref[hh] = lax.while_loop(cond, body, init)[2]

    o_ref[...] = jnp.concatenate([acc_ref[hh] for hh in range(hp)], axis=-1)


def _attention(qh, kh, vh, *, batch, seq, rows=256, span=512, tw=256, heads_per_step=2):
    H, T, hd = qh.shape
    ng = seq // rows
    hp = heads_per_step
    return pl.pallas_call(
        functools.partial(_attn_kernel, rows=rows, span=span, tw=tw),
        grid=(H // hp, batch, ng),
        in_specs=[
            pl.BlockSpec((hp, rows, hd), lambda h, b, i: (h, b * ng + i, 0)),
            pl.BlockSpec((hp, seq, hd), lambda h, b, i: (h, b, 0)),
            pl.BlockSpec((hp, seq, hd), lambda h, b, i: (h, b, 0)),
        ],
        out_specs=pl.BlockSpec((rows, hp * hd), lambda h, b, i: (b * ng + i, h)),
        out_shape=jax.ShapeDtypeStruct((T, H * hd), F32),
        scratch_shapes=[pltpu.VMEM((hp, rows, 1), F32),
                        pltpu.VMEM((hp, rows, hd), F32)],
        compiler_params=pltpu.CompilerParams(
            dimension_semantics=("arbitrary", "arbitrary", "arbitrary"),
            vmem_limit_bytes=VMEM_LIMIT),
        name="sb_attention",
    )(qh, kh, vh)


def _mix_kernel(x_ref, o_ref, hg_ref, hgp_ref, cw_ref, cb_ref, lg_ref, lb_ref, ag_ref,
                wo_ref, g2_ref, wqh_ref, wql_ref, h_ref, xn_ref, qh_ref, ext_ref,
                *, blocks_per_seq):
    tm = x_ref.shape[0]
    first = (pl.program_id(0) % blocks_per_seq) == 0
    ext_ref[0:CONV_HALO, :] = jnp.where(first, 0.0, hgp_ref[...])
    ext_ref[CONV_HALO:, :] = hg_ref[...]
    conv = jnp.zeros(hg_ref.shape, F32)
    for j in range(CONV_KERNEL):
        off = CONV_HALO - (CONV_KERNEL - 1) + j
        conv = conv + cw_ref[j:j + 1, :] * ext_ref[pl.ds(off, tm), :]
    conv = conv + cb_ref[...]
    mu = jnp.mean(conv, axis=-1, keepdims=True)
    xc = conv - mu
    var = jnp.mean(xc * xc, axis=-1, keepdims=True)
    y = xc * lax.rsqrt(var + EPS) * lg_ref[...] + lb_ref[...]
    o_conv = y * _sigmoid(y)

    o = o_ref[...]
    o_attn = o * lax.rsqrt(jnp.mean(o * o, axis=-1, keepdims=True) + EPS) * ag_ref[...]
    mixed = jnp.concatenate([o_attn, o_conv], axis=-1).astype(BF16)
    h = x_ref[...] + _dot(mixed, wo_ref[...])
    h_ref[...] = h

    xn = h * lax.rsqrt(jnp.mean(h * h, axis=-1, keepdims=True) + EPS) * g2_ref[...]
    xn_ref[...] = xn
    hi, lo = _split_bf16(xn)
    wqh = wqh_ref[...]
    qh_ref[...] = _dot(hi, wqh) + _dot(lo, wqh) + _dot(hi, wql_ref[...])


def _mix(x2, o, hg, conv_w, conv_b, ln_g, ln_b, attn_g, wo_bf, g2, wq_hi, wq_lo, *, seq, tm):
    T, D = x2.shape
    cwid = hg.shape[1]
    E = wq_hi.shape[1]
    halo_per_block = tm // CONV_HALO
    row = lambda i: (i, 0)
    const = lambda i: (0, 0)
    return pl.pallas_call(
        functools.partial(_mix_kernel, blocks_per_seq=seq // tm),
        grid=(T // tm,),
        in_specs=[
            pl.BlockSpec((tm, D), row),
            pl.BlockSpec((tm, o.shape[1]), row),
            pl.BlockSpec((tm, cwid), row),
            pl.BlockSpec((CONV_HALO, cwid),
                         lambda i: (jnp.maximum(i * halo_per_block - 1, 0), 0)),
            pl.BlockSpec((CONV_KERNEL, cwid), const),
            pl.BlockSpec((1, cwid), const),
            pl.BlockSpec((1, cwid), const),
            pl.BlockSpec((1, cwid), const),
            pl.BlockSpec((1, o.shape[1]), const),
            pl.BlockSpec(wo_bf.shape, const),
            pl.BlockSpec((1, D), const),
            pl.BlockSpec(wq_hi.shape, const),
            pl.BlockSpec(wq_lo.shape, const),
        ],
        out_specs=[pl.BlockSpec((tm, D), row), pl.BlockSpec((tm, D), row),
                   pl.BlockSpec((tm, E), row)],
        out_shape=[jax.ShapeDtypeStruct((T, D), F32), jax.ShapeDtypeStruct((T, D), F32),
                   jax.ShapeDtypeStruct((T, E), F32)],
        scratch_shapes=[pltpu.VMEM((tm + CONV_HALO, cwid), F32)],
        compiler_params=pltpu.CompilerParams(
            dimension_semantics=("arbitrary",), vmem_limit_bytes=VMEM_LIMIT),
        name="mix_outproj",
    )(x2, o, hg, hg, conv_w, conv_b, ln_g, ln_b, attn_g, wo_bf, g2, wq_hi, wq_lo)


def _top16(s, pos=None, payload=None):
    if pos is None:
        pos = lax.broadcasted_iota(I32, s.shape, 0).astype(F32)
    vals, outs = [], []
    for _ in range(PEER_TOPK):
        m = jnp.max(s, axis=0, keepdims=True)
        idx = jnp.min(jnp.where(s == m, pos, 1e9), axis=0, keepdims=True)
        sel = pos == idx
        if payload is None:
            outs.append(idx)
        else:
            outs.append(jnp.sum(jnp.where(sel, payload, 0.0), axis=0, keepdims=True))
        s = jnp.where(sel, -jnp.inf, s)
        vals.append(m)
    return jnp.concatenate(vals, axis=0), jnp.concatenate(outs, axis=0)


def _dot3_nt(a, b):
    ah, al = _split_bf16(a)
    bh, bl = _split_bf16(b)
    return _dot_nt(ah, bh) + _dot_nt(al, bh) + _dot_nt(ah, bl)


def _pair_candidates(v1, i1, v2, i2):
    r8 = lax.broadcasted_iota(I32, (8, v1.shape[1]), 0).astype(F32)
    sc, ex, ps = [], [], []

    def add(s1, e1, s2, e2, p, keep=None):
        s = s1 + s2
        sc.append(s if keep is None else jnp.where(keep, s, -jnp.inf))
        ex.append(e1 * float(N_KEYS) + e2)
        ps.append(p if keep is None else jnp.where(keep, p, 2e9))

    for a, half in ((0, 0), (0, 1), (1, 0), (2, 0), (3, 0)):
        b = slice(8 * half, 8 * half + 8)
        add(v1[a:a + 1], i1[a:a + 1], v2[b], i2[b], r8 + float(a * PEER_TOPK + 8 * half))
    for b in range(3):
        add(v1[0:8], i1[0:8], v2[b:b + 1], i2[b:b + 1], r8 * float(PEER_TOPK) + float(b),
            keep=r8 >= 4.0)
    add(v1[8:16], i1[8:16], v2[0:1], i2[0:1], (r8 + 8.0) * float(PEER_TOPK))
    return (jnp.concatenate(sc, axis=0), jnp.concatenate(ex, axis=0),
            jnp.concatenate(ps, axis=0))


def _topk_kernel(qh_ref, k1_ref, k2_ref, idx_ref, gate_ref):
    all_experts, all_gates = [], []
    for h in range(PEER_HEADS):
        base = h * 2 * PEER_HALF
        q1 = qh_ref[:, base:base + PEER_HALF]
        q2 = qh_ref[:, base + PEER_HALF:base + 2 * PEER_HALF]
        s1 = _dot3_nt(k1_ref[h], q1)
        s2 = _dot3_nt(k2_ref[h], q2)
        v1, i1 = _top16(s1)
        v2, i2 = _top16(s2)
        cand, cexp, cpos = _pair_candidates(v1, i1, v2, i2)
        top_s, experts = _top16(cand, pos=cpos, payload=cexp)
        e = jnp.exp(top_s - top_s[0:1, :])
        all_experts.append(experts)
        all_gates.append(e / jnp.sum(e, axis=0, keepdims=True))
    idx_ref[...] = jnp.concatenate(all_experts, axis=0).T.astype(I32) * CHUNKS
    gate_ref[...] = jnp.concatenate(all_gates, axis=0).T


def _topk(qh, k1, k2, *, tm):
    T, E = qh.shape
    out_spec = pl.BlockSpec((tm, SEL), lambda i: (i, 0))
    return pl.pallas_call(
        _topk_kernel,
        grid=(T // tm,),
        in_specs=[
            pl.BlockSpec((tm, E), lambda i: (i, 0)),
            pl.BlockSpec(k1.shape, lambda i: (0, 0, 0)),
            pl.BlockSpec(k2.shape, lambda i: (0, 0, 0)),
        ],
        out_specs=[out_spec, out_spec],
        out_shape=[jax.ShapeDtypeStruct((T, SEL), I32), jax.ShapeDtypeStruct((T, SEL), F32)],
        compiler_params=pltpu.CompilerParams(
            dimension_semantics=("arbitrary",), vmem_limit_bytes=VMEM_LIMIT),
        name="peer_topk",
    )(qh, k1, k2)


CHUNKS = 4
CHUNK_STRIDE = SEL + 8


def _unpack(word):
    lo = lax.bitcast_convert_type(word << 16, F32)
    hi = lax.bitcast_convert_type(word & HI_MASK, F32)
    return lo, hi


def _gather_group(idx_ref, tab_ref, buf_ref, t0, slots, part=0, parts=1):
    ids = [idx_ref.at[0, 0, pl.ds((t0 + j) * SEL, SEL)] for j in range(len(slots))]
    for k in range(part * SEL // parts, (part + 1) * SEL // parts):
        for j, s in enumerate(slots):
            kk = (k + j) % SEL
            r = pl.multiple_of(ids[j][kk], CHUNKS)
            buf_ref[s, pl.ds(kk, CHUNKS, stride=CHUNK_STRIDE), :] = tab_ref[pl.ds(r, CHUNKS), :]


def _cols(c):
    return slice(c * 128, (c + 1) * 128)


def _chunk(buf_ref, s, c):
    return _unpack(buf_ref[s, c * CHUNK_STRIDE:c * CHUNK_STRIDE + SEL, :])


def _lane_replicated_sum(a):
    ones = jnp.ones((128, 128), BF16)
    hi, lo = _split_bf16(a)
    return _dot(hi, ones) + _dot(lo, ones)


def _diag_mask():
    return (lax.broadcasted_iota(I32, (SEL, 128), 0) == lax.broadcasted_iota(I32, (SEL, 128), 1))


GROUP = 4


def _grouped_tokens(tb, gather, reduce):
    half = (tuple(range(GROUP)), tuple(range(GROUP, 2 * GROUP)))
    gather(0, half[0], 0, 1)

    def trip(i, carry):
        t = 2 * GROUP * i
        for j in range(GROUP):
            gather(t + GROUP, half[1], j, GROUP)
            reduce(t + j, half[0][j])
        nxt = jnp.minimum(t + 2 * GROUP, tb - GROUP)
        for j in range(GROUP):
            gather(nxt, half[0], j, GROUP)
            reduce(t + GROUP + j, half[1][j])
        return carry

    lax.fori_loop(0, tb // (2 * GROUP), trip, 0)


def _peer_u_kernel(idx_ref, x_ref, gate_ref, tab_ref, w_ref, buf_ref, act_ref):
    tb = x_ref.shape[0]
    diag = _diag_mask()

    def gather(t0, slots, part, parts):
        _gather_group(idx_ref, tab_ref, buf_ref, t0, slots, part, parts)

    def reduce(t, slot):
        total = None
        for c in range(CHUNKS):
            lo, hi = _chunk(buf_ref, slot, c)
            term = lo * x_ref[t, c:c + 1, :] + hi * x_ref[t, c + CHUNKS:c + CHUNKS + 1, :]
            total = term if total is None else total + term
        act_rep = _lane_replicated_sum(total)
        act_ref[pl.ds(t, 1), :] = jnp.sum(jnp.where(diag, act_rep, 0.0), axis=0, keepdims=True)

    _grouped_tokens(tb, gather, reduce)
    act = act_ref[...]
    gelu = 0.5 * act * (1.0 + lax.erf(act * (2.0 ** -0.5)))
    w_ref[...] = gate_ref[...] * gelu


def _gather_scratch():
    return pltpu.VMEM((2 * GROUP, CHUNKS * CHUNK_STRIDE, 128), I32)


def _index_spec(tb):
    return pl.BlockSpec((1, 1, tb * SEL), lambda i: (i, 0, 0), memory_space=pltpu.SMEM)


def _peer_u(idx3, x3, gates, tab, *, tb):
    T = x3.shape[0]
    return pl.pallas_call(
        _peer_u_kernel,
        grid=(T // tb,),
        in_specs=[
            _index_spec(tb),
            pl.BlockSpec((tb, 2 * CHUNKS, 128), lambda i: (i, 0, 0)),
            pl.BlockSpec((tb, SEL), lambda i: (i, 0)),
            pl.BlockSpec(memory_space=pltpu.VMEM),
        ],
        out_specs=pl.BlockSpec((tb, SEL), lambda i: (i, 0)),
        out_shape=jax.ShapeDtypeStruct((T, SEL), F32),
        scratch_shapes=[_gather_scratch(), pltpu.VMEM((tb, SEL), F32)],
        compiler_params=pltpu.CompilerParams(
            dimension_semantics=("arbitrary",), vmem_limit_bytes=VMEM_LIMIT),
        name="peer_u",
    )(idx3, x3, gates, tab)


def _peer_v_kernel(idx_ref, w_ref, h_ref, tab_ref, y_ref, buf_ref):
    tb = h_ref.shape[0]
    diag = _diag_mask()

    def gather(t0, slots, part, parts):
        _gather_group(idx_ref, tab_ref, buf_ref, t0, slots, part, parts)

    def reduce(t, slot):
        w_rep = _lane_replicated_sum(jnp.where(diag, w_ref[pl.ds(t, 1), :], 0.0))
        lows, highs = [], []
        for c in range(CHUNKS):
            lo, hi = _chunk(buf_ref, slot, c)
            lows.append(jnp.sum(lo * w_rep, axis=0, keepdims=True))
            highs.append(jnp.sum(hi * w_rep, axis=0, keepdims=True))
        y_ref[pl.ds(t, 1), :] = h_ref[pl.ds(t, 1), :] + jnp.concatenate(lows + highs, axis=-1)

    _grouped_tokens(tb, gather, reduce)


def _peer_v(idx3, w, h2, tab, *, tb):
    T, D = h2.shape
    return pl.pallas_call(
        _peer_v_kernel,
        grid=(T // tb,),
        in_specs=[
            _index_spec(tb),
            pl.BlockSpec((tb, SEL), lambda i: (i, 0)),
            pl.BlockSpec((tb, D), lambda i: (i, 0)),
            pl.BlockSpec(memory_space=pltpu.VMEM),
        ],
        out_specs=pl.BlockSpec((tb, D), lambda i: (i, 0)),
        out_shape=jax.ShapeDtypeStruct(h2.shape, F32),
        scratch_shapes=[_gather_scratch()],
        compiler_params=pltpu.CompilerParams(
            dimension_semantics=("arbitrary",), vmem_limit_bytes=VMEM_LIMIT),
        name="peer_v",
    )(idx3, w, h2, tab)


def _pack_table(tab):
    n, d = tab.shape
    bits = lax.bitcast_convert_type(tab.astype(BF16), jnp.uint16).astype(jnp.uint32)
    word = bits[:, :d // 2] | (bits[:, d // 2:] << 16)
    return lax.bitcast_convert_type(word, I32).reshape(n * 4, 128)


def _layer(x2, p, *, batch, seq, tm_in, tm_mix, tm_topk, tb):
    T, D = x2.shape
    row = lambda a: a.reshape(1, -1)
    w_in_bf = p["w_in"].astype(BF16)
    gq_t = row(jnp.tile(p["q_norm_g"], ATTN_HEADS))
    gk_t = row(jnp.tile(p["k_norm_g"], ATTN_HEADS))
    qh, kh, vh, hg = _inproj(x2, row(p["norm_mix_g"]), w_in_bf, gq_t, gk_t, tm=tm_in)
    o = _attention(qh, kh, vh, batch=batch, seq=seq)

    wq = p["peer_wq"]
    wq_hi = wq.astype(BF16)
    wq_lo = (wq - wq_hi.astype(F32)).astype(BF16)
    h, xn, pq = _mix(x2, o, hg, p["conv_w"], row(p["conv_b"]), row(p["conv_ln_g"]),
                     row(p["conv_ln_b"]), row(p["attn_out_g"]), p["w_out"].astype(BF16),
                     row(p["norm_ffn_g"]), wq_hi, wq_lo, seq=seq, tm=tm_mix)

    idx, gates = _topk(pq, p["peer_k1"], p["peer_k2"], tm=tm_topk)
    idx3 = idx.reshape(T // tb, 1, tb * SEL)
    w = _peer_u(idx3, xn.reshape(T, 2 * CHUNKS, 128), gates, _pack_table(p["peer_u"]), tb=tb)
    return _peer_v(idx3, w, h, _pack_table(p["peer_v"]), tb=tb)


def kernel(x, norm_mix_g, w_in, q_norm_g, k_norm_g, attn_out_g, conv_w, conv_b, conv_ln_g,
           conv_ln_b, w_out, norm_ffn_g, peer_wq, peer_k1, peer_k2, peer_u, peer_v):
    batch, seq, d = x.shape
    stacked = dict(norm_mix_g=norm_mix_g, w_in=w_in, q_norm_g=q_norm_g, k_norm_g=k_norm_g,
                   attn_out_g=attn_out_g, conv_w=conv_w, conv_b=conv_b, conv_ln_g=conv_ln_g,
                   conv_ln_b=conv_ln_b, w_out=w_out, norm_ffn_g=norm_ffn_g, peer_wq=peer_wq,
                   peer_k1=peer_k1, peer_k2=peer_k2, peer_u=peer_u, peer_v=peer_v)
    h = x.reshape(batch * seq, d)
    for l in range(w_in.shape[0]):
        p = {name: a[l] for name, a in stacked.items()}
        h = _layer(h, p, batch=batch, seq=seq, tm_in=256, tm_mix=256,
                   tm_topk=256, tb=64)
    return h.reshape(batch, seq, d)
```

```python
import functools

import jax
import jax.numpy as jnp
from jax import lax
from jax.experimental import pallas as pl
from jax.experimental.pallas import tpu as pltpu
from jax.experimental.pallas import tpu_sc as plsc

F32 = jnp.float32
BF16 = jnp.bfloat16
I32 = jnp.int32

EPS = 1e-6
HEAD_DIM = 64
ATTN_HEADS = 8
ATTN_WIDTH = ATTN_HEADS * HEAD_DIM
CONV_KERNEL = 31
CONV_HALO = 32
PEER_HEADS = 8
PEER_HALF = 64
N_KEYS = 128
PEER_TOPK = 16
SEL = PEER_HEADS * PEER_TOPK
EXP_UNDERFLOW = -88.0
HI_MASK = -65536

VMEM_LIMIT = 56 * 1024 * 1024


def _split_bf16(a):
    hi = a.astype(BF16)
    lo = (a - hi.astype(F32)).astype(BF16)
    return hi, lo


def _dot(a, b):
    return jnp.dot(a, b, preferred_element_type=F32)


def _dot_nt(a, b):
    return lax.dot_general(a, b, (((1,), (1,)), ((), ())), preferred_element_type=F32)


def _sigmoid(x):
    return 1.0 / (1.0 + jnp.exp(-x))


def _inproj_kernel(x_ref, g_ref, w_ref, gq_ref, gk_ref, q_ref, k_ref, v_ref, hg_ref):
    x = x_ref[...]
    ms = jnp.mean(x * x, axis=-1, keepdims=True)
    xn = (x * lax.rsqrt(ms + EPS) * g_ref[...]).astype(BF16)
    proj = _dot(xn, w_ref[...])

    r = lax.broadcasted_iota(I32, (ATTN_WIDTH, ATTN_WIDTH), 0) // HEAD_DIM
    c = lax.broadcasted_iota(I32, (ATTN_WIDTH, ATTN_WIDTH), 1) // HEAD_DIM
    same_head = (r == c).astype(BF16)

    def head_norm(t, g):
        hi, lo = _split_bf16(t * t)
        msq = (_dot(hi, same_head) + _dot(lo, same_head)) * (1.0 / HEAD_DIM)
        return t * lax.rsqrt(msq + EPS) * g

    q = head_norm(proj[:, :ATTN_WIDTH], gq_ref[...]) * (HEAD_DIM ** -0.5)
    k = head_norm(proj[:, ATTN_WIDTH:2 * ATTN_WIDTH], gk_ref[...])
    v = proj[:, 2 * ATTN_WIDTH:3 * ATTN_WIDTH]
    for h in range(ATTN_HEADS):
        sl = slice(h * HEAD_DIM, (h + 1) * HEAD_DIM)
        q_ref[h] = q[:, sl].astype(BF16)
        k_ref[h] = k[:, sl].astype(BF16)
        v_ref[h] = v[:, sl].astype(BF16)
    cw = (proj.shape[1] - 3 * ATTN_WIDTH) // 2
    a = proj[:, 3 * ATTN_WIDTH:3 * ATTN_WIDTH + cw]
    gate = proj[:, 3 * ATTN_WIDTH + cw:]
    hg_ref[...] = a * _sigmoid(gate)


def _inproj(x2, g, w_bf, gq_t, gk_t, *, tm):
    T, D = x2.shape
    E = w_bf.shape[1]
    cw = (E - 3 * ATTN_WIDTH) // 2
    head_shape = jax.ShapeDtypeStruct((ATTN_HEADS, T, HEAD_DIM), BF16)
    head_spec = pl.BlockSpec((ATTN_HEADS, tm, HEAD_DIM), lambda i: (0, i, 0))
    return pl.pallas_call(
        _inproj_kernel,
        grid=(T // tm,),
        in_specs=[
            pl.BlockSpec((tm, D), lambda i: (i, 0)),
            pl.BlockSpec((1, D), lambda i: (0, 0)),
            pl.BlockSpec((D, E), lambda i: (0, 0)),
            pl.BlockSpec((1, ATTN_WIDTH), lambda i: (0, 0)),
            pl.BlockSpec((1, ATTN_WIDTH), lambda i: (0, 0)),
        ],
        out_specs=[head_spec, head_spec, head_spec,
                   pl.BlockSpec((tm, cw), lambda i: (i, 0))],
        out_shape=[head_shape, head_shape, head_shape,
                   jax.ShapeDtypeStruct((T, cw), F32)],
        compiler_params=pltpu.CompilerParams(
            dimension_semantics=("arbitrary",), vmem_limit_bytes=VMEM_LIMIT),
        name="inproj",
    )(x2, g, w_bf, gq_t, gk_t)


def _attn_span(q, k, v, offset, carry, suffix):
    tw = suffix.shape[0]
    z = _dot_nt(q, k)
    sp = jnp.maximum(z, 0.0) + jnp.log(1.0 + jnp.exp(-jnp.abs(z)))
    col_minus_row = (lax.broadcasted_iota(I32, z.shape, 1)
                     - lax.broadcasted_iota(I32, z.shape, 0))
    mask = col_minus_row < offset
    log_keep = jnp.where(mask, -sp, 0.0)
    later = []
    for s in reversed(range(z.shape[1] // tw)):
        lk = log_keep[:, s * tw:(s + 1) * tw]
        hi, lo = _split_bf16(lk)
        later.append(carry + (_dot(hi, suffix) + _dot(lo, suffix)))
        carry = carry + jnp.sum(lk, axis=-1, keepdims=True)
    later = jnp.concatenate(later[::-1], axis=-1)
    att = jnp.where(mask, jnp.exp(z - sp + later), 0.0)
    return carry, _dot(att.astype(BF16), v)


def _attn_kernel(q_ref, k_ref, v_ref, o_ref, carry_ref, acc_ref, *, rows, span, tw):
    g = pl.program_id(2)
    hp = q_ref.shape[0]
    suffix = (lax.broadcasted_iota(I32, (tw, tw), 0)
              > lax.broadcasted_iota(I32, (tw, tw), 1)).astype(BF16)

    start = pl.multiple_of(jnp.maximum(g * rows - (span - rows), 0), rows)
    cmax = None
    for hh in range(hp):
        carry, acc = _attn_span(q_ref[hh], k_ref[hh, pl.ds(start, span), :],
                                v_ref[hh, pl.ds(start, span), :], g * rows - start,
                                jnp.zeros((rows, 1), F32), suffix)
        carry_ref[hh] = carry
        acc_ref[hh] = acc
        cmax = carry if cmax is None else jnp.maximum(cmax, carry)

    @pl.when(jnp.max(cmax) > EXP_UNDERFLOW)
    def _():
        for hh in range(hp):

            def cond(st):
                j, carry, _ = st
                return jnp.logical_and(j >= 0, jnp.max(carry) > EXP_UNDERFLOW)

            def body(st, hh=hh):
                j, carry, acc = st
                ks = pl.multiple_of(j * tw, tw)
                carry, out = _attn_span(q_ref[hh], k_ref[hh, pl.ds(ks, tw), :],
                                        v_ref[hh, pl.ds(ks, tw), :], rows + tw, carry, suffix)
                return j - 1, carry, acc + out

            init = (start // tw - 1, carry_ref[hh], acc_ref[hh])
            acc_ref[hh] = lax.while_loop(cond, body, init)[2]

    o_ref[...] = jnp.concatenate([acc_ref[hh] for hh in range(hp)], axis=-1)


def _attention(qh, kh, vh, *, batch, seq, rows=256, span=512, tw=256, heads_per_step=2):
    H, T, hd = qh.shape
    ng = seq // rows
    hp = heads_per_step
    return pl.pallas_call(
        functools.partial(_attn_kernel, rows=rows, span=span, tw=tw),
        grid=(H // hp, batch, ng),
        in_specs=[
            pl.BlockSpec((hp, rows, hd), lambda h, b, i: (h, b * ng + i, 0)),
            pl.BlockSpec((hp, seq, hd), lambda h, b, i: (h, b, 0)),
            pl.BlockSpec((hp, seq, hd), lambda h, b, i: (h, b, 0)),
        ],
        out_specs=pl.BlockSpec((rows, hp * hd), lambda h, b, i: (b * ng + i, h)),
        out_shape=jax.ShapeDtypeStruct((T, H * hd), F32),
        scratch_shapes=[pltpu.VMEM((hp, rows, 1), F32),
                        pltpu.VMEM((hp, rows, hd), F32)],
        compiler_params=pltpu.CompilerParams(
            dimension_semantics=("arbitrary", "arbitrary", "arbitrary"),
            vmem_limit_bytes=VMEM_LIMIT),
        name="sb_attention",
    )(qh, kh, vh)


def _mix_kernel(x_ref, o_ref, hg_ref, hgp_ref, cw_ref, cb_ref, lg_ref, lb_ref, ag_ref,
                wo_ref, g2_ref, wqh_ref, wql_ref, h_ref, xn_ref, qh_ref, ext_ref,
                *, blocks_per_seq):
    tm = x_ref.shape[0]
    first = (pl.program_id(0) % blocks_per_seq) == 0
    ext_ref[0:CONV_HALO, :] = jnp.where(first, 0.0, hgp_ref[...])
    ext_ref[CONV_HALO:, :] = hg_ref[...]
    conv = jnp.zeros(hg_ref.shape, F32)
    for j in range(CONV_KERNEL):
        off = CONV_HALO - (CONV_KERNEL - 1) + j
        conv = conv + cw_ref[j:j + 1, :] * ext_ref[pl.ds(off, tm), :]
    conv = conv + cb_ref[...]
    mu = jnp.mean(conv, axis=-1, keepdims=True)
    xc = conv - mu
    var = jnp.mean(xc * xc, axis=-1, keepdims=True)
    y = xc * lax.rsqrt(var + EPS) * lg_ref[...] + lb_ref[...]
    o_conv = y * _sigmoid(y)

    o = o_ref[...]
    o_attn = o * lax.rsqrt(jnp.mean(o * o, axis=-1, keepdims=True) + EPS) * ag_ref[...]
    mixed = jnp.concatenate([o_attn, o_conv], axis=-1).astype(BF16)
    h = x_ref[...] + _dot(mixed, wo_ref[...])
    h_ref[...] = h

    xn = h * lax.rsqrt(jnp.mean(h * h, axis=-1, keepdims=True) + EPS) * g2_ref[...]
    xn_ref[...] = xn
    hi, lo = _split_bf16(xn)
    wqh = wqh_ref[...]
    qh_ref[...] = _dot(hi, wqh) + _dot(lo, wqh) + _dot(hi, wql_ref[...])


def _mix(x2, o, hg, conv_w, conv_b, ln_g, ln_b, attn_g, wo_bf, g2, wq_hi, wq_lo, *, seq, tm):
    T, D = x2.shape
    cwid = hg.shape[1]
    E = wq_hi.shape[1]
    halo_per_block = tm // CONV_HALO
    row = lambda i: (i, 0)
    const = lambda i: (0, 0)
    return pl.pallas_call(
        functools.partial(_mix_kernel, blocks_per_seq=seq // tm),
        grid=(T // tm,),
        in_specs=[
            pl.BlockSpec((tm, D), row),
            pl.BlockSpec((tm, o.shape[1]), row),
            pl.BlockSpec((tm, cwid), row),
            pl.BlockSpec((CONV_HALO, cwid),
                         lambda i: (jnp.maximum(i * halo_per_block - 1, 0), 0)),
            pl.BlockSpec((CONV_KERNEL, cwid), const),
            pl.BlockSpec((1, cwid), const),
            pl.BlockSpec((1, cwid), const),
            pl.BlockSpec((1, cwid), const),
            pl.BlockSpec((1, o.shape[1]), const),
            pl.BlockSpec(wo_bf.shape, const),
            pl.BlockSpec((1, D), const),
            pl.BlockSpec(wq_hi.shape, const),
            pl.BlockSpec(wq_lo.shape, const),
        ],
        out_specs=[pl.BlockSpec((tm, D), row), pl.BlockSpec((tm, D), row),
                   pl.BlockSpec((tm, E), row)],
        out_shape=[jax.ShapeDtypeStruct((T, D), F32), jax.ShapeDtypeStruct((T, D), F32),
                   jax.ShapeDtypeStruct((T, E), F32)],
        scratch_shapes=[pltpu.VMEM((tm + CONV_HALO, cwid), F32)],
        compiler_params=pltpu.CompilerParams(
            dimension_semantics=("arbitrary",), vmem_limit_bytes=VMEM_LIMIT),
        name="mix_outproj",
    )(x2, o, hg, hg, conv_w, conv_b, ln_g, ln_b, attn_g, wo_bf, g2, wq_hi, wq_lo)


def _top16(s, pos=None, payload=None):
    if pos is None:
        pos = lax.broadcasted_iota(I32, s.shape, 0).astype(F32)
    vals, outs = [], []
    for _ in range(PEER_TOPK):
        m = jnp.max(s, axis=0, keepdims=True)
        idx = jnp.min(jnp.where(s == m, pos, 1e9), axis=0, keepdims=True)
        sel = pos == idx
        if payload is None:
            outs.append(idx)
        else:
            outs.append(jnp.sum(jnp.where(sel, payload, 0.0), axis=0, keepdims=True))
        s = jnp.where(sel, -jnp.inf, s)
        vals.append(m)
    return jnp.concatenate(vals, axis=0), jnp.concatenate(outs, axis=0)


def _dot3_nt(a, b):
    ah, al = _split_bf16(a)
    bh, bl = _split_bf16(b)
    return _dot_nt(ah, bh) + _dot_nt(al, bh) + _dot_nt(ah, bl)


def _pair_candidates(v1, i1, v2, i2):
    r8 = lax.broadcasted_iota(I32, (8, v1.shape[1]), 0).astype(F32)
    sc, ex, ps = [], [], []

    def add(s1, e1, s2, e2, p, keep=None):
        s = s1 + s2
        sc.append(s if keep is None else jnp.where(keep, s, -jnp.inf))
        ex.append(e1 * float(N_KEYS) + e2)
        ps.append(p if keep is None else jnp.where(keep, p, 2e9))

    for a, half in ((0, 0), (0, 1), (1, 0), (2, 0), (3, 0)):
        b = slice(8 * half, 8 * half + 8)
        add(v1[a:a + 1], i1[a:a + 1], v2[b], i2[b], r8 + float(a * PEER_TOPK + 8 * half))
    for b in range(3):
        add(v1[0:8], i1[0:8], v2[b:b + 1], i2[b:b + 1], r8 * float(PEER_TOPK) + float(b),
            keep=r8 >= 4.0)
    add(v1[8:16], i1[8:16], v2[0:1], i2[0:1], (r8 + 8.0) * float(PEER_TOPK))
    return (jnp.concatenate(sc, axis=0), jnp.concatenate(ex, axis=0),
            jnp.concatenate(ps, axis=0))


def _topk_kernel(qh_ref, k1_ref, k2_ref, idx_ref, gate_ref):
    all_experts, all_gates = [], []
    for h in range(PEER_HEADS):
        base = h * 2 * PEER_HALF
        q1 = qh_ref[:, base:base + PEER_HALF]
        q2 = qh_ref[:, base + PEER_HALF:base + 2 * PEER_HALF]
        s1 = _dot3_nt(k1_ref[h], q1)
        s2 = _dot3_nt(k2_ref[h], q2)
        v1, i1 = _top16(s1)
        v2, i2 = _top16(s2)
        cand, cexp, cpos = _pair_candidates(v1, i1, v2, i2)
        top_s, experts = _top16(cand, pos=cpos, payload=cexp)
        e = jnp.exp(top_s - top_s[0:1, :])
        all_experts.append(experts)
        all_gates.append(e / jnp.sum(e, axis=0, keepdims=True))
    idx_ref[...] = jnp.concatenate(all_experts, axis=0).T.astype(I32) * CHUNKS
    gate_ref[...] = jnp.concatenate(all_gates, axis=0).T


def _topk(qh, k1, k2, *, tm):
    T, E = qh.shape
    out_spec = pl.BlockSpec((tm, SEL), lambda i: (i, 0))
    return pl.pallas_call(
        _topk_kernel,
        grid=(T // tm,),
        in_specs=[
            pl.BlockSpec((tm, E), lambda i: (i, 0)),
            pl.BlockSpec(k1.shape, lambda i: (0, 0, 0)),
            pl.BlockSpec(k2.shape, lambda i: (0, 0, 0)),
        ],
        out_specs=[out_spec, out_spec],
        out_shape=[jax.ShapeDtypeStruct((T, SEL), I32), jax.ShapeDtypeStruct((T, SEL), F32)],
        compiler_params=pltpu.CompilerParams(
            dimension_semantics=("arbitrary",), vmem_limit_bytes=VMEM_LIMIT),
        name="peer_topk",
    )(qh, k1, k2)


CHUNKS = 4
CHUNK_STRIDE = SEL + 8


def _unpack(word):
    lo = lax.bitcast_convert_type(word << 16, F32)
    hi = lax.bitcast_convert_type(word & HI_MASK, F32)
    return lo, hi


def _gather_group(idx_ref, tab_ref, buf_ref, t0, slots, part=0, parts=1):
    ids = [idx_ref.at[0, 0, pl.ds((t0 + j) * SEL, SEL)] for j in range(len(slots))]
    for k in range(part * SEL // parts, (part + 1) * SEL // parts):
        for j, s in enumerate(slots):
            r = pl.multiple_of(ids[j][k], CHUNKS)
            buf_ref[s, pl.ds(k, CHUNKS, stride=CHUNK_STRIDE), :] = tab_ref[pl.ds(r, CHUNKS), :]


def _cols(c):
    return slice(c * 128, (c + 1) * 128)


def _chunk(buf_ref, s, c):
    return _unpack(buf_ref[s, c * CHUNK_STRIDE:c * CHUNK_STRIDE + SEL, :])


def _lane_replicated_sum(a):
    ones = jnp.ones((128, 128), BF16)
    hi, lo = _split_bf16(a)
    return _dot(hi, ones) + _dot(lo, ones)


def _diag_mask():
    return (lax.broadcasted_iota(I32, (SEL, 128), 0) == lax.broadcasted_iota(I32, (SEL, 128), 1))


GROUP = 4


def _grouped_tokens(tb, gather, reduce):
    half = (tuple(range(GROUP)), tuple(range(GROUP, 2 * GROUP)))
    gather(0, half[0], 0, 1)

    def trip(i, carry):
        t = 2 * GROUP * i
        for j in range(GROUP):
            gather(t + GROUP, half[1], j, GROUP)
            reduce(t + j, half[0][j])
        nxt = jnp.minimum(t + 2 * GROUP, tb - GROUP)
        for j in range(GROUP):
            gather(nxt, half[0], j, GROUP)
            reduce(t + GROUP + j, half[1][j])
        return carry

    lax.fori_loop(0, tb // (2 * GROUP), trip, 0)


def _peer_u_kernel(idx_ref, x_ref, gate_ref, tab_ref, w_ref, buf_ref, act_ref):
    tb = x_ref.shape[0]
    diag = _diag_mask()

    def gather(t0, slots, part, parts):
        _gather_group(idx_ref, tab_ref, buf_ref, t0, slots, part, parts)

    def reduce(t, slot):
        total = None
        for c in range(CHUNKS):
            lo, hi = _chunk(buf_ref, slot, c)
            term = lo * x_ref[t, c:c + 1, :] + hi * x_ref[t, c + CHUNKS:c + CHUNKS + 1, :]
            total = term if total is None else total + term
        act_rep = _lane_replicated_sum(total)
        act_ref[pl.ds(t, 1), :] = jnp.sum(jnp.where(diag, act_rep, 0.0), axis=0, keepdims=True)

    _grouped_tokens(tb, gather, reduce)
    act = act_ref[...]
    gelu = 0.5 * act * (1.0 + lax.erf(act * (2.0 ** -0.5)))
    w_ref[...] = gate_ref[...] * gelu


def _gather_scratch():
    return pltpu.VMEM((2 * GROUP, CHUNKS * CHUNK_STRIDE, 128), I32)


def _index_spec(tb):
    return pl.BlockSpec((1, 1, tb * SEL), lambda i: (i, 0, 0), memory_space=pltpu.SMEM)


def _peer_u(idx3, x3, gates, tab, *, tb):
    T = x3.shape[0]
    return pl.pallas_call(
        _peer_u_kernel,
        grid=(T // tb,),
        in_specs=[
            _index_spec(tb),
            pl.BlockSpec((tb, 2 * CHUNKS, 128), lambda i: (i, 0, 0)),
            pl.BlockSpec((tb, SEL), lambda i: (i, 0)),
            pl.BlockSpec(memory_space=pltpu.VMEM),
        ],
        out_specs=pl.BlockSpec((tb, SEL), lambda i: (i, 0)),
        out_shape=jax.ShapeDtypeStruct((T, SEL), F32),
        scratch_shapes=[_gather_scratch(), pltpu.VMEM((tb, SEL), F32)],
        compiler_params=pltpu.CompilerParams(
            dimension_semantics=("arbitrary",), vmem_limit_bytes=VMEM_LIMIT),
        name="peer_u",
    )(idx3, x3, gates, tab)


def _peer_v_kernel(idx_ref, w_ref, h_ref, tab_ref, y_ref, buf_ref):
    tb = h_ref.shape[0]
    diag = _diag_mask()

    def gather(t0, slots, part, parts):
        _gather_group(idx_ref, tab_ref, buf_ref, t0, slots, part, parts)

    def reduce(t, slot):
        w_rep = _lane_replicated_sum(jnp.where(diag, w_ref[pl.ds(t, 1), :], 0.0))
        lows, highs = [], []
        for c in range(CHUNKS):
            lo, hi = _chunk(buf_ref, slot, c)
            lows.append(jnp.sum(lo * w_rep, axis=0, keepdims=True))
            highs.append(jnp.sum(hi * w_rep, axis=0, keepdims=True))
        y_ref[pl.ds(t, 1), :] = h_ref[pl.ds(t, 1), :] + jnp.concatenate(lows + highs, axis=-1)

    _grouped_tokens(tb, gather, reduce)


def _peer_v(idx3, w, h2, tab, *, tb, tokens):
    D = h2.shape[1]
    return pl.pallas_call(
        _peer_v_kernel,
        grid=(tokens // tb,),
        in_specs=[
            _index_spec(tb),
            pl.BlockSpec((tb, SEL), lambda i: (i, 0)),
            pl.BlockSpec((tb, D), lambda i: (i, 0)),
            pl.BlockSpec(memory_space=pltpu.VMEM),
        ],
        out_specs=pl.BlockSpec((tb, D), lambda i: (i, 0)),
        out_shape=jax.ShapeDtypeStruct((tokens, D), F32),
        scratch_shapes=[_gather_scratch()],
        compiler_params=pltpu.CompilerParams(
            dimension_semantics=("arbitrary",), vmem_limit_bytes=VMEM_LIMIT),
        name="peer_v",
    )(idx3, w, h2, tab)


SC_CORES = 2
SC_WORKERS = 32
SC_LANES = 16
ROW_WORDS = 512


def _peer_v_sc_kernel(idx_hbm, w_hbm, h_hbm, tab_hbm, y_hbm, idx_v, w_v, rows_v, acc_v, sem,
                      *, first, per):
    wid = lax.axis_index("s") * SC_CORES + lax.axis_index("c")

    @pl.loop(0, per)
    def _(i):
        local = wid * per + i
        t = first + local
        pltpu.sync_copy(idx_hbm.at[t], idx_v)
        pltpu.sync_copy(w_hbm.at[t], w_v)
        pltpu.sync_copy(h_hbm.at[t], acc_v)
        pltpu.async_copy(tab_hbm.at[idx_v], rows_v, sem).wait()

        @pl.loop(0, SEL)
        def _(k):
            wk = plsc.load_gather(w_v, [jnp.full((SC_LANES,), k, I32)])
            for j in range(ROW_WORDS // SC_LANES):
                lo, hi = _unpack(rows_v[k, pl.ds(j * SC_LANES, SC_LANES)])
                plsc.addupdate(acc_v.at[pl.ds(j * SC_LANES, SC_LANES)], wk * lo)
                plsc.addupdate(acc_v.at[pl.ds(ROW_WORDS + j * SC_LANES, SC_LANES)], wk * hi)

        pltpu.sync_copy(acc_v, y_hbm.at[local])


def _peer_v_sc(ids, w, h2, tab_rows, *, first):
    T, D = h2.shape
    n = T - first
    mesh = plsc.VectorSubcoreMesh(core_axis_name="c", subcore_axis_name="s")
    body = functools.partial(_peer_v_sc_kernel, first=first, per=n // SC_WORKERS)
    return pl.kernel(
        body,
        out_type=jax.ShapeDtypeStruct((n, D), F32),
        mesh=mesh,
        scratch_types=[pltpu.VMEM((SEL,), I32), pltpu.VMEM((SEL,), F32),
                       pltpu.VMEM((SEL, ROW_WORDS), I32), pltpu.VMEM((D,), F32),
                       pltpu.SemaphoreType.DMA],
        compiler_params=pltpu.CompilerParams(needs_layout_passes=False),
        name="peer_v_sc",
    )(ids, w, h2, tab_rows)


def _pack_table(tab):
    n, d = tab.shape
    bits = lax.bitcast_convert_type(tab.astype(BF16), jnp.uint16).astype(jnp.uint32)
    word = bits[:, :d // 2] | (bits[:, d // 2:] << 16)
    return lax.bitcast_convert_type(word, I32).reshape(n * 4, 128)


def _layer(x2, p, *, batch, seq, tm_in, tm_mix, tm_topk, tb, sc_tokens):
    T, D = x2.shape
    row = lambda a: a.reshape(1, -1)
    w_in_bf = p["w_in"].astype(BF16)
    gq_t = row(jnp.tile(p["q_norm_g"], ATTN_HEADS))
    gk_t = row(jnp.tile(p["k_norm_g"], ATTN_HEADS))
    qh, kh, vh, hg = _inproj(x2, row(p["norm_mix_g"]), w_in_bf, gq_t, gk_t, tm=tm_in)
    o = _attention(qh, kh, vh, batch=batch, seq=seq)

    wq = p["peer_wq"]
    wq_hi = wq.astype(BF16)
    wq_lo = (wq - wq_hi.astype(F32)).astype(BF16)
    h, xn, pq = _mix(x2, o, hg, p["conv_w"], row(p["conv_b"]), row(p["conv_ln_g"]),
                     row(p["conv_ln_b"]), row(p["attn_out_g"]), p["w_out"].astype(BF16),
                     row(p["norm_ffn_g"]), wq_hi, wq_lo, seq=seq, tm=tm_mix)

    idx, gates = _topk(pq, p["peer_k1"], p["peer_k2"], tm=tm_topk)
    idx3 = idx.reshape(T // tb, 1, tb * SEL)
    w = _peer_u(idx3, xn.reshape(T, 2 * CHUNKS, 128), gates, _pack_table(p["peer_u"]), tb=tb)
    v_tab = _pack_table(p["peer_v"])
    head = T - sc_tokens
    y_head = _peer_v(idx3, w, h, v_tab, tb=tb, tokens=head)
    y_tail = _peer_v_sc(idx // CHUNKS, w, h, v_tab.reshape(-1, ROW_WORDS), first=head)
    return jnp.concatenate([y_head, y_tail], axis=0)


def kernel(x, norm_mix_g, w_in, q_norm_g, k_norm_g, attn_out_g, conv_w, conv_b, conv_ln_g,
           conv_ln_b, w_out, norm_ffn_g, peer_wq, peer_k1, peer_k2, peer_u, peer_v):
    batch, seq, d = x.shape
    stacked = dict(norm_mix_g=norm_mix_g, w_in=w_in, q_norm_g=q_norm_g, k_norm_g=k_norm_g,
                   attn_out_g=attn_out_g, conv_w=conv_w, conv_b=conv_b, conv_ln_g=conv_ln_g,
                   conv_ln_b=conv_ln_b, w_out=w_out, norm_ffn_g=norm_ffn_g, peer_wq=peer_wq,
                   peer_k1=peer_k1, peer_k2=peer_k2, peer_u=peer_u, peer_v=peer_v)
    h = x.reshape(batch * seq, d)
    for l in range(w_in.shape[0]):
        p = {name: a[l] for name, a in stacked.items()}
        h = _layer(h, p, batch=batch, seq=seq, tm_in=256, tm_mix=256,
                   tm_topk=256, tb=64, sc_tokens=8192)
    return h.reshape(batch, seq, d)
```

```python
import functools

import jax
import jax.numpy as jnp
from jax import lax
from jax.experimental import pallas as pl
from jax.experimental.pallas import tpu as pltpu
from jax.experimental.pallas import tpu_sc as plsc

F32 = jnp.float32
BF16 = jnp.bfloat16
I32 = jnp.int32

EPS = 1e-6
HEAD_DIM = 64
ATTN_HEADS = 8
ATTN_WIDTH = ATTN_HEADS * HEAD_DIM
CONV_KERNEL = 31
CONV_HALO = 32
PEER_HEADS = 8
PEER_HALF = 64
N_KEYS = 128
PEER_TOPK = 16
SEL = PEER_HEADS * PEER_TOPK
EXP_UNDERFLOW = -88.0
HI_MASK = -65536

VMEM_LIMIT = 56 * 1024 * 1024


def _split_bf16(a):
    hi = a.astype(BF16)
    lo = (a - hi.astype(F32)).astype(BF16)
    return hi, lo


def _dot(a, b):
    return jnp.dot(a, b, preferred_element_type=F32)


def _dot_nt(a, b):
    return lax.dot_general(a, b, (((1,), (1,)), ((), ())), preferred_element_type=F32)


def _sigmoid(x):
    return 1.0 / (1.0 + jnp.exp(-x))


def _inproj_kernel(x_ref, g_ref, w_ref, gq_ref, gk_ref, q_ref, k_ref, v_ref, hg_ref):
    x = x_ref[...]
    ms = jnp.mean(x * x, axis=-1, keepdims=True)
    xn = (x * lax.rsqrt(ms + EPS) * g_ref[...]).astype(BF16)
    proj = _dot(xn, w_ref[...])

    r = lax.broadcasted_iota(I32, (ATTN_WIDTH, ATTN_WIDTH), 0) // HEAD_DIM
    c = lax.broadcasted_iota(I32, (ATTN_WIDTH, ATTN_WIDTH), 1) // HEAD_DIM
    same_head = (r == c).astype(BF16)

    def head_norm(t, g):
        hi, lo = _split_bf16(t * t)
        msq = (_dot(hi, same_head) + _dot(lo, same_head)) * (1.0 / HEAD_DIM)
        return t * lax.rsqrt(msq + EPS) * g

    q = head_norm(proj[:, :ATTN_WIDTH], gq_ref[...]) * (HEAD_DIM ** -0.5)
    k = head_norm(proj[:, ATTN_WIDTH:2 * ATTN_WIDTH], gk_ref[...])
    v = proj[:, 2 * ATTN_WIDTH:3 * ATTN_WIDTH]
    for h in range(ATTN_HEADS):
        sl = slice(h * HEAD_DIM, (h + 1) * HEAD_DIM)
        q_ref[h] = q[:, sl].astype(BF16)
        k_ref[h] = k[:, sl].astype(BF16)
        v_ref[h] = v[:, sl].astype(BF16)
    cw = (proj.shape[1] - 3 * ATTN_WIDTH) // 2
    a = proj[:, 3 * ATTN_WIDTH:3 * ATTN_WIDTH + cw]
    gate = proj[:, 3 * ATTN_WIDTH + cw:]
    hg_ref[...] = a * _sigmoid(gate)


def _inproj(x2, g, w_bf, gq_t, gk_t, *, tm):
    T, D = x2.shape
    E = w_bf.shape[1]
    cw = (E - 3 * ATTN_WIDTH) // 2
    head_shape = jax.ShapeDtypeStruct((ATTN_HEADS, T, HEAD_DIM), BF16)
    head_spec = pl.BlockSpec((ATTN_HEADS, tm, HEAD_DIM), lambda i: (0, i, 0))
    return pl.pallas_call(
        _inproj_kernel,
        grid=(T // tm,),
        in_specs=[
            pl.BlockSpec((tm, D), lambda i: (i, 0)),
            pl.BlockSpec((1, D), lambda i: (0, 0)),
            pl.BlockSpec((D, E), lambda i: (0, 0)),
            pl.BlockSpec((1, ATTN_WIDTH), lambda i: (0, 0)),
            pl.BlockSpec((1, ATTN_WIDTH), lambda i: (0, 0)),
        ],
        out_specs=[head_spec, head_spec, head_spec,
                   pl.BlockSpec((tm, cw), lambda i: (i, 0))],
        out_shape=[head_shape, head_shape, head_shape,
                   jax.ShapeDtypeStruct((T, cw), F32)],
        compiler_params=pltpu.CompilerParams(
            dimension_semantics=("arbitrary",), vmem_limit_bytes=VMEM_LIMIT),
        name="inproj",
    )(x2, g, w_bf, gq_t, gk_t)


def _attn_span(q, k, v, offset, carry, suffix):
    tw = suffix.shape[0]
    z = _dot_nt(q, k)
    sp = jnp.maximum(z, 0.0) + jnp.log(1.0 + jnp.exp(-jnp.abs(z)))
    col_minus_row = (lax.broadcasted_iota(I32, z.shape, 1)
                     - lax.broadcasted_iota(I32, z.shape, 0))
    mask = col_minus_row < offset
    log_keep = jnp.where(mask, -sp, 0.0)
    later = []
    for s in reversed(range(z.shape[1] // tw)):
        lk = log_keep[:, s * tw:(s + 1) * tw]
        hi, lo = _split_bf16(lk)
        later.append(carry + (_dot(hi, suffix) + _dot(lo, suffix)))
        carry = carry + jnp.sum(lk, axis=-1, keepdims=True)
    later = jnp.concatenate(later[::-1], axis=-1)
    att = jnp.where(mask, jnp.exp(z - sp + later), 0.0)
    return carry, _dot(att.astype(BF16), v)


def _attn_kernel(q_ref, k_ref, v_ref, o_ref, carry_ref, acc_ref, *, rows, span, tw):
    g = pl.program_id(2)
    hp = q_ref.shape[0]
    suffix = (lax.broadcasted_iota(I32, (tw, tw), 0)
              > lax.broadcasted_iota(I32, (tw, tw), 1)).astype(BF16)

    start = pl.multiple_of(jnp.maximum(g * rows - (span - rows), 0), rows)
    cmax = None
    for hh in range(hp):
        carry, acc = _attn_span(q_ref[hh], k_ref[hh, pl.ds(start, span), :],
                                v_ref[hh, pl.ds(start, span), :], g * rows - start,
                                jnp.zeros((rows, 1), F32), suffix)
        carry_ref[hh] = carry
        acc_ref[hh] = acc
        cmax = carry if cmax is None else jnp.maximum(cmax, carry)

    @pl.when(jnp.max(cmax) > EXP_UNDERFLOW)
    def _():
        for hh in range(hp):

            def cond(st):
                j, carry, _ = st
                return jnp.logical_and(j >= 0, jnp.max(carry) > EXP_UNDERFLOW)

            def body(st, hh=hh):
                j, carry, acc = st
                ks = pl.multiple_of(j * tw, tw)
                carry, out = _attn_span(q_ref[hh], k_ref[hh, pl.ds(ks, tw), :],
                                        v_ref[hh, pl.ds(ks, tw), :], rows + tw, carry, suffix)
                return j - 1, carry, acc + out

            init = (start // tw - 1, carry_ref[hh], acc_ref[hh])
            acc_ref[hh] = lax.while_loop(cond, body, init)[2]

    o_ref[...] = jnp.concatenate([acc_ref[hh] for hh in range(hp)], axis=-1)


def _attention(qh, kh, vh, *, batch, seq, rows=256, span=512, tw=256, heads_per_step=2):
    H, T, hd = qh.shape
    ng = seq // rows
    hp = heads_per_step
    return pl.pallas_call(
        functools.partial(_attn_kernel, rows=rows, span=span, tw=tw),
        grid=(H // hp, batch, ng),
        in_specs=[
            pl.BlockSpec((hp, rows, hd), lambda h, b, i: (h, b * ng + i, 0)),
            pl.BlockSpec((hp, seq, hd), lambda h, b, i: (h, b, 0)),
            pl.BlockSpec((hp, seq, hd), lambda h, b, i: (h, b, 0)),
        ],
        out_specs=pl.BlockSpec((rows, hp * hd), lambda h, b, i: (b * ng + i, h)),
        out_shape=jax.ShapeDtypeStruct((T, H * hd), F32),
        scratch_shapes=[pltpu.VMEM((hp, rows, 1), F32),
                        pltpu.VMEM((hp, rows, hd), F32)],
        compiler_params=pltpu.CompilerParams(
            dimension_semantics=("arbitrary", "arbitrary", "arbitrary"),
            vmem_limit_bytes=VMEM_LIMIT),
        name="sb_attention",
    )(qh, kh, vh)


def _mix_kernel(x_ref, o_ref, hg_ref, hgp_ref, cw_ref, cb_ref, lg_ref, lb_ref, ag_ref,
                wo_ref, g2_ref, wqh_ref, wql_ref, h_ref, xn_ref, qh_ref, ext_ref,
                *, blocks_per_seq):
    tm = x_ref.shape[0]
    first = (pl.program_id(0) % blocks_per_seq) == 0
    ext_ref[0:CONV_HALO, :] = jnp.where(first, 0.0, hgp_ref[...])
    ext_ref[CONV_HALO:, :] = hg_ref[...]
    conv = jnp.zeros(hg_ref.shape, F32)
    for j in range(CONV_KERNEL):
        off = CONV_HALO - (CONV_KERNEL - 1) + j
        conv = conv + cw_ref[j:j + 1, :] * ext_ref[pl.ds(off, tm), :]
    conv = conv + cb_ref[...]
    mu = jnp.mean(conv, axis=-1, keepdims=True)
    xc = conv - mu
    var = jnp.mean(xc * xc, axis=-1, keepdims=True)
    y = xc * lax.rsqrt(var + EPS) * lg_ref[...] + lb_ref[...]
    o_conv = y * _sigmoid(y)

    o = o_ref[...]
    o_attn = o * lax.rsqrt(jnp.mean(o * o, axis=-1, keepdims=True) + EPS) * ag_ref[...]
    mixed = jnp.concatenate([o_attn, o_conv], axis=-1).astype(BF16)
    h = x_ref[...] + _dot(mixed, wo_ref[...])
    h_ref[...] = h

    xn = h * lax.rsqrt(jnp.mean(h * h, axis=-1, keepdims=True) + EPS) * g2_ref[...]
    xn_ref[...] = xn
    hi, lo = _split_bf16(xn)
    wqh = wqh_ref[...]
    qh_ref[...] = _dot(hi, wqh) + _dot(lo, wqh) + _dot(hi, wql_ref[...])


def _mix(x2, o, hg, conv_w, conv_b, ln_g, ln_b, attn_g, wo_bf, g2, wq_hi, wq_lo, *, seq, tm):
    T, D = x2.shape
    cwid = hg.shape[1]
    E = wq_hi.shape[1]
    halo_per_block = tm // CONV_HALO
    row = lambda i: (i, 0)
    const = lambda i: (0, 0)
    return pl.pallas_call(
        functools.partial(_mix_kernel, blocks_per_seq=seq // tm),
        grid=(T // tm,),
        in_specs=[
            pl.BlockSpec((tm, D), row),
            pl.BlockSpec((tm, o.shape[1]), row),
            pl.BlockSpec((tm, cwid), row),
            pl.BlockSpec((CONV_HALO, cwid),
                         lambda i: (jnp.maximum(i * halo_per_block - 1, 0), 0)),
            pl.BlockSpec((CONV_KERNEL, cwid), const),
            pl.BlockSpec((1, cwid), const),
            pl.BlockSpec((1, cwid), const),
            pl.BlockSpec((1, cwid), const),
            pl.BlockSpec((1, o.shape[1]), const),
            pl.BlockSpec(wo_bf.shape, const),
            pl.BlockSpec((1, D), const),
            pl.BlockSpec(wq_hi.shape, const),
            pl.BlockSpec(wq_lo.shape, const),
        ],
        out_specs=[pl.BlockSpec((tm, D), row), pl.BlockSpec((tm, D), row),
                   pl.BlockSpec((tm, E), row)],
        out_shape=[jax.ShapeDtypeStruct((T, D), F32), jax.ShapeDtypeStruct((T, D), F32),
                   jax.ShapeDtypeStruct((T, E), F32)],
        scratch_shapes=[pltpu.VMEM((tm + CONV_HALO, cwid), F32)],
        compiler_params=pltpu.CompilerParams(
            dimension_semantics=("arbitrary",), vmem_limit_bytes=VMEM_LIMIT),
        name="mix_outproj",
    )(x2, o, hg, hg, conv_w, conv_b, ln_g, ln_b, attn_g, wo_bf, g2, wq_hi, wq_lo)


def _top16(s, pos=None, payload=None):
    if pos is None:
        pos = lax.broadcasted_iota(I32, s.shape, 0).astype(F32)
    vals, outs = [], []
    for _ in range(PEER_TOPK):
        m = jnp.max(s, axis=0, keepdims=True)
        idx = jnp.min(jnp.where(s == m, pos, 1e9), axis=0, keepdims=True)
        sel = pos == idx
        if payload is None:
            outs.append(idx)
        else:
            outs.append(jnp.sum(jnp.where(sel, payload, 0.0), axis=0, keepdims=True))
        s = jnp.where(sel, -jnp.inf, s)
        vals.append(m)
    return jnp.concatenate(vals, axis=0), jnp.concatenate(outs, axis=0)


def _dot3_nt(a, b):
    ah, al = _split_bf16(a)
    bh, bl = _split_bf16(b)
    return _dot_nt(ah, bh) + _dot_nt(al, bh) + _dot_nt(ah, bl)


def _pair_candidates(v1, i1, v2, i2):
    r8 = lax.broadcasted_iota(I32, (8, v1.shape[1]), 0).astype(F32)
    sc, ex, ps = [], [], []

    def add(s1, e1, s2, e2, p, keep=None):
        s = s1 + s2
        sc.append(s if keep is None else jnp.where(keep, s, -jnp.inf))
        ex.append(e1 * float(N_KEYS) + e2)
        ps.append(p if keep is None else jnp.where(keep, p, 2e9))

    for a, half in ((0, 0), (0, 1), (1, 0), (2, 0), (3, 0)):
        b = slice(8 * half, 8 * half + 8)
        add(v1[a:a + 1], i1[a:a + 1], v2[b], i2[b], r8 + float(a * PEER_TOPK + 8 * half))
    for b in range(3):
        add(v1[0:8], i1[0:8], v2[b:b + 1], i2[b:b + 1], r8 * float(PEER_TOPK) + float(b),
            keep=r8 >= 4.0)
    add(v1[8:16], i1[8:16], v2[0:1], i2[0:1], (r8 + 8.0) * float(PEER_TOPK))
    return (jnp.concatenate(sc, axis=0), jnp.concatenate(ex, axis=0),
            jnp.concatenate(ps, axis=0))


def _topk_kernel(qh_ref, k1_ref, k2_ref, idx_ref, gate_ref):
    all_experts, all_gates = [], []
    for h in range(PEER_HEADS):
        base = h * 2 * PEER_HALF
        q1 = qh_ref[:, base:base + PEER_HALF]
        q2 = qh_ref[:, base + PEER_HALF:base + 2 * PEER_HALF]
        s1 = _dot3_nt(k1_ref[h], q1)
        s2 = _dot3_nt(k2_ref[h], q2)
        v1, i1 = _top16(s1)
        v2, i2 = _top16(s2)
        cand, cexp, cpos = _pair_candidates(v1, i1, v2, i2)
        top_s, experts = _top16(cand, pos=cpos, payload=cexp)
        e = jnp.exp(top_s - top_s[0:1, :])
        all_experts.append(experts)
        all_gates.append(e / jnp.sum(e, axis=0, keepdims=True))
    idx_ref[...] = jnp.concatenate(all_experts, axis=0).T.astype(I32) * CHUNKS
    gate_ref[...] = jnp.concatenate(all_gates, axis=0).T


def _topk(qh, k1, k2, *, tm):
    T, E = qh.shape
    out_spec = pl.BlockSpec((tm, SEL), lambda i: (i, 0))
    return pl.pallas_call(
        _topk_kernel,
        grid=(T // tm,),
        in_specs=[
            pl.BlockSpec((tm, E), lambda i: (i, 0)),
            pl.BlockSpec(k1.shape, lambda i: (0, 0, 0)),
            pl.BlockSpec(k2.shape, lambda i: (0, 0, 0)),
        ],
        out_specs=[out_spec, out_spec],
        out_shape=[jax.ShapeDtypeStruct((T, SEL), I32), jax.ShapeDtypeStruct((T, SEL), F32)],
        compiler_params=pltpu.CompilerParams(
            dimension_semantics=("arbitrary",), vmem_limit_bytes=VMEM_LIMIT),
        name="peer_topk",
    )(qh, k1, k2)


CHUNKS = 4
CHUNK_STRIDE = SEL + 8


def _unpack(word):
    lo = lax.bitcast_convert_type(word << 16, F32)
    hi = lax.bitcast_convert_type(word & HI_MASK, F32)
    return lo, hi


def _gather_group(idx_ref, tab_ref, buf_ref, t0, slots, part=0, parts=1):
    ids = [idx_ref.at[0, 0, pl.ds((t0 + j) * SEL, SEL)] for j in range(len(slots))]
    for k in range(part * SEL // parts, (part + 1) * SEL // parts):
        for j, s in enumerate(slots):
            r = pl.multiple_of(ids[j][k], CHUNKS)
            buf_ref[s, pl.ds(k, CHUNKS, stride=CHUNK_STRIDE), :] = tab_ref[pl.ds(r, CHUNKS), :]


def _cols(c):
    return slice(c * 128, (c + 1) * 128)


def _chunk(buf_ref, s, c):
    return _unpack(buf_ref[s, c * CHUNK_STRIDE:c * CHUNK_STRIDE + SEL, :])


def _lane_replicated_sum(a):
    ones = jnp.ones((128, 128), BF16)
    hi, lo = _split_bf16(a)
    return _dot(hi, ones) + _dot(lo, ones)


def _diag_mask():
    return (lax.broadcasted_iota(I32, (SEL, 128), 0) == lax.broadcasted_iota(I32, (SEL, 128), 1))


GROUP = 4


def _grouped_tokens(tb, gather, reduce):
    half = (tuple(range(GROUP)), tuple(range(GROUP, 2 * GROUP)))
    gather(0, half[0], 0, 1)

    def trip(i, carry):
        t = 2 * GROUP * i
        for j in range(GROUP):
            gather(t + GROUP, half[1], j, GROUP)
            reduce(t + j, half[0][j])
        nxt = jnp.minimum(t + 2 * GROUP, tb - GROUP)
        for j in range(GROUP):
            gather(nxt, half[0], j, GROUP)
            reduce(t + GROUP + j, half[1][j])
        return carry

    lax.fori_loop(0, tb // (2 * GROUP), trip, 0)


def _peer_u_kernel(idx_ref, x_ref, gate_ref, tab_ref, w_ref, buf_ref, act_ref):
    tb = x_ref.shape[0]
    diag = _diag_mask()

    def gather(t0, slots, part, parts):
        _gather_group(idx_ref, tab_ref, buf_ref, t0, slots, part, parts)

    def reduce(t, slot):
        total = None
        for c in range(CHUNKS):
            lo, hi = _chunk(buf_ref, slot, c)
            term = lo * x_ref[t, c:c + 1, :] + hi * x_ref[t, c + CHUNKS:c + CHUNKS + 1, :]
            total = term if total is None else total + term
        act_rep = _lane_replicated_sum(total)
        act_ref[pl.ds(t, 1), :] = jnp.sum(jnp.where(diag, act_rep, 0.0), axis=0, keepdims=True)

    _grouped_tokens(tb, gather, reduce)
    act = act_ref[...]
    gelu = 0.5 * act * (1.0 + lax.erf(act * (2.0 ** -0.5)))
    w_ref[...] = gate_ref[...] * gelu


def _gather_scratch():
    return pltpu.VMEM((2 * GROUP, CHUNKS * CHUNK_STRIDE, 128), I32)


def _index_spec(tb):
    return pl.BlockSpec((1, 1, tb * SEL), lambda i: (i, 0, 0), memory_space=pltpu.SMEM)


def _peer_u(idx3, x3, gates, tab, *, tb):
    T = x3.shape[0]
    return pl.pallas_call(
        _peer_u_kernel,
        grid=(T // tb,),
        in_specs=[
            _index_spec(tb),
            pl.BlockSpec((tb, 2 * CHUNKS, 128), lambda i: (i, 0, 0)),
            pl.BlockSpec((tb, SEL), lambda i: (i, 0)),
            pl.BlockSpec(memory_space=pltpu.VMEM),
        ],
        out_specs=pl.BlockSpec((tb, SEL), lambda i: (i, 0)),
        out_shape=jax.ShapeDtypeStruct((T, SEL), F32),
        scratch_shapes=[_gather_scratch(), pltpu.VMEM((tb, SEL), F32)],
        compiler_params=pltpu.CompilerParams(
            dimension_semantics=("arbitrary",), vmem_limit_bytes=VMEM_LIMIT),
        name="peer_u",
    )(idx3, x3, gates, tab)


def _peer_v_kernel(idx_ref, w_ref, h_ref, tab_ref, y_ref, buf_ref):
    tb = h_ref.shape[0]
    diag = _diag_mask()

    def gather(t0, slots, part, parts):
        _gather_group(idx_ref, tab_ref, buf_ref, t0, slots, part, parts)

    def reduce(t, slot):
        w_rep = _lane_replicated_sum(jnp.where(diag, w_ref[pl.ds(t, 1), :], 0.0))
        lows, highs = [], []
        for c in range(CHUNKS):
            lo, hi = _chunk(buf_ref, slot, c)
            lows.append(jnp.sum(lo * w_rep, axis=0, keepdims=True))
            highs.append(jnp.sum(hi * w_rep, axis=0, keepdims=True))
        y_ref[pl.ds(t, 1), :] = h_ref[pl.ds(t, 1), :] + jnp.concatenate(lows + highs, axis=-1)

    _grouped_tokens(tb, gather, reduce)


def _peer_v(idx3, w, h2, tab, *, tb, tokens):
    D = h2.shape[1]
    return pl.pallas_call(
        _peer_v_kernel,
        grid=(tokens // tb,),
        in_specs=[
            _index_spec(tb),
            pl.BlockSpec((tb, SEL), lambda i: (i, 0)),
            pl.BlockSpec((tb, D), lambda i: (i, 0)),
            pl.BlockSpec(memory_space=pltpu.VMEM),
        ],
        out_specs=pl.BlockSpec((tb, D), lambda i: (i, 0)),
        out_shape=jax.ShapeDtypeStruct((tokens, D), F32),
        scratch_shapes=[_gather_scratch()],
        compiler_params=pltpu.CompilerParams(
            dimension_semantics=("arbitrary",), vmem_limit_bytes=VMEM_LIMIT),
        name="peer_v",
    )(idx3, w, h2, tab)


SC_CORES = 2
SC_WORKERS = 32
SC_LANES = 16
ROW_WORDS = 512
HALF_WORDS = ROW_WORDS // 2
HALF_VECS = HALF_WORDS // SC_LANES


def _peer_v_sc_kernel(ids_hbm, w_hbm, h_hbm, tab_hbm, y_hbm, ids_v, w_v, h_v, y_v, rows_v, sems,
                      *, first, per):
    base = (lax.axis_index("s") * SC_CORES + lax.axis_index("c")) * per

    def gather(slot, half):
        return pltpu.make_async_copy(tab_hbm.at[ids_v.at[slot, half]], rows_v.at[half],
                                     sems.at[half])

    def load_token(i, slot):
        pltpu.sync_copy(ids_hbm.at[first + base + i], ids_v.at[slot])
        pltpu.sync_copy(w_hbm.at[first + base + i], w_v.at[slot])

    def lanes(j):
        return pl.ds(j * SC_LANES, SC_LANES)

    def accumulate(slot, half):
        off = half * HALF_VECS
        init = (tuple(h_v[lanes(off + j)] for j in range(HALF_VECS))
                + tuple(h_v[lanes(2 * HALF_VECS + off + j)] for j in range(HALF_VECS)))
        slot_vec = jnp.full((SC_LANES,), slot, I32)

        def body(k, acc):
            wk = plsc.load_gather(w_v, [slot_vec, jnp.full((SC_LANES,), k, I32)])
            lows, highs = [], []
            for j in range(HALF_VECS):
                lo, hi = _unpack(rows_v[half, k, lanes(j)])
                lows.append(acc[j] + wk * lo)
                highs.append(acc[HALF_VECS + j] + wk * hi)
            return tuple(lows + highs)

        acc = lax.fori_loop(0, SEL, body, init)
        for j in range(HALF_VECS):
            y_v[lanes(off + j)] = acc[j]
            y_v[lanes(2 * HALF_VECS + off + j)] = acc[HALF_VECS + j]

    load_token(0, 0)
    gather(0, 0).start()

    @pl.loop(0, per)
    def _(i):
        slot = i % 2
        pltpu.sync_copy(h_hbm.at[first + base + i], h_v)
        gather(slot, 1).start()
        gather(slot, 0).wait()
        accumulate(slot, 0)

        @pl.when(i + 1 < per)
        def _():
            load_token(i + 1, 1 - slot)
            gather(1 - slot, 0).start()

        gather(slot, 1).wait()
        accumulate(slot, 1)
        pltpu.sync_copy(y_v, y_hbm.at[base + i])


def _peer_v_sc(half_ids, w, h2, tab_halves, *, first):
    T, D = h2.shape
    n = T - first
    mesh = plsc.VectorSubcoreMesh(core_axis_name="c", subcore_axis_name="s")
    body = functools.partial(_peer_v_sc_kernel, first=first, per=n // SC_WORKERS)
    return pl.kernel(
        body,
        out_type=jax.ShapeDtypeStruct((n, D), F32),
        mesh=mesh,
        scratch_types=[pltpu.VMEM((2, 2, SEL), I32), pltpu.VMEM((2, SEL), F32),
                       pltpu.VMEM((D,), F32), pltpu.VMEM((D,), F32),
                       pltpu.VMEM((2, SEL, HALF_WORDS), I32), pltpu.SemaphoreType.DMA((2,))],
        compiler_params=pltpu.CompilerParams(needs_layout_passes=False),
        name="peer_v_sc",
    )(half_ids, w, h2, tab_halves)


def _pack_table(tab):
    n, d = tab.shape
    bits = lax.bitcast_convert_type(tab.astype(BF16), jnp.uint16).astype(jnp.uint32)
    word = bits[:, :d // 2] | (bits[:, d // 2:] << 16)
    return lax.bitcast_convert_type(word, I32).reshape(n * 4, 128)


def _layer(x2, p, *, batch, seq, tm_in, tm_mix, tm_topk, tb, sc_tokens):
    T, D = x2.shape
    row = lambda a: a.reshape(1, -1)
    w_in_bf = p["w_in"].astype(BF16)
    gq_t = row(jnp.tile(p["q_norm_g"], ATTN_HEADS))
    gk_t = row(jnp.tile(p["k_norm_g"], ATTN_HEADS))
    qh, kh, vh, hg = _inproj(x2, row(p["norm_mix_g"]), w_in_bf, gq_t, gk_t, tm=tm_in)
    o = _attention(qh, kh, vh, batch=batch, seq=seq)

    wq = p["peer_wq"]
    wq_hi = wq.astype(BF16)
    wq_lo = (wq - wq_hi.astype(F32)).astype(BF16)
    h, xn, pq = _mix(x2, o, hg, p["conv_w"], row(p["conv_b"]), row(p["conv_ln_g"]),
                     row(p["conv_ln_b"]), row(p["attn_out_g"]), p["w_out"].astype(BF16),
                     row(p["norm_ffn_g"]), wq_hi, wq_lo, seq=seq, tm=tm_mix)

    idx, gates = _topk(pq, p["peer_k1"], p["peer_k2"], tm=tm_topk)
    idx3 = idx.reshape(T // tb, 1, tb * SEL)
    w = _peer_u(idx3, xn.reshape(T, 2 * CHUNKS, 128), gates, _pack_table(p["peer_u"]), tb=tb)
    v_tab = _pack_table(p["peer_v"])
    head = T - sc_tokens
    y_head = _peer_v(idx3, w, h, v_tab, tb=tb, tokens=head)
    half_row = idx // (CHUNKS // 2)
    half_ids = jnp.stack([half_row, half_row + 1], axis=1)
    y_tail = _peer_v_sc(half_ids, w, h, v_tab.reshape(-1, HALF_WORDS), first=head)
    return jnp.concatenate([y_head, y_tail], axis=0)


def kernel(x, norm_mix_g, w_in, q_norm_g, k_norm_g, attn_out_g, conv_w, conv_b, conv_ln_g,
           conv_ln_b, w_out, norm_ffn_g, peer_wq, peer_k1, peer_k2, peer_u, peer_v):
    batch, seq, d = x.shape
    stacked = dict(norm_mix_g=norm_mix_g, w_in=w_in, q_norm_g=q_norm_g, k_norm_g=k_norm_g,
                   attn_out_g=attn_out_g, conv_w=conv_w, conv_b=conv_b, conv_ln_g=conv_ln_g,
                   conv_ln_b=conv_ln_b, w_out=w_out, norm_ffn_g=norm_ffn_g, peer_wq=peer_wq,
                   peer_k1=peer_k1, peer_k2=peer_k2, peer_u=peer_u, peer_v=peer_v)
    h = x.reshape(batch * seq, d)
    for l in range(w_in.shape[0]):
        p = {name: a[l] for name, a in stacked.items()}
        h = _layer(h, p, batch=batch, seq=seq, tm_in=256, tm_mix=256,
                   tm_topk=256, tb=64, sc_tokens=8192)
    return h.reshape(batch, seq, d)
```

```python
import functools

import jax
import jax.numpy as jnp
from jax import lax
from jax.experimental import pallas as pl
from jax.experimental.pallas import tpu as pltpu
from jax.experimental.pallas import tpu_sc as plsc

F32 = jnp.float32
BF16 = jnp.bfloat16
I32 = jnp.int32

EPS = 1e-6
HEAD_DIM = 64
ATTN_HEADS = 8
ATTN_WIDTH = ATTN_HEADS * HEAD_DIM
CONV_KERNEL = 31
CONV_HALO = 32
PEER_HEADS = 8
PEER_HALF = 64
N_KEYS = 128
PEER_TOPK = 16
SEL = PEER_HEADS * PEER_TOPK
EXP_UNDERFLOW = -88.0
HI_MASK = -65536

VMEM_LIMIT = 56 * 1024 * 1024


def _split_bf16(a):
    hi = a.astype(BF16)
    lo = (a - hi.astype(F32)).astype(BF16)
    return hi, lo


def _dot(a, b):
    return jnp.dot(a, b, preferred_element_type=F32)


def _dot_nt(a, b):
    return lax.dot_general(a, b, (((1,), (1,)), ((), ())), preferred_element_type=F32)


def _sigmoid(x):
    return 1.0 / (1.0 + jnp.exp(-x))


def _inproj_kernel(x_ref, g_ref, w_ref, gq_ref, gk_ref, q_ref, k_ref, v_ref, hg_ref):
    x = x_ref[...]
    ms = jnp.mean(x * x, axis=-1, keepdims=True)
    xn = (x * lax.rsqrt(ms + EPS) * g_ref[...]).astype(BF16)
    proj = _dot(xn, w_ref[...])

    r = lax.broadcasted_iota(I32, (ATTN_WIDTH, ATTN_WIDTH), 0) // HEAD_DIM
    c = lax.broadcasted_iota(I32, (ATTN_WIDTH, ATTN_WIDTH), 1) // HEAD_DIM
    same_head = (r == c).astype(BF16)

    def head_norm(t, g):
        hi, lo = _split_bf16(t * t)
        msq = (_dot(hi, same_head) + _dot(lo, same_head)) * (1.0 / HEAD_DIM)
        return t * lax.rsqrt(msq + EPS) * g

    q = head_norm(proj[:, :ATTN_WIDTH], gq_ref[...]) * (HEAD_DIM ** -0.5)
    k = head_norm(proj[:, ATTN_WIDTH:2 * ATTN_WIDTH], gk_ref[...])
    v = proj[:, 2 * ATTN_WIDTH:3 * ATTN_WIDTH]
    for h in range(ATTN_HEADS):
        sl = slice(h * HEAD_DIM, (h + 1) * HEAD_DIM)
        q_ref[h] = q[:, sl].astype(BF16)
        k_ref[h] = k[:, sl].astype(BF16)
        v_ref[h] = v[:, sl].astype(BF16)
    cw = (proj.shape[1] - 3 * ATTN_WIDTH) // 2
    a = proj[:, 3 * ATTN_WIDTH:3 * ATTN_WIDTH + cw]
    gate = proj[:, 3 * ATTN_WIDTH + cw:]
    hg_ref[...] = a * _sigmoid(gate)


def _inproj(x2, g, w_bf, gq_t, gk_t, *, tm):
    T, D = x2.shape
    E = w_bf.shape[1]
    cw = (E - 3 * ATTN_WIDTH) // 2
    head_shape = jax.ShapeDtypeStruct((ATTN_HEADS, T, HEAD_DIM), BF16)
    head_spec = pl.BlockSpec((ATTN_HEADS, tm, HEAD_DIM), lambda i: (0, i, 0))
    return pl.pallas_call(
        _inproj_kernel,
        grid=(T // tm,),
        in_specs=[
            pl.BlockSpec((tm, D), lambda i: (i, 0)),
            pl.BlockSpec((1, D), lambda i: (0, 0)),
            pl.BlockSpec((D, E), lambda i: (0, 0)),
            pl.BlockSpec((1, ATTN_WIDTH), lambda i: (0, 0)),
            pl.BlockSpec((1, ATTN_WIDTH), lambda i: (0, 0)),
        ],
        out_specs=[head_spec, head_spec, head_spec,
                   pl.BlockSpec((tm, cw), lambda i: (i, 0))],
        out_shape=[head_shape, head_shape, head_shape,
                   jax.ShapeDtypeStruct((T, cw), F32)],
        compiler_params=pltpu.CompilerParams(
            dimension_semantics=("arbitrary",), vmem_limit_bytes=VMEM_LIMIT),
        name="inproj",
    )(x2, g, w_bf, gq_t, gk_t)


def _attn_span(q, k, v, offset, carry, suffix):
    tw = suffix.shape[0]
    z = _dot_nt(q, k)
    sp = jnp.maximum(z, 0.0) + jnp.log(1.0 + jnp.exp(-jnp.abs(z)))
    col_minus_row = (lax.broadcasted_iota(I32, z.shape, 1)
                     - lax.broadcasted_iota(I32, z.shape, 0))
    mask = col_minus_row < offset
    log_keep = jnp.where(mask, -sp, 0.0)
    later = []
    for s in reversed(range(z.shape[1] // tw)):
        lk = log_keep[:, s * tw:(s + 1) * tw]
        hi, lo = _split_bf16(lk)
        later.append(carry + (_dot(hi, suffix) + _dot(lo, suffix)))
        carry = carry + jnp.sum(lk, axis=-1, keepdims=True)
    later = jnp.concatenate(later[::-1], axis=-1)
    att = jnp.where(mask, jnp.exp(z - sp + later), 0.0)
    return carry, _dot(att.astype(BF16), v)


def _attn_kernel(q_ref, k_ref, v_ref, o_ref, carry_ref, acc_ref, *, rows, span, tw):
    g = pl.program_id(2)
    hp = q_ref.shape[0]
    suffix = (lax.broadcasted_iota(I32, (tw, tw), 0)
              > lax.broadcasted_iota(I32, (tw, tw), 1)).astype(BF16)

    start = pl.multiple_of(jnp.maximum(g * rows - (span - rows), 0), rows)
    cmax = None
    for hh in range(hp):
        carry, acc = _attn_span(q_ref[hh], k_ref[hh, pl.ds(start, span), :],
                                v_ref[hh, pl.ds(start, span), :], g * rows - start,
                                jnp.zeros((rows, 1), F32), suffix)
        carry_ref[hh] = carry
        acc_ref[hh] = acc
        cmax = carry if cmax is None else jnp.maximum(cmax, carry)

    @pl.when(jnp.max(cmax) > EXP_UNDERFLOW)
    def _():
        for hh in range(hp):

            def cond(st):
                j, carry, _ = st
                return jnp.logical_and(j >= 0, jnp.max(carry) > EXP_UNDERFLOW)

            def body(st, hh=hh):
                j, carry, acc = st
                ks = pl.multiple_of(j * tw, tw)
                carry, out = _attn_span(q_ref[hh], k_ref[hh, pl.ds(ks, tw), :],
                                        v_ref[hh, pl.ds(ks, tw), :], rows + tw, carry, suffix)
                return j - 1, carry, acc + out

            init = (start // tw - 1, carry_ref[hh], acc_ref[hh])
            acc_ref[hh] = lax.while_loop(cond, body, init)[2]

    o_ref[...] = jnp.concatenate([acc_ref[hh] for hh in range(hp)], axis=-1)


def _attention(qh, kh, vh, *, batch, seq, rows=256, span=512, tw=256, heads_per_step=2):
    H, T, hd = qh.shape
    ng = seq // rows
    hp = heads_per_step
    return pl.pallas_call(
        functools.partial(_attn_kernel, rows=rows, span=span, tw=tw),
        grid=(H // hp, batch, ng),
        in_specs=[
            pl.BlockSpec((hp, rows, hd), lambda h, b, i: (h, b * ng + i, 0)),
            pl.BlockSpec((hp, seq, hd), lambda h, b, i: (h, b, 0)),
            pl.BlockSpec((hp, seq, hd), lambda h, b, i: (h, b, 0)),
        ],
        out_specs=pl.BlockSpec((rows, hp * hd), lambda h, b, i: (b * ng + i, h)),
        out_shape=jax.ShapeDtypeStruct((T, H * hd), F32),
        scratch_shapes=[pltpu.VMEM((hp, rows, 1), F32),
                        pltpu.VMEM((hp, rows, hd), F32)],
        compiler_params=pltpu.CompilerParams(
            dimension_semantics=("arbitrary", "arbitrary", "arbitrary"),
            vmem_limit_bytes=VMEM_LIMIT),
        name="sb_attention",
    )(qh, kh, vh)


def _mix_kernel(x_ref, o_ref, hg_ref, hgp_ref, cw_ref, cb_ref, lg_ref, lb_ref, ag_ref,
                wo_ref, g2_ref, wqh_ref, wql_ref, h_ref, xn_ref, qh_ref, ext_ref,
                *, blocks_per_seq):
    tm = x_ref.shape[0]
    first = (pl.program_id(0) % blocks_per_seq) == 0
    ext_ref[0:CONV_HALO, :] = jnp.where(first, 0.0, hgp_ref[...])
    ext_ref[CONV_HALO:, :] = hg_ref[...]
    conv = jnp.zeros(hg_ref.shape, F32)
    for j in range(CONV_KERNEL):
        off = CONV_HALO - (CONV_KERNEL - 1) + j
        conv = conv + cw_ref[j:j + 1, :] * ext_ref[pl.ds(off, tm), :]
    conv = conv + cb_ref[...]
    mu = jnp.mean(conv, axis=-1, keepdims=True)
    xc = conv - mu
    var = jnp.mean(xc * xc, axis=-1, keepdims=True)
    y = xc * lax.rsqrt(var + EPS) * lg_ref[...] + lb_ref[...]
    o_conv = y * _sigmoid(y)

    o = o_ref[...]
    o_attn = o * lax.rsqrt(jnp.mean(o * o, axis=-1, keepdims=True) + EPS) * ag_ref[...]
    mixed = jnp.concatenate([o_attn, o_conv], axis=-1).astype(BF16)
    h = x_ref[...] + _dot(mixed, wo_ref[...])
    h_ref[...] = h

    xn = h * lax.rsqrt(jnp.mean(h * h, axis=-1, keepdims=True) + EPS) * g2_ref[...]
    xn_ref[...] = xn
    hi, lo = _split_bf16(xn)
    wqh = wqh_ref[...]
    qh_ref[...] = _dot(hi, wqh) + _dot(lo, wqh) + _dot(hi, wql_ref[...])


def _mix(x2, o, hg, conv_w, conv_b, ln_g, ln_b, attn_g, wo_bf, g2, wq_hi, wq_lo, *, seq, tm):
    T, D = x2.shape
    cwid = hg.shape[1]
    E = wq_hi.shape[1]
    halo_per_block = tm // CONV_HALO
    row = lambda i: (i, 0)
    const = lambda i: (0, 0)
    return pl.pallas_call(
        functools.partial(_mix_kernel, blocks_per_seq=seq // tm),
        grid=(T // tm,),
        in_specs=[
            pl.BlockSpec((tm, D), row),
            pl.BlockSpec((tm, o.shape[1]), row),
            pl.BlockSpec((tm, cwid), row),
            pl.BlockSpec((CONV_HALO, cwid),
                         lambda i: (jnp.maximum(i * halo_per_block - 1, 0), 0)),
            pl.BlockSpec((CONV_KERNEL, cwid), const),
            pl.BlockSpec((1, cwid), const),
            pl.BlockSpec((1, cwid), const),
            pl.BlockSpec((1, cwid), const),
            pl.BlockSpec((1, o.shape[1]), const),
            pl.BlockSpec(wo_bf.shape, const),
            pl.BlockSpec((1, D), const),
            pl.BlockSpec(wq_hi.shape, const),
            pl.BlockSpec(wq_lo.shape, const),
        ],
        out_specs=[pl.BlockSpec((tm, D), row), pl.BlockSpec((tm, D), row),
                   pl.BlockSpec((tm, E), row)],
        out_shape=[jax.ShapeDtypeStruct((T, D), F32), jax.ShapeDtypeStruct((T, D), F32),
                   jax.ShapeDtypeStruct((T, E), F32)],
        scratch_shapes=[pltpu.VMEM((tm + CONV_HALO, cwid), F32)],
        compiler_params=pltpu.CompilerParams(
            dimension_semantics=("arbitrary",), vmem_limit_bytes=VMEM_LIMIT),
        name="mix_outproj",
    )(x2, o, hg, hg, conv_w, conv_b, ln_g, ln_b, attn_g, wo_bf, g2, wq_hi, wq_lo)


def _top16(s, pos=None, payload=None):
    if pos is None:
        pos = lax.broadcasted_iota(I32, s.shape, 0).astype(F32)
    vals, outs = [], []
    for _ in range(PEER_TOPK):
        m = jnp.max(s, axis=0, keepdims=True)
        idx = jnp.min(jnp.where(s == m, pos, 1e9), axis=0, keepdims=True)
        sel = pos == idx
        if payload is None:
            outs.append(idx)
        else:
            outs.append(jnp.sum(jnp.where(sel, payload, 0.0), axis=0, keepdims=True))
        s = jnp.where(sel, -jnp.inf, s)
        vals.append(m)
    return jnp.concatenate(vals, axis=0), jnp.concatenate(outs, axis=0)


def _dot3_nt(a, b):
    ah, al = _split_bf16(a)
    bh, bl = _split_bf16(b)
    return _dot_nt(ah, bh) + _dot_nt(al, bh) + _dot_nt(ah, bl)


def _pair_candidates(v1, i1, v2, i2):
    r8 = lax.broadcasted_iota(I32, (8, v1.shape[1]), 0).astype(F32)
    sc, ex, ps = [], [], []

    def add(s1, e1, s2, e2, p, keep=None):
        s = s1 + s2
        sc.append(s if keep is None else jnp.where(keep, s, -jnp.inf))
        ex.append(e1 * float(N_KEYS) + e2)
        ps.append(p if keep is None else jnp.where(keep, p, 2e9))

    for a, half in ((0, 0), (0, 1), (1, 0), (2, 0), (3, 0)):
        b = slice(8 * half, 8 * half + 8)
        add(v1[a:a + 1], i1[a:a + 1], v2[b], i2[b], r8 + float(a * PEER_TOPK + 8 * half))
    for b in range(3):
        add(v1[0:8], i1[0:8], v2[b:b + 1], i2[b:b + 1], r8 * float(PEER_TOPK) + float(b),
            keep=r8 >= 4.0)
    add(v1[8:16], i1[8:16], v2[0:1], i2[0:1], (r8 + 8.0) * float(PEER_TOPK))
    return (jnp.concatenate(sc, axis=0), jnp.concatenate(ex, axis=0),
            jnp.concatenate(ps, axis=0))


def _topk_kernel(qh_ref, k1_ref, k2_ref, idx_ref, gate_ref):
    all_experts, all_gates = [], []
    for h in range(PEER_HEADS):
        base = h * 2 * PEER_HALF
        q1 = qh_ref[:, base:base + PEER_HALF]
        q2 = qh_ref[:, base + PEER_HALF:base + 2 * PEER_HALF]
        s1 = _dot3_nt(k1_ref[h], q1)
        s2 = _dot3_nt(k2_ref[h], q2)
        v1, i1 = _top16(s1)
        v2, i2 = _top16(s2)
        cand, cexp, cpos = _pair_candidates(v1, i1, v2, i2)
        top_s, experts = _top16(cand, pos=cpos, payload=cexp)
        e = jnp.exp(top_s - top_s[0:1, :])
        all_experts.append(experts)
        all_gates.append(e / jnp.sum(e, axis=0, keepdims=True))
    idx_ref[...] = jnp.concatenate(all_experts, axis=0).T.astype(I32) * CHUNKS
    gate_ref[...] = jnp.concatenate(all_gates, axis=0).T


def _topk(qh, k1, k2, *, tm):
    T, E = qh.shape
    out_spec = pl.BlockSpec((tm, SEL), lambda i: (i, 0))
    return pl.pallas_call(
        _topk_kernel,
        grid=(T // tm,),
        in_specs=[
            pl.BlockSpec((tm, E), lambda i: (i, 0)),
            pl.BlockSpec(k1.shape, lambda i: (0, 0, 0)),
            pl.BlockSpec(k2.shape, lambda i: (0, 0, 0)),
        ],
        out_specs=[out_spec, out_spec],
        out_shape=[jax.ShapeDtypeStruct((T, SEL), I32), jax.ShapeDtypeStruct((T, SEL), F32)],
        compiler_params=pltpu.CompilerParams(
            dimension_semantics=("arbitrary",), vmem_limit_bytes=VMEM_LIMIT),
        name="peer_topk",
    )(qh, k1, k2)


CHUNKS = 4
CHUNK_STRIDE = SEL + 8


def _unpack(word):
    lo = lax.bitcast_convert_type(word << 16, F32)
    hi = lax.bitcast_convert_type(word & HI_MASK, F32)
    return lo, hi


def _gather_group(idx_ref, tab_ref, buf_ref, t0, slots, part=0, parts=1):
    ids = [idx_ref.at[0, 0, pl.ds((t0 + j) * SEL, SEL)] for j in range(len(slots))]
    for k in range(part * SEL // parts, (part + 1) * SEL // parts):
        for j, s in enumerate(slots):
            r = pl.multiple_of(ids[j][k], CHUNKS)
            buf_ref[s, pl.ds(k, CHUNKS, stride=CHUNK_STRIDE), :] = tab_ref[pl.ds(r, CHUNKS), :]


def _cols(c):
    return slice(c * 128, (c + 1) * 128)


def _chunk(buf_ref, s, c):
    return _unpack(buf_ref[s, c * CHUNK_STRIDE:c * CHUNK_STRIDE + SEL, :])


def _lane_replicated_sum(a):
    ones = jnp.ones((128, 128), BF16)
    hi, lo = _split_bf16(a)
    return _dot(hi, ones) + _dot(lo, ones)


def _diag_mask():
    return (lax.broadcasted_iota(I32, (SEL, 128), 0) == lax.broadcasted_iota(I32, (SEL, 128), 1))


GROUP = 4


def _grouped_tokens(tb, gather, reduce):
    half = (tuple(range(GROUP)), tuple(range(GROUP, 2 * GROUP)))
    gather(0, half[0], 0, 1)

    def trip(i, carry):
        t = 2 * GROUP * i
        for j in range(GROUP):
            gather(t + GROUP, half[1], j, GROUP)
            reduce(t + j, half[0][j])
        nxt = jnp.minimum(t + 2 * GROUP, tb - GROUP)
        for j in range(GROUP):
            gather(nxt, half[0], j, GROUP)
            reduce(t + GROUP + j, half[1][j])
        return carry

    lax.fori_loop(0, tb // (2 * GROUP), trip, 0)


def _peer_u_kernel(idx_ref, x_ref, gate_ref, tab_ref, w_ref, buf_ref, act_ref):
    tb = x_ref.shape[0]
    diag = _diag_mask()

    def gather(t0, slots, part, parts):
        _gather_group(idx_ref, tab_ref, buf_ref, t0, slots, part, parts)

    def reduce(t, slot):
        total = None
        for c in range(CHUNKS):
            lo, hi = _chunk(buf_ref, slot, c)
            term = lo * x_ref[t, c:c + 1, :] + hi * x_ref[t, c + CHUNKS:c + CHUNKS + 1, :]
            total = term if total is None else total + term
        act_rep = _lane_replicated_sum(total)
        act_ref[pl.ds(t, 1), :] = jnp.sum(jnp.where(diag, act_rep, 0.0), axis=0, keepdims=True)

    _grouped_tokens(tb, gather, reduce)
    act = act_ref[...]
    gelu = 0.5 * act * (1.0 + lax.erf(act * (2.0 ** -0.5)))
    w_ref[...] = gate_ref[...] * gelu


def _gather_scratch():
    return pltpu.VMEM((2 * GROUP, CHUNKS * CHUNK_STRIDE, 128), I32)


def _index_spec(tb):
    return pl.BlockSpec((1, 1, tb * SEL), lambda i: (i, 0, 0), memory_space=pltpu.SMEM)


def _peer_u(idx3, x3, gates, tab, *, tb, tokens):
    return pl.pallas_call(
        _peer_u_kernel,
        grid=(tokens // tb,),
        in_specs=[
            _index_spec(tb),
            pl.BlockSpec((tb, 2 * CHUNKS, 128), lambda i: (i, 0, 0)),
            pl.BlockSpec((tb, SEL), lambda i: (i, 0)),
            pl.BlockSpec(memory_space=pltpu.VMEM),
        ],
        out_specs=pl.BlockSpec((tb, SEL), lambda i: (i, 0)),
        out_shape=jax.ShapeDtypeStruct((tokens, SEL), F32),
        scratch_shapes=[_gather_scratch(), pltpu.VMEM((tb, SEL), F32)],
        compiler_params=pltpu.CompilerParams(
            dimension_semantics=("arbitrary",), vmem_limit_bytes=VMEM_LIMIT),
        name="peer_u",
    )(idx3, x3, gates, tab)


def _peer_v_kernel(idx_ref, w_ref, h_ref, tab_ref, y_ref, buf_ref):
    tb = h_ref.shape[0]
    diag = _diag_mask()

    def gather(t0, slots, part, parts):
        _gather_group(idx_ref, tab_ref, buf_ref, t0, slots, part, parts)

    def reduce(t, slot):
        w_rep = _lane_replicated_sum(jnp.where(diag, w_ref[pl.ds(t, 1), :], 0.0))
        lows, highs = [], []
        for c in range(CHUNKS):
            lo, hi = _chunk(buf_ref, slot, c)
            lows.append(jnp.sum(lo * w_rep, axis=0, keepdims=True))
            highs.append(jnp.sum(hi * w_rep, axis=0, keepdims=True))
        y_ref[pl.ds(t, 1), :] = h_ref[pl.ds(t, 1), :] + jnp.concatenate(lows + highs, axis=-1)

    _grouped_tokens(tb, gather, reduce)


def _peer_v(idx3, w, h2, tab, *, tb, tokens):
    D = h2.shape[1]
    return pl.pallas_call(
        _peer_v_kernel,
        grid=(tokens // tb,),
        in_specs=[
            _index_spec(tb),
            pl.BlockSpec((tb, SEL), lambda i: (i, 0)),
            pl.BlockSpec((tb, D), lambda i: (i, 0)),
            pl.BlockSpec(memory_space=pltpu.VMEM),
        ],
        out_specs=pl.BlockSpec((tb, D), lambda i: (i, 0)),
        out_shape=jax.ShapeDtypeStruct((tokens, D), F32),
        scratch_shapes=[_gather_scratch()],
        compiler_params=pltpu.CompilerParams(
            dimension_semantics=("arbitrary",), vmem_limit_bytes=VMEM_LIMIT),
        name="peer_v",
    )(idx3, w, h2, tab)


SC_CORES = 2
SC_WORKERS = 32
SC_LANES = 16
ROW_WORDS = 512
HALF_WORDS = ROW_WORDS // 2
HALF_VECS = HALF_WORDS // SC_LANES


def _sc_lanes(j):
    return pl.ds(j * SC_LANES, SC_LANES)


def _sc_token_pipeline(ids_hbm, tab_hbm, ids_v, rows_v, sems, *, first, per, load_extra, process):
    base = (lax.axis_index("s") * SC_CORES + lax.axis_index("c")) * per

    def gather(slot, half):
        return pltpu.make_async_copy(tab_hbm.at[ids_v.at[slot, half]], rows_v.at[half],
                                     sems.at[half])

    def load_token(i, slot):
        pltpu.sync_copy(ids_hbm.at[first + base + i], ids_v.at[slot])
        load_extra(base + i, slot)

    load_token(0, 0)
    gather(0, 0).start()

    @pl.loop(0, per)
    def _(i):
        slot = i % 2
        gather(slot, 1).start()
        gather(slot, 0).wait()
        process(base + i, slot, 0)

        @pl.when(i + 1 < per)
        def _():
            load_token(i + 1, 1 - slot)
            gather(1 - slot, 0).start()

        gather(slot, 1).wait()
        process(base + i, slot, 1)


def _peer_u_sc_kernel(ids_hbm, x_hbm, tab_hbm, act_hbm, ids_v, x_v, part_v, act_v, rows_v, sems,
                      *, first, per):
    def load_extra(local, slot):
        pltpu.sync_copy(x_hbm.at[first + local], x_v.at[slot])

    def process(local, slot, half):
        off = half * HALF_VECS
        x_lo = [x_v[slot, _sc_lanes(off + j)] for j in range(HALF_VECS)]
        x_hi = [x_v[slot, _sc_lanes(2 * HALF_VECS + off + j)] for j in range(HALF_VECS)]

        @pl.loop(0, SEL)
        def _(k):
            sums = [None] * 4
            for j in range(HALF_VECS):
                lo, hi = _unpack(rows_v[half, k, _sc_lanes(j)])
                term = lo * x_lo[j] + hi * x_hi[j]
                sums[j % 4] = term if sums[j % 4] is None else sums[j % 4] + term
            part_v[half * SEL + k, :] = (sums[0] + sums[1]) + (sums[2] + sums[3])

        if half == 1:
            lane_ids = lax.broadcasted_iota(I32, (SC_LANES,), 0)
            for kb in range(SEL // SC_LANES):
                rows0 = lane_ids + kb * SC_LANES
                total = None
                for l in range(SC_LANES):
                    col = jnp.full((SC_LANES,), l, I32)
                    both = (plsc.load_gather(part_v, [rows0, col])
                            + plsc.load_gather(part_v, [rows0 + SEL, col]))
                    total = both if total is None else total + both
                act_v[_sc_lanes(kb)] = total
            pltpu.sync_copy(act_v, act_hbm.at[local])

    _sc_token_pipeline(ids_hbm, tab_hbm, ids_v, rows_v, sems, first=first, per=per,
                       load_extra=load_extra, process=process)


def _sc_call(body, out_cols, n, scratch, name):
    return pl.kernel(
        body,
        out_type=jax.ShapeDtypeStruct((n, out_cols), F32),
        mesh=plsc.VectorSubcoreMesh(core_axis_name="c", subcore_axis_name="s"),
        scratch_types=scratch + [pltpu.VMEM((2, SEL, HALF_WORDS), I32),
                                 pltpu.SemaphoreType.DMA((2,))],
        compiler_params=pltpu.CompilerParams(needs_layout_passes=False),
        name=name,
    )


def _peer_u_sc(half_ids, x2, tab_halves, *, first):
    T, D = x2.shape
    n = T - first
    body = functools.partial(_peer_u_sc_kernel, first=first, per=n // SC_WORKERS)
    scratch = [pltpu.VMEM((2, 2, SEL), I32), pltpu.VMEM((2, D), F32),
               pltpu.VMEM((2 * SEL, SC_LANES), F32), pltpu.VMEM((SEL,), F32)]
    return _sc_call(body, SEL, n, scratch, "peer_u_sc")(half_ids, x2, tab_halves)


def _peer_v_sc_kernel(ids_hbm, w_hbm, h_hbm, tab_hbm, y_hbm, ids_v, w_v, h_v, y_v, rows_v, sems,
                      *, first, per):
    def load_extra(local, slot):
        pltpu.sync_copy(w_hbm.at[local], w_v.at[slot])
        pltpu.sync_copy(h_hbm.at[first + local], h_v.at[slot])

    def process(local, slot, half):
        off = half * HALF_VECS
        init = (tuple(h_v[slot, _sc_lanes(off + j)] for j in range(HALF_VECS))
                + tuple(h_v[slot, _sc_lanes(2 * HALF_VECS + off + j)] for j in range(HALF_VECS)))
        slot_vec = jnp.full((SC_LANES,), slot, I32)

        def body(k, acc):
            wk = plsc.load_gather(w_v, [slot_vec, jnp.full((SC_LANES,), k, I32)])
            lows, highs = [], []
            for j in range(HALF_VECS):
                lo, hi = _unpack(rows_v[half, k, _sc_lanes(j)])
                lows.append(acc[j] + wk * lo)
                highs.append(acc[HALF_VECS + j] + wk * hi)
            return tuple(lows + highs)

        acc = lax.fori_loop(0, SEL, body, init)
        for j in range(HALF_VECS):
            y_v[_sc_lanes(off + j)] = acc[j]
            y_v[_sc_lanes(2 * HALF_VECS + off + j)] = acc[HALF_VECS + j]
        if half == 1:
            pltpu.sync_copy(y_v, y_hbm.at[local])

    _sc_token_pipeline(ids_hbm, tab_hbm, ids_v, rows_v, sems, first=first, per=per,
                       load_extra=load_extra, process=process)


def _peer_v_sc(half_ids, w_tail, h2, tab_halves, *, first):
    T, D = h2.shape
    n = T - first
    body = functools.partial(_peer_v_sc_kernel, first=first, per=n // SC_WORKERS)
    scratch = [pltpu.VMEM((2, 2, SEL), I32), pltpu.VMEM((2, SEL), F32),
               pltpu.VMEM((2, D), F32), pltpu.VMEM((D,), F32)]
    return _sc_call(body, D, n, scratch, "peer_v_sc")(half_ids, w_tail, h2, tab_halves)


def _gelu_gate_kernel(act_ref, gate_ref, w_ref):
    act = act_ref[...]
    w_ref[...] = gate_ref[...] * (0.5 * act * (1.0 + lax.erf(act * (2.0 ** -0.5))))


def _gelu_gate(act_tail, gates, *, first, tm):
    n = act_tail.shape[0]
    return pl.pallas_call(
        _gelu_gate_kernel,
        grid=(n // tm,),
        in_specs=[pl.BlockSpec((tm, SEL), lambda i: (i, 0)),
                  pl.BlockSpec((tm, SEL), lambda i: (i + first // tm, 0))],
        out_specs=pl.BlockSpec((tm, SEL), lambda i: (i, 0)),
        out_shape=jax.ShapeDtypeStruct((n, SEL), F32),
        name="peer_gelu_gate",
    )(act_tail, gates)


def _pack_table(tab):
    n, d = tab.shape
    bits = lax.bitcast_convert_type(tab.astype(BF16), jnp.uint16).astype(jnp.uint32)
    word = bits[:, :d // 2] | (bits[:, d // 2:] << 16)
    return lax.bitcast_convert_type(word, I32).reshape(n * 4, 128)


def _layer(x2, p, *, batch, seq, tm_in, tm_mix, tm_topk, tb, sc_tokens):
    T, D = x2.shape
    row = lambda a: a.reshape(1, -1)
    w_in_bf = p["w_in"].astype(BF16)
    gq_t = row(jnp.tile(p["q_norm_g"], ATTN_HEADS))
    gk_t = row(jnp.tile(p["k_norm_g"], ATTN_HEADS))
    qh, kh, vh, hg = _inproj(x2, row(p["norm_mix_g"]), w_in_bf, gq_t, gk_t, tm=tm_in)
    o = _attention(qh, kh, vh, batch=batch, seq=seq)

    wq = p["peer_wq"]
    wq_hi = wq.astype(BF16)
    wq_lo = (wq - wq_hi.astype(F32)).astype(BF16)
    h, xn, pq = _mix(x2, o, hg, p["conv_w"], row(p["conv_b"]), row(p["conv_ln_g"]),
                     row(p["conv_ln_b"]), row(p["attn_out_g"]), p["w_out"].astype(BF16),
                     row(p["norm_ffn_g"]), wq_hi, wq_lo, seq=seq, tm=tm_mix)

    idx, gates = _topk(pq, p["peer_k1"], p["peer_k2"], tm=tm_topk)
    idx3 = idx.reshape(T // tb, 1, tb * SEL)
    u_tab = _pack_table(p["peer_u"])
    v_tab = _pack_table(p["peer_v"])
    head = T - sc_tokens
    half_row = idx // (CHUNKS // 2)
    half_ids = jnp.stack([half_row, half_row + 1], axis=1)
    w_head = _peer_u(idx3, xn.reshape(T, 2 * CHUNKS, 128), gates, u_tab, tb=tb, tokens=head)
    act_tail = _peer_u_sc(half_ids, xn, u_tab.reshape(-1, HALF_WORDS), first=head)
    w_tail = _gelu_gate(act_tail, gates, first=head, tm=tm_topk)
    y_head = _peer_v(idx3, w_head, h, v_tab, tb=tb, tokens=head)
    y_tail = _peer_v_sc(half_ids, w_tail, h, v_tab.reshape(-1, HALF_WORDS), first=head)
    return jnp.concatenate([y_head, y_tail], axis=0)


def kernel(x, norm_mix_g, w_in, q_norm_g, k_norm_g, attn_out_g, conv_w, conv_b, conv_ln_g,
           conv_ln_b, w_out, norm_ffn_g, peer_wq, peer_k1, peer_k2, peer_u, peer_v):
    batch, seq, d = x.shape
    stacked = dict(norm_mix_g=norm_mix_g, w_in=w_in, q_norm_g=q_norm_g, k_norm_g=k_norm_g,
                   attn_out_g=attn_out_g, conv_w=conv_w, conv_b=conv_b, conv_ln_g=conv_ln_g,
                   conv_ln_b=conv_ln_b, w_out=w_out, norm_ffn_g=norm_ffn_g, peer_wq=peer_wq,
                   peer_k1=peer_k1, peer_k2=peer_k2, peer_u=peer_u, peer_v=peer_v)
    h = x.reshape(batch * seq, d)
    for l in range(w_in.shape[0]):
        p = {name: a[l] for name, a in stacked.items()}
        h = _layer(h, p, batch=batch, seq=seq, tm_in=256, tm_mix=256,
                   tm_topk=256, tb=64, sc_tokens=12288)
    return h.reshape(batch, seq, d)
```

```python
import functools

import jax
import jax.numpy as jnp
from jax import lax
from jax.experimental import pallas as pl
from jax.experimental.pallas import tpu as pltpu
from jax.experimental.pallas import tpu_sc as plsc

F32 = jnp.float32
BF16 = jnp.bfloat16
I32 = jnp.int32

EPS = 1e-6
HEAD_DIM = 64
ATTN_HEADS = 8
ATTN_WIDTH = ATTN_HEADS * HEAD_DIM
CONV_KERNEL = 31
CONV_HALO = 32
PEER_HEADS = 8
PEER_HALF = 64
N_KEYS = 128
PEER_TOPK = 16
SEL = PEER_HEADS * PEER_TOPK
EXP_UNDERFLOW = -88.0
HI_MASK = -65536

VMEM_LIMIT = 56 * 1024 * 1024


def _split_bf16(a):
    hi = a.astype(BF16)
    lo = (a - hi.astype(F32)).astype(BF16)
    return hi, lo


def _dot(a, b):
    return jnp.dot(a, b, preferred_element_type=F32)


def _dot_nt(a, b):
    return lax.dot_general(a, b, (((1,), (1,)), ((), ())), preferred_element_type=F32)


def _sigmoid(x):
    return 1.0 / (1.0 + jnp.exp(-x))


def _inproj_kernel(x_ref, g_ref, w_ref, gq_ref, gk_ref, q_ref, k_ref, v_ref, hg_ref):
    x = x_ref[...]
    ms = jnp.mean(x * x, axis=-1, keepdims=True)
    xn = (x * lax.rsqrt(ms + EPS) * g_ref[...]).astype(BF16)
    proj = _dot(xn, w_ref[...])

    r = lax.broadcasted_iota(I32, (ATTN_WIDTH, ATTN_WIDTH), 0) // HEAD_DIM
    c = lax.broadcasted_iota(I32, (ATTN_WIDTH, ATTN_WIDTH), 1) // HEAD_DIM
    same_head = (r == c).astype(BF16)

    def head_norm(t, g):
        hi, lo = _split_bf16(t * t)
        msq = (_dot(hi, same_head) + _dot(lo, same_head)) * (1.0 / HEAD_DIM)
        return t * lax.rsqrt(msq + EPS) * g

    q = head_norm(proj[:, :ATTN_WIDTH], gq_ref[...]) * (HEAD_DIM ** -0.5)
    k = head_norm(proj[:, ATTN_WIDTH:2 * ATTN_WIDTH], gk_ref[...])
    v = proj[:, 2 * ATTN_WIDTH:3 * ATTN_WIDTH]
    for h in range(ATTN_HEADS):
        sl = slice(h * HEAD_DIM, (h + 1) * HEAD_DIM)
        q_ref[h] = q[:, sl].astype(BF16)
        k_ref[h] = k[:, sl].astype(BF16)
        v_ref[h] = v[:, sl].astype(BF16)
    cw = (proj.shape[1] - 3 * ATTN_WIDTH) // 2
    a = proj[:, 3 * ATTN_WIDTH:3 * ATTN_WIDTH + cw]
    gate = proj[:, 3 * ATTN_WIDTH + cw:]
    hg_ref[...] = a * _sigmoid(gate)


def _inproj(x2, g, w_bf, gq_t, gk_t, *, tm):
    T, D = x2.shape
    E = w_bf.shape[1]
    cw = (E - 3 * ATTN_WIDTH) // 2
    head_shape = jax.ShapeDtypeStruct((ATTN_HEADS, T, HEAD_DIM), BF16)
    head_spec = pl.BlockSpec((ATTN_HEADS, tm, HEAD_DIM), lambda i: (0, i, 0))
    return pl.pallas_call(
        _inproj_kernel,
        grid=(T // tm,),
        in_specs=[
            pl.BlockSpec((tm, D), lambda i: (i, 0)),
            pl.BlockSpec((1, D), lambda i: (0, 0)),
            pl.BlockSpec((D, E), lambda i: (0, 0)),
            pl.BlockSpec((1, ATTN_WIDTH), lambda i: (0, 0)),
            pl.BlockSpec((1, ATTN_WIDTH), lambda i: (0, 0)),
        ],
        out_specs=[head_spec, head_spec, head_spec,
                   pl.BlockSpec((tm, cw), lambda i: (i, 0))],
        out_shape=[head_shape, head_shape, head_shape,
                   jax.ShapeDtypeStruct((T, cw), F32)],
        compiler_params=pltpu.CompilerParams(
            dimension_semantics=("arbitrary",), vmem_limit_bytes=VMEM_LIMIT),
        name="inproj",
    )(x2, g, w_bf, gq_t, gk_t)


def _attn_span(q, k, v, offset, carry, suffix):
    tw = suffix.shape[0]
    z = _dot_nt(q, k)
    sp = jnp.maximum(z, 0.0) + jnp.log(1.0 + jnp.exp(-jnp.abs(z)))
    col_minus_row = (lax.broadcasted_iota(I32, z.shape, 1)
                     - lax.broadcasted_iota(I32, z.shape, 0))
    mask = col_minus_row < offset
    log_keep = jnp.where(mask, -sp, 0.0)
    later = []
    for s in reversed(range(z.shape[1] // tw)):
        lk = log_keep[:, s * tw:(s + 1) * tw]
        hi, lo = _split_bf16(lk)
        later.append(carry + (_dot(hi, suffix) + _dot(lo, suffix)))
        carry = carry + jnp.sum(lk, axis=-1, keepdims=True)
    later = jnp.concatenate(later[::-1], axis=-1)
    att = jnp.where(mask, jnp.exp(z - sp + later), 0.0)
    return carry, _dot(att.astype(BF16), v)


def _attn_kernel(q_ref, k_ref, v_ref, o_ref, carry_ref, acc_ref, *, rows, span, tw):
    g = pl.program_id(2)
    hp = q_ref.shape[0]
    suffix = (lax.broadcasted_iota(I32, (tw, tw), 0)
              > lax.broadcasted_iota(I32, (tw, tw), 1)).astype(BF16)

    start = pl.multiple_of(jnp.maximum(g * rows - (span - rows), 0), rows)
    cmax = None
    for hh in range(hp):
        carry, acc = _attn_span(q_ref[hh], k_ref[hh, pl.ds(start, span), :],
                                v_ref[hh, pl.ds(start, span), :], g * rows - start,
                                jnp.zeros((rows, 1), F32), suffix)
        carry_ref[hh] = carry
        acc_ref[hh] = acc
        cmax = carry if cmax is None else jnp.maximum(cmax, carry)

    @pl.when(jnp.max(cmax) > EXP_UNDERFLOW)
    def _():
        for hh in range(hp):

            def cond(st):
                j, carry, _ = st
                return jnp.logical_and(j >= 0, jnp.max(carry) > EXP_UNDERFLOW)

            def body(st, hh=hh):
                j, carry, acc = st
                ks = pl.multiple_of(j * tw, tw)
                carry, out = _attn_span(q_ref[hh], k_ref[hh, pl.ds(ks, tw), :],
                                        v_ref[hh, pl.ds(ks, tw), :], rows + tw, carry, suffix)
                return j - 1, carry, acc + out

            init = (start // tw - 1, carry_ref[hh], acc_ref[hh])
            acc_ref[hh] = lax.while_loop(cond, body, init)[2]

    o_ref[...] = jnp.concatenate([acc_ref[hh] for hh in range(hp)], axis=-1)


def _attention(qh, kh, vh, *, batch, seq, rows=256, span=512, tw=256, heads_per_step=2):
    H, T, hd = qh.shape
    ng = seq // rows
    hp = heads_per_step
    return pl.pallas_call(
        functools.partial(_attn_kernel, rows=rows, span=span, tw=tw),
        grid=(H // hp, batch, ng),
        in_specs=[
            pl.BlockSpec((hp, rows, hd), lambda h, b, i: (h, b * ng + i, 0)),
            pl.BlockSpec((hp, seq, hd), lambda h, b, i: (h, b, 0)),
            pl.BlockSpec((hp, seq, hd), lambda h, b, i: (h, b, 0)),
        ],
        out_specs=pl.BlockSpec((rows, hp * hd), lambda h, b, i: (b * ng + i, h)),
        out_shape=jax.ShapeDtypeStruct((T, H * hd), F32),
        scratch_shapes=[pltpu.VMEM((hp, rows, 1), F32),
                        pltpu.VMEM((hp, rows, hd), F32)],
        compiler_params=pltpu.CompilerParams(
            dimension_semantics=("arbitrary", "arbitrary", "arbitrary"),
            vmem_limit_bytes=VMEM_LIMIT),
        name="sb_attention",
    )(qh, kh, vh)


def _mix_kernel(x_ref, o_ref, hg_ref, hgp_ref, cw_ref, cb_ref, lg_ref, lb_ref, ag_ref,
                wo_ref, g2_ref, wqh_ref, wql_ref, h_ref, xn_ref, qh_ref, ext_ref,
                *, blocks_per_seq):
    tm = x_ref.shape[0]
    first = (pl.program_id(0) % blocks_per_seq) == 0
    ext_ref[0:CONV_HALO, :] = jnp.where(first, 0.0, hgp_ref[...])
    ext_ref[CONV_HALO:, :] = hg_ref[...]
    conv = jnp.zeros(hg_ref.shape, F32)
    for j in range(CONV_KERNEL):
        off = CONV_HALO - (CONV_KERNEL - 1) + j
        conv = conv + cw_ref[j:j + 1, :] * ext_ref[pl.ds(off, tm), :]
    conv = conv + cb_ref[...]
    mu = jnp.mean(conv, axis=-1, keepdims=True)
    xc = conv - mu
    var = jnp.mean(xc * xc, axis=-1, keepdims=True)
    y = xc * lax.rsqrt(var + EPS) * lg_ref[...] + lb_ref[...]
    o_conv = y * _sigmoid(y)

    o = o_ref[...]
    o_attn = o * lax.rsqrt(jnp.mean(o * o, axis=-1, keepdims=True) + EPS) * ag_ref[...]
    mixed = jnp.concatenate([o_attn, o_conv], axis=-1).astype(BF16)
    h = x_ref[...] + _dot(mixed, wo_ref[...])
    h_ref[...] = h

    xn = h * lax.rsqrt(jnp.mean(h * h, axis=-1, keepdims=True) + EPS) * g2_ref[...]
    xn_ref[...] = xn
    hi, lo = _split_bf16(xn)
    wqh = wqh_ref[...]
    qh_ref[...] = _dot(hi, wqh) + _dot(lo, wqh) + _dot(hi, wql_ref[...])


def _mix(x2, o, hg, conv_w, conv_b, ln_g, ln_b, attn_g, wo_bf, g2, wq_hi, wq_lo, *, seq, tm):
    T, D = x2.shape
    cwid = hg.shape[1]
    E = wq_hi.shape[1]
    halo_per_block = tm // CONV_HALO
    row = lambda i: (i, 0)
    const = lambda i: (0, 0)
    return pl.pallas_call(
        functools.partial(_mix_kernel, blocks_per_seq=seq // tm),
        grid=(T // tm,),
        in_specs=[
            pl.BlockSpec((tm, D), row),
            pl.BlockSpec((tm, o.shape[1]), row),
            pl.BlockSpec((tm, cwid), row),
            pl.BlockSpec((CONV_HALO, cwid),
                         lambda i: (jnp.maximum(i * halo_per_block - 1, 0), 0)),
            pl.BlockSpec((CONV_KERNEL, cwid), const),
            pl.BlockSpec((1, cwid), const),
            pl.BlockSpec((1, cwid), const),
            pl.BlockSpec((1, cwid), const),
            pl.BlockSpec((1, o.shape[1]), const),
            pl.BlockSpec(wo_bf.shape, const),
            pl.BlockSpec((1, D), const),
            pl.BlockSpec(wq_hi.shape, const),
            pl.BlockSpec(wq_lo.shape, const),
        ],
        out_specs=[pl.BlockSpec((tm, D), row), pl.BlockSpec((tm, D), row),
                   pl.BlockSpec((tm, E), row)],
        out_shape=[jax.ShapeDtypeStruct((T, D), F32), jax.ShapeDtypeStruct((T, D), F32),
                   jax.ShapeDtypeStruct((T, E), F32)],
        scratch_shapes=[pltpu.VMEM((tm + CONV_HALO, cwid), F32)],
        compiler_params=pltpu.CompilerParams(
            dimension_semantics=("arbitrary",), vmem_limit_bytes=VMEM_LIMIT),
        name="mix_outproj",
    )(x2, o, hg, hg, conv_w, conv_b, ln_g, ln_b, attn_g, wo_bf, g2, wq_hi, wq_lo)


def _top16(s, pos=None, payload=None):
    if pos is None:
        pos = lax.broadcasted_iota(I32, s.shape, 0).astype(F32)
    vals, outs = [], []
    for _ in range(PEER_TOPK):
        m = jnp.max(s, axis=0, keepdims=True)
        idx = jnp.min(jnp.where(s == m, pos, 1e9), axis=0, keepdims=True)
        sel = pos == idx
        if payload is None:
            outs.append(idx)
        else:
            outs.append(jnp.sum(jnp.where(sel, payload, 0.0), axis=0, keepdims=True))
        s = jnp.where(sel, -jnp.inf, s)
        vals.append(m)
    return jnp.concatenate(vals, axis=0), jnp.concatenate(outs, axis=0)


def _dot3_nt(a, b):
    ah, al = _split_bf16(a)
    bh, bl = _split_bf16(b)
    return _dot_nt(ah, bh) + _dot_nt(al, bh) + _dot_nt(ah, bl)


def _pair_candidates(v1, i1, v2, i2):
    r8 = lax.broadcasted_iota(I32, (8, v1.shape[1]), 0).astype(F32)
    sc, ex, ps = [], [], []

    def add(s1, e1, s2, e2, p, keep=None):
        s = s1 + s2
        sc.append(s if keep is None else jnp.where(keep, s, -jnp.inf))
        ex.append(e1 * float(N_KEYS) + e2)
        ps.append(p if keep is None else jnp.where(keep, p, 2e9))

    for a, half in ((0, 0), (0, 1), (1, 0), (2, 0), (3, 0)):
        b = slice(8 * half, 8 * half + 8)
        add(v1[a:a + 1], i1[a:a + 1], v2[b], i2[b], r8 + float(a * PEER_TOPK + 8 * half))
    for b in range(3):
        add(v1[0:8], i1[0:8], v2[b:b + 1], i2[b:b + 1], r8 * float(PEER_TOPK) + float(b),
            keep=r8 >= 4.0)
    add(v1[8:16], i1[8:16], v2[0:1], i2[0:1], (r8 + 8.0) * float(PEER_TOPK))
    return (jnp.concatenate(sc, axis=0), jnp.concatenate(ex, axis=0),
            jnp.concatenate(ps, axis=0))


def _topk_kernel(qh_ref, k1_ref, k2_ref, idx_ref, gate_ref):
    all_experts, all_gates = [], []
    for h in range(PEER_HEADS):
        base = h * 2 * PEER_HALF
        q1 = qh_ref[:, base:base + PEER_HALF]
        q2 = qh_ref[:, base + PEER_HALF:base + 2 * PEER_HALF]
        s1 = _dot3_nt(k1_ref[h], q1)
        s2 = _dot3_nt(k2_ref[h], q2)
        v1, i1 = _top16(s1)
        v2, i2 = _top16(s2)
        cand, cexp, cpos = _pair_candidates(v1, i1, v2, i2)
        top_s, experts = _top16(cand, pos=cpos, payload=cexp)
        e = jnp.exp(top_s - top_s[0:1, :])
        all_experts.append(experts)
        all_gates.append(e / jnp.sum(e, axis=0, keepdims=True))
    idx_ref[...] = jnp.concatenate(all_experts, axis=0).T.astype(I32) * CHUNKS
    gate_ref[...] = jnp.concatenate(all_gates, axis=0).T


def _topk(qh, k1, k2, *, tm):
    T, E = qh.shape
    out_spec = pl.BlockSpec((tm, SEL), lambda i: (i, 0))
    return pl.pallas_call(
        _topk_kernel,
        grid=(T // tm,),
        in_specs=[
            pl.BlockSpec((tm, E), lambda i: (i, 0)),
            pl.BlockSpec(k1.shape, lambda i: (0, 0, 0)),
            pl.BlockSpec(k2.shape, lambda i: (0, 0, 0)),
        ],
        out_specs=[out_spec, out_spec],
        out_shape=[jax.ShapeDtypeStruct((T, SEL), I32), jax.ShapeDtypeStruct((T, SEL), F32)],
        compiler_params=pltpu.CompilerParams(
            dimension_semantics=("arbitrary",), vmem_limit_bytes=VMEM_LIMIT),
        name="peer_topk",
    )(qh, k1, k2)


CHUNKS = 4
CHUNK_STRIDE = SEL + 8


def _unpack(word):
    lo = lax.bitcast_convert_type(word << 16, F32)
    hi = lax.bitcast_convert_type(word & HI_MASK, F32)
    return lo, hi


def _gather_group(idx_ref, tab_ref, buf_ref, t0, slots, part=0, parts=1):
    ids = [idx_ref.at[0, 0, pl.ds((t0 + j) * SEL, SEL)] for j in range(len(slots))]
    for k in range(part * SEL // parts, (part + 1) * SEL // parts):
        for j, s in enumerate(slots):
            r = pl.multiple_of(ids[j][k], CHUNKS)
            buf_ref[s, pl.ds(k, CHUNKS, stride=CHUNK_STRIDE), :] = tab_ref[pl.ds(r, CHUNKS), :]


def _cols(c):
    return slice(c * 128, (c + 1) * 128)


def _chunk(buf_ref, s, c):
    return _unpack(buf_ref[s, c * CHUNK_STRIDE:c * CHUNK_STRIDE + SEL, :])


def _lane_replicated_sum(a):
    ones = jnp.ones((128, 128), BF16)
    hi, lo = _split_bf16(a)
    return _dot(hi, ones) + _dot(lo, ones)


def _diag_mask():
    return (lax.broadcasted_iota(I32, (SEL, 128), 0) == lax.broadcasted_iota(I32, (SEL, 128), 1))


GROUP = 4


def _grouped_tokens(tb, gather, reduce):
    half = (tuple(range(GROUP)), tuple(range(GROUP, 2 * GROUP)))
    gather(0, half[0], 0, 1)

    def trip(i, carry):
        t = 2 * GROUP * i
        for j in range(GROUP):
            gather(t + GROUP, half[1], j, GROUP)
            reduce(t + j, half[0][j])
        nxt = jnp.minimum(t + 2 * GROUP, tb - GROUP)
        for j in range(GROUP):
            gather(nxt, half[0], j, GROUP)
            reduce(t + GROUP + j, half[1][j])
        return carry

    lax.fori_loop(0, tb // (2 * GROUP), trip, 0)


def _peer_u_kernel(idx_ref, x_ref, gate_ref, tab_ref, w_ref, buf_ref, act_ref):
    tb = x_ref.shape[0]
    diag = _diag_mask()

    def gather(t0, slots, part, parts):
        _gather_group(idx_ref, tab_ref, buf_ref, t0, slots, part, parts)

    def reduce(t, slot):
        total = None
        for c in range(CHUNKS):
            lo, hi = _chunk(buf_ref, slot, c)
            term = lo * x_ref[t, c:c + 1, :] + hi * x_ref[t, c + CHUNKS:c + CHUNKS + 1, :]
            total = term if total is None else total + term
        act_rep = _lane_replicated_sum(total)
        act_ref[pl.ds(t, 1), :] = jnp.sum(jnp.where(diag, act_rep, 0.0), axis=0, keepdims=True)

    _grouped_tokens(tb, gather, reduce)
    act = act_ref[...]
    gelu = 0.5 * act * (1.0 + lax.erf(act * (2.0 ** -0.5)))
    w_ref[...] = gate_ref[...] * gelu


def _gather_scratch():
    return pltpu.VMEM((2 * GROUP, CHUNKS * CHUNK_STRIDE, 128), I32)


def _index_spec(tb):
    return pl.BlockSpec((1, 1, tb * SEL), lambda i: (i, 0, 0), memory_space=pltpu.SMEM)


def _peer_u(idx3, x3, gates, tab, *, tb, tokens):
    return pl.pallas_call(
        _peer_u_kernel,
        grid=(tokens // tb,),
        in_specs=[
            _index_spec(tb),
            pl.BlockSpec((tb, 2 * CHUNKS, 128), lambda i: (i, 0, 0)),
            pl.BlockSpec((tb, SEL), lambda i: (i, 0)),
            pl.BlockSpec(memory_space=pltpu.VMEM),
        ],
        out_specs=pl.BlockSpec((tb, SEL), lambda i: (i, 0)),
        out_shape=jax.ShapeDtypeStruct((tokens, SEL), F32),
        scratch_shapes=[_gather_scratch(), pltpu.VMEM((tb, SEL), F32)],
        compiler_params=pltpu.CompilerParams(
            dimension_semantics=("arbitrary",), vmem_limit_bytes=VMEM_LIMIT),
        name="peer_u",
    )(idx3, x3, gates, tab)


def _peer_v_kernel(idx_ref, w_ref, h_ref, tab_ref, y_ref, buf_ref):
    tb = h_ref.shape[0]
    diag = _diag_mask()

    def gather(t0, slots, part, parts):
        _gather_group(idx_ref, tab_ref, buf_ref, t0, slots, part, parts)

    def reduce(t, slot):
        w_rep = _lane_replicated_sum(jnp.where(diag, w_ref[pl.ds(t, 1), :], 0.0))
        lows, highs = [], []
        for c in range(CHUNKS):
            lo, hi = _chunk(buf_ref, slot, c)
            lows.append(jnp.sum(lo * w_rep, axis=0, keepdims=True))
            highs.append(jnp.sum(hi * w_rep, axis=0, keepdims=True))
        y_ref[pl.ds(t, 1), :] = h_ref[pl.ds(t, 1), :] + jnp.concatenate(lows + highs, axis=-1)

    _grouped_tokens(tb, gather, reduce)


def _peer_v(idx3, w, h2, tab, *, tb, tokens):
    D = h2.shape[1]
    return pl.pallas_call(
        _peer_v_kernel,
        grid=(tokens // tb,),
        in_specs=[
            _index_spec(tb),
            pl.BlockSpec((tb, SEL), lambda i: (i, 0)),
            pl.BlockSpec((tb, D), lambda i: (i, 0)),
            pl.BlockSpec(memory_space=pltpu.VMEM),
        ],
        out_specs=pl.BlockSpec((tb, D), lambda i: (i, 0)),
        out_shape=jax.ShapeDtypeStruct((tokens, D), F32),
        scratch_shapes=[_gather_scratch()],
        compiler_params=pltpu.CompilerParams(
            dimension_semantics=("arbitrary",), vmem_limit_bytes=VMEM_LIMIT),
        name="peer_v",
    )(idx3, w, h2, tab)


SC_CORES = 2
SC_WORKERS = 32
SC_LANES = 16
ROW_WORDS = 512
HALF_WORDS = ROW_WORDS // 2
HALF_VECS = HALF_WORDS // SC_LANES


def _sc_lanes(j):
    return pl.ds(j * SC_LANES, SC_LANES)


def _sc_token_pipeline(ids_hbm, tab_hbm, ids_v, rows_v, sems, *, first, per, load_extra, process):
    base = (lax.axis_index("s") * SC_CORES + lax.axis_index("c")) * per

    def gather(slot, half):
        return pltpu.make_async_copy(tab_hbm.at[ids_v.at[slot, half]], rows_v.at[half],
                                     sems.at[half])

    def load_token(i, slot):
        pltpu.sync_copy(ids_hbm.at[first + base + i], ids_v.at[slot])
        load_extra(base + i, slot)

    load_token(0, 0)
    gather(0, 0).start()

    @pl.loop(0, per)
    def _(i):
        slot = i % 2
        gather(slot, 1).start()
        gather(slot, 0).wait()
        process(base + i, slot, 0)

        @pl.when(i + 1 < per)
        def _():
            load_token(i + 1, 1 - slot)
            gather(1 - slot, 0).start()

        gather(slot, 1).wait()
        process(base + i, slot, 1)


def _peer_u_sc_kernel(ids_hbm, x_hbm, tab_hbm, act_hbm, ids_v, x_v, part_v, act_v, rows_v, sems,
                      *, first, per):
    def load_extra(local, slot):
        pltpu.sync_copy(x_hbm.at[first + local], x_v.at[slot])

    def process(local, slot, half):
        off = half * HALF_VECS
        x_lo = [x_v[slot, _sc_lanes(off + j)] for j in range(HALF_VECS)]
        x_hi = [x_v[slot, _sc_lanes(2 * HALF_VECS + off + j)] for j in range(HALF_VECS)]

        @pl.loop(0, SEL)
        def _(k):
            sums = [None] * 4
            for j in range(HALF_VECS):
                lo, hi = _unpack(rows_v[half, k, _sc_lanes(j)])
                term = lo * x_lo[j] + hi * x_hi[j]
                sums[j % 4] = term if sums[j % 4] is None else sums[j % 4] + term
            part_v[half * SEL + k, :] = (sums[0] + sums[1]) + (sums[2] + sums[3])

        if half == 1:
            lane_ids = lax.broadcasted_iota(I32, (SC_LANES,), 0)
            for kb in range(SEL // SC_LANES):
                rows0 = lane_ids + kb * SC_LANES
                total = None
                for l in range(SC_LANES):
                    col = jnp.full((SC_LANES,), l, I32)
                    both = (plsc.load_gather(part_v, [rows0, col])
                            + plsc.load_gather(part_v, [rows0 + SEL, col]))
                    total = both if total is None else total + both
                act_v[_sc_lanes(kb)] = total
            pltpu.sync_copy(act_v, act_hbm.at[local])

    _sc_token_pipeline(ids_hbm, tab_hbm, ids_v, rows_v, sems, first=first, per=per,
                       load_extra=load_extra, process=process)


def _sc_call(body, out_cols, n, scratch, name):
    return pl.kernel(
        body,
        out_type=jax.ShapeDtypeStruct((n, out_cols), F32),
        mesh=plsc.VectorSubcoreMesh(core_axis_name="c", subcore_axis_name="s"),
        scratch_types=scratch + [pltpu.VMEM((2, SEL, HALF_WORDS), I32),
                                 pltpu.SemaphoreType.DMA((2,))],
        compiler_params=pltpu.CompilerParams(needs_layout_passes=False),
        name=name,
    )


def _peer_u_sc(half_ids, x2, tab_halves, *, first):
    T, D = x2.shape
    n = T - first
    body = functools.partial(_peer_u_sc_kernel, first=first, per=n // SC_WORKERS)
    scratch = [pltpu.VMEM((2, 2, SEL), I32), pltpu.VMEM((2, D), F32),
               pltpu.VMEM((2 * SEL, SC_LANES), F32), pltpu.VMEM((SEL,), F32)]
    return _sc_call(body, SEL, n, scratch, "peer_u_sc")(half_ids, x2, tab_halves)


def _peer_v_sc_kernel(ids_hbm, w_hbm, h_hbm, tab_hbm, y_hbm, ids_v, w_v, h_v, y_v, rows_v, sems,
                      *, first, per):
    def load_extra(local, slot):
        pltpu.sync_copy(w_hbm.at[local], w_v.at[slot])
        pltpu.sync_copy(h_hbm.at[first + local], h_v.at[slot])

    def process(local, slot, half):
        off = half * HALF_VECS
        init = (tuple(h_v[slot, _sc_lanes(off + j)] for j in range(HALF_VECS))
                + tuple(h_v[slot, _sc_lanes(2 * HALF_VECS + off + j)] for j in range(HALF_VECS)))
        slot_vec = jnp.full((SC_LANES,), slot, I32)

        def body(k, acc):
            wk = plsc.load_gather(w_v, [slot_vec, jnp.full((SC_LANES,), k, I32)])
            lows, highs = [], []
            for j in range(HALF_VECS):
                lo, hi = _unpack(rows_v[half, k, _sc_lanes(j)])
                lows.append(acc[j] + wk * lo)
                highs.append(acc[HALF_VECS + j] + wk * hi)
            return tuple(lows + highs)

        acc = lax.fori_loop(0, SEL, body, init)
        for j in range(HALF_VECS):
            y_v[_sc_lanes(off + j)] = acc[j]
            y_v[_sc_lanes(2 * HALF_VECS + off + j)] = acc[HALF_VECS + j]
        if half == 1:
            pltpu.sync_copy(y_v, y_hbm.at[local])

    _sc_token_pipeline(ids_hbm, tab_hbm, ids_v, rows_v, sems, first=first, per=per,
                       load_extra=load_extra, process=process)


def _peer_v_sc(half_ids, w_tail, h2, tab_halves, *, first):
    T, D = h2.shape
    n = T - first
    body = functools.partial(_peer_v_sc_kernel, first=first, per=n // SC_WORKERS)
    scratch = [pltpu.VMEM((2, 2, SEL), I32), pltpu.VMEM((2, SEL), F32),
               pltpu.VMEM((2, D), F32), pltpu.VMEM((D,), F32)]
    return _sc_call(body, D, n, scratch, "peer_v_sc")(half_ids, w_tail, h2, tab_halves)


def _gelu_gate_kernel(act_ref, gate_ref, w_ref):
    act = act_ref[...]
    w_ref[...] = gate_ref[...] * (0.5 * act * (1.0 + lax.erf(act * (2.0 ** -0.5))))


def _gelu_gate(act_tail, gates, *, first, tm):
    n = act_tail.shape[0]
    return pl.pallas_call(
        _gelu_gate_kernel,
        grid=(n // tm,),
        in_specs=[pl.BlockSpec((tm, SEL), lambda i: (i, 0)),
                  pl.BlockSpec((tm, SEL), lambda i: (i + first // tm, 0))],
        out_specs=pl.BlockSpec((tm, SEL), lambda i: (i, 0)),
        out_shape=jax.ShapeDtypeStruct((n, SEL), F32),
        name="peer_gelu_gate",
    )(act_tail, gates)


def _pack_table(tab):
    n, d = tab.shape
    bits = lax.bitcast_convert_type(tab.astype(BF16), jnp.uint16).astype(jnp.uint32)
    word = bits[:, :d // 2] | (bits[:, d // 2:] << 16)
    return lax.bitcast_convert_type(word, I32).reshape(n * 4, 128)


def _layer(x2, p, *, batch, seq, tm_in, tm_mix, tm_topk, tb, sc_tokens):
    T, D = x2.shape
    row = lambda a: a.reshape(1, -1)
    w_in_bf = p["w_in"].astype(BF16)
    gq_t = row(jnp.tile(p["q_norm_g"], ATTN_HEADS))
    gk_t = row(jnp.tile(p["k_norm_g"], ATTN_HEADS))
    qh, kh, vh, hg = _inproj(x2, row(p["norm_mix_g"]), w_in_bf, gq_t, gk_t, tm=tm_in)
    o = _attention(qh, kh, vh, batch=batch, seq=seq)

    wq = p["peer_wq"]
    wq_hi = wq.astype(BF16)
    wq_lo = (wq - wq_hi.astype(F32)).astype(BF16)
    h, xn, pq = _mix(x2, o, hg, p["conv_w"], row(p["conv_b"]), row(p["conv_ln_g"]),
                     row(p["conv_ln_b"]), row(p["attn_out_g"]), p["w_out"].astype(BF16),
                     row(p["norm_ffn_g"]), wq_hi, wq_lo, seq=seq, tm=tm_mix)

    idx, gates = _topk(pq, p["peer_k1"], p["peer_k2"], tm=tm_topk)
    u_tab = _pack_table(p["peer_u"])
    v_tab = _pack_table(p["peer_v"])
    head = T - sc_tokens
    half_row = idx // (CHUNKS // 2)
    half_ids = jnp.stack([half_row, half_row + 1], axis=1)
    idx3 = (half_ids[:, 0, :] * (CHUNKS // 2)).reshape(T // tb, 1, tb * SEL)
    act_tail = _peer_u_sc(half_ids, xn, u_tab.reshape(-1, HALF_WORDS), first=head)
    w_head = _peer_u(idx3, xn.reshape(T, 2 * CHUNKS, 128), gates, u_tab, tb=tb, tokens=head)
    w_tail = _gelu_gate(act_tail, gates, first=head, tm=tm_topk)
    y_tail = _peer_v_sc(half_ids, w_tail, h, v_tab.reshape(-1, HALF_WORDS), first=head)
    y_head = _peer_v(idx3, w_head, h, v_tab, tb=tb, tokens=head)
    return jnp.concatenate([y_head, y_tail], axis=0)


def kernel(x, norm_mix_g, w_in, q_norm_g, k_norm_g, attn_out_g, conv_w, conv_b, conv_ln_g,
           conv_ln_b, w_out, norm_ffn_g, peer_wq, peer_k1, peer_k2, peer_u, peer_v):
    batch, seq, d = x.shape
    stacked = dict(norm_mix_g=norm_mix_g, w_in=w_in, q_norm_g=q_norm_g, k_norm_g=k_norm_g,
                   attn_out_g=attn_out_g, conv_w=conv_w, conv_b=conv_b, conv_ln_g=conv_ln_g,
                   conv_ln_b=conv_ln_b, w_out=w_out, norm_ffn_g=norm_ffn_g, peer_wq=peer_wq,
                   peer_k1=peer_k1, peer_k2=peer_k2, peer_u=peer_u, peer_v=peer_v)
    h = x.reshape(batch * seq, d)
    for l in range(w_in.shape[0]):
        p = {name: a[l] for name, a in stacked.items()}
        h = _layer(h, p, batch=batch, seq=seq, tm_in=256, tm_mix=256,
                   tm_topk=256, tb=64, sc_tokens=12288)
    return h.reshape(batch, seq, d)
```

```python
import functools

import jax
import jax.numpy as jnp
from jax import lax
from jax.experimental import pallas as pl
from jax.experimental.pallas import tpu as pltpu
from jax.experimental.pallas import tpu_sc as plsc

F32 = jnp.float32
BF16 = jnp.bfloat16
I32 = jnp.int32

EPS = 1e-6
HEAD_DIM = 64
ATTN_HEADS = 8
ATTN_WIDTH = ATTN_HEADS * HEAD_DIM
CONV_KERNEL = 31
CONV_HALO = 32
PEER_HEADS = 8
PEER_HALF = 64
N_KEYS = 128
PEER_TOPK = 16
SEL = PEER_HEADS * PEER_TOPK
EXP_UNDERFLOW = -88.0
HI_MASK = -65536

VMEM_LIMIT = 56 * 1024 * 1024


def _split_bf16(a):
    hi = a.astype(BF16)
    lo = (a - hi.astype(F32)).astype(BF16)
    return hi, lo


def _dot(a, b):
    return jnp.dot(a, b, preferred_element_type=F32)


def _dot_nt(a, b):
    return lax.dot_general(a, b, (((1,), (1,)), ((), ())), preferred_element_type=F32)


def _sigmoid(x):
    return 1.0 / (1.0 + jnp.exp(-x))


def _inproj_kernel(x_ref, g_ref, w_ref, gq_ref, gk_ref, q_ref, k_ref, v_ref, hg_ref):
    x = x_ref[...]
    ms = jnp.mean(x * x, axis=-1, keepdims=True)
    xn = (x * lax.rsqrt(ms + EPS) * g_ref[...]).astype(BF16)
    proj = _dot(xn, w_ref[...])

    r = lax.broadcasted_iota(I32, (ATTN_WIDTH, ATTN_WIDTH), 0) // HEAD_DIM
    c = lax.broadcasted_iota(I32, (ATTN_WIDTH, ATTN_WIDTH), 1) // HEAD_DIM
    same_head = (r == c).astype(BF16)

    def head_norm(t, g):
        hi, lo = _split_bf16(t * t)
        msq = (_dot(hi, same_head) + _dot(lo, same_head)) * (1.0 / HEAD_DIM)
        return t * lax.rsqrt(msq + EPS) * g

    q = head_norm(proj[:, :ATTN_WIDTH], gq_ref[...]) * (HEAD_DIM ** -0.5)
    k = head_norm(proj[:, ATTN_WIDTH:2 * ATTN_WIDTH], gk_ref[...])
    v = proj[:, 2 * ATTN_WIDTH:3 * ATTN_WIDTH]
    for h in range(ATTN_HEADS):
        sl = slice(h * HEAD_DIM, (h + 1) * HEAD_DIM)
        q_ref[h] = q[:, sl].astype(BF16)
        k_ref[h] = k[:, sl].astype(BF16)
        v_ref[h] = v[:, sl].astype(BF16)
    cw = (proj.shape[1] - 3 * ATTN_WIDTH) // 2
    a = proj[:, 3 * ATTN_WIDTH:3 * ATTN_WIDTH + cw]
    gate = proj[:, 3 * ATTN_WIDTH + cw:]
    hg_ref[...] = a * _sigmoid(gate)


def _inproj(x2, g, w_bf, gq_t, gk_t, *, tm):
    T, D = x2.shape
    E = w_bf.shape[1]
    cw = (E - 3 * ATTN_WIDTH) // 2
    head_shape = jax.ShapeDtypeStruct((ATTN_HEADS, T, HEAD_DIM), BF16)
    head_spec = pl.BlockSpec((ATTN_HEADS, tm, HEAD_DIM), lambda i: (0, i, 0))
    return pl.pallas_call(
        _inproj_kernel,
        grid=(T // tm,),
        in_specs=[
            pl.BlockSpec((tm, D), lambda i: (i, 0)),
            pl.BlockSpec((1, D), lambda i: (0, 0)),
            pl.BlockSpec((D, E), lambda i: (0, 0)),
            pl.BlockSpec((1, ATTN_WIDTH), lambda i: (0, 0)),
            pl.BlockSpec((1, ATTN_WIDTH), lambda i: (0, 0)),
        ],
        out_specs=[head_spec, head_spec, head_spec,
                   pl.BlockSpec((tm, cw), lambda i: (i, 0))],
        out_shape=[head_shape, head_shape, head_shape,
                   jax.ShapeDtypeStruct((T, cw), F32)],
        compiler_params=pltpu.CompilerParams(
            dimension_semantics=("arbitrary",), vmem_limit_bytes=VMEM_LIMIT),
        name="inproj",
    )(x2, g, w_bf, gq_t, gk_t)


def _attn_span(q, k, v, offset, carry, suffix):
    tw = suffix.shape[0]
    z = _dot_nt(q, k)
    sp = jnp.maximum(z, 0.0) + jnp.log(1.0 + jnp.exp(-jnp.abs(z)))
    col_minus_row = (lax.broadcasted_iota(I32, z.shape, 1)
                     - lax.broadcasted_iota(I32, z.shape, 0))
    mask = col_minus_row < offset
    log_keep = jnp.where(mask, -sp, 0.0)
    later = []
    for s in reversed(range(z.shape[1] // tw)):
        lk = log_keep[:, s * tw:(s + 1) * tw]
        hi, lo = _split_bf16(lk)
        later.append(carry + (_dot(hi, suffix) + _dot(lo, suffix)))
        carry = carry + jnp.sum(lk, axis=-1, keepdims=True)
    later = jnp.concatenate(later[::-1], axis=-1)
    att = jnp.where(mask, jnp.exp(z - sp + later), 0.0)
    return carry, _dot(att.astype(BF16), v)


def _attn_kernel(q_ref, k_ref, v_ref, o_ref, carry_ref, acc_ref, *, rows, span, tw):
    g = pl.program_id(2)
    hp = q_ref.shape[0]
    suffix = (lax.broadcasted_iota(I32, (tw, tw), 0)
              > lax.broadcasted_iota(I32, (tw, tw), 1)).astype(BF16)

    start = pl.multiple_of(jnp.maximum(g * rows - (span - rows), 0), rows)
    cmax = None
    for hh in range(hp):
        carry, acc = _attn_span(q_ref[hh], k_ref[hh, pl.ds(start, span), :],
                                v_ref[hh, pl.ds(start, span), :], g * rows - start,
                                jnp.zeros((rows, 1), F32), suffix)
        carry_ref[hh] = carry
        acc_ref[hh] = acc
        cmax = carry if cmax is None else jnp.maximum(cmax, carry)

    @pl.when(jnp.max(cmax) > EXP_UNDERFLOW)
    def _():
        for hh in range(hp):

            def cond(st):
                j, carry, _ = st
                return jnp.logical_and(j >= 0, jnp.max(carry) > EXP_UNDERFLOW)

            def body(st, hh=hh):
                j, carry, acc = st
                ks = pl.multiple_of(j * tw, tw)
                carry, out = _attn_span(q_ref[hh], k_ref[hh, pl.ds(ks, tw), :],
                                        v_ref[hh, pl.ds(ks, tw), :], rows + tw, carry, suffix)
                return j - 1, carry, acc + out

            init = (start // tw - 1, carry_ref[hh], acc_ref[hh])
            acc_ref[hh] = lax.while_loop(cond, body, init)[2]

    o_ref[...] = jnp.concatenate([acc_ref[hh] for hh in range(hp)], axis=-1)


def _attention(qh, kh, vh, *, batch, seq, rows=256, span=512, tw=256, heads_per_step=2):
    H, T, hd = qh.shape
    ng = seq // rows
    hp = heads_per_step
    return pl.pallas_call(
        functools.partial(_attn_kernel, rows=rows, span=span, tw=tw),
        grid=(H // hp, batch, ng),
        in_specs=[
            pl.BlockSpec((hp, rows, hd), lambda h, b, i: (h, b * ng + i, 0)),
            pl.BlockSpec((hp, seq, hd), lambda h, b, i: (h, b, 0)),
            pl.BlockSpec((hp, seq, hd), lambda h, b, i: (h, b, 0)),
        ],
        out_specs=pl.BlockSpec((rows, hp * hd), lambda h, b, i: (b * ng + i, h)),
        out_shape=jax.ShapeDtypeStruct((T, H * hd), F32),
        scratch_shapes=[pltpu.VMEM((hp, rows, 1), F32),
                        pltpu.VMEM((hp, rows, hd), F32)],
        compiler_params=pltpu.CompilerParams(
            dimension_semantics=("arbitrary", "arbitrary", "arbitrary"),
            vmem_limit_bytes=VMEM_LIMIT),
        name="sb_attention",
    )(qh, kh, vh)


def _mix_kernel(x_ref, o_ref, hg_ref, hgp_ref, cw_ref, cb_ref, lg_ref, lb_ref, ag_ref,
                wo_ref, g2_ref, wqh_ref, wql_ref, h_ref, xn_ref, xn3_ref, qh_ref, ext_ref,
                *, blocks_per_seq):
    tm = x_ref.shape[0]
    first = (pl.program_id(0) % blocks_per_seq) == 0
    ext_ref[0:CONV_HALO, :] = jnp.where(first, 0.0, hgp_ref[...])
    ext_ref[CONV_HALO:, :] = hg_ref[...]
    conv = jnp.zeros(hg_ref.shape, F32)
    for j in range(CONV_KERNEL):
        off = CONV_HALO - (CONV_KERNEL - 1) + j
        conv = conv + cw_ref[j:j + 1, :] * ext_ref[pl.ds(off, tm), :]
    conv = conv + cb_ref[...]
    mu = jnp.mean(conv, axis=-1, keepdims=True)
    xc = conv - mu
    var = jnp.mean(xc * xc, axis=-1, keepdims=True)
    y = xc * lax.rsqrt(var + EPS) * lg_ref[...] + lb_ref[...]
    o_conv = y * _sigmoid(y)

    o = o_ref[...]
    o_attn = o * lax.rsqrt(jnp.mean(o * o, axis=-1, keepdims=True) + EPS) * ag_ref[...]
    mixed = jnp.concatenate([o_attn, o_conv], axis=-1).astype(BF16)
    h = x_ref[...] + _dot(mixed, wo_ref[...])
    h_ref[...] = h

    xn = h * lax.rsqrt(jnp.mean(h * h, axis=-1, keepdims=True) + EPS) * g2_ref[...]
    xn_ref[...] = xn
    for c in range(xn3_ref.shape[1]):
        xn3_ref[:, c, :] = xn[:, c * 128:(c + 1) * 128]
    hi, lo = _split_bf16(xn)
    wqh = wqh_ref[...]
    qh_ref[...] = _dot(hi, wqh) + _dot(lo, wqh) + _dot(hi, wql_ref[...])


def _mix(x2, o, hg, conv_w, conv_b, ln_g, ln_b, attn_g, wo_bf, g2, wq_hi, wq_lo, *, seq, tm):
    T, D = x2.shape
    cwid = hg.shape[1]
    E = wq_hi.shape[1]
    halo_per_block = tm // CONV_HALO
    row = lambda i: (i, 0)
    const = lambda i: (0, 0)
    return pl.pallas_call(
        functools.partial(_mix_kernel, blocks_per_seq=seq // tm),
        grid=(T // tm,),
        in_specs=[
            pl.BlockSpec((tm, D), row),
            pl.BlockSpec((tm, o.shape[1]), row),
            pl.BlockSpec((tm, cwid), row),
            pl.BlockSpec((CONV_HALO, cwid),
                         lambda i: (jnp.maximum(i * halo_per_block - 1, 0), 0)),
            pl.BlockSpec((CONV_KERNEL, cwid), const),
            pl.BlockSpec((1, cwid), const),
            pl.BlockSpec((1, cwid), const),
            pl.BlockSpec((1, cwid), const),
            pl.BlockSpec((1, o.shape[1]), const),
            pl.BlockSpec(wo_bf.shape, const),
            pl.BlockSpec((1, D), const),
            pl.BlockSpec(wq_hi.shape, const),
            pl.BlockSpec(wq_lo.shape, const),
        ],
        out_specs=[pl.BlockSpec((tm, D), row), pl.BlockSpec((tm, D), row),
                   pl.BlockSpec((tm, D // 128, 128), lambda i: (i, 0, 0)),
                   pl.BlockSpec((tm, E), row)],
        out_shape=[jax.ShapeDtypeStruct((T, D), F32), jax.ShapeDtypeStruct((T, D), F32),
                   jax.ShapeDtypeStruct((T, D // 128, 128), F32),
                   jax.ShapeDtypeStruct((T, E), F32)],
        scratch_shapes=[pltpu.VMEM((tm + CONV_HALO, cwid), F32)],
        compiler_params=pltpu.CompilerParams(
            dimension_semantics=("arbitrary",), vmem_limit_bytes=VMEM_LIMIT),
        name="mix_outproj",
    )(x2, o, hg, hg, conv_w, conv_b, ln_g, ln_b, attn_g, wo_bf, g2, wq_hi, wq_lo)


def _top16(s, pos=None, payload=None):
    if pos is None:
        pos = lax.broadcasted_iota(I32, s.shape, 0).astype(F32)
    vals, outs = [], []
    for _ in range(PEER_TOPK):
        m = jnp.max(s, axis=0, keepdims=True)
        idx = jnp.min(jnp.where(s == m, pos, 1e9), axis=0, keepdims=True)
        sel = pos == idx
        if payload is None:
            outs.append(idx)
        else:
            outs.append(jnp.sum(jnp.where(sel, payload, 0.0), axis=0, keepdims=True))
        s = jnp.where(sel, -jnp.inf, s)
        vals.append(m)
    return jnp.concatenate(vals, axis=0), jnp.concatenate(outs, axis=0)


def _dot3_nt(a, b):
    ah, al = _split_bf16(a)
    bh, bl = _split_bf16(b)
    return _dot_nt(ah, bh) + _dot_nt(al, bh) + _dot_nt(ah, bl)


def _pair_candidates(v1, i1, v2, i2):
    r8 = lax.broadcasted_iota(I32, (8, v1.shape[1]), 0).astype(F32)
    sc, ex, ps = [], [], []

    def add(s1, e1, s2, e2, p, keep=None):
        s = s1 + s2
        sc.append(s if keep is None else jnp.where(keep, s, -jnp.inf))
        ex.append(e1 * float(N_KEYS) + e2)
        ps.append(p if keep is None else jnp.where(keep, p, 2e9))

    for a, half in ((0, 0), (0, 1), (1, 0), (2, 0), (3, 0)):
        b = slice(8 * half, 8 * half + 8)
        add(v1[a:a + 1], i1[a:a + 1], v2[b], i2[b], r8 + float(a * PEER_TOPK + 8 * half))
    for b in range(3):
        add(v1[0:8], i1[0:8], v2[b:b + 1], i2[b:b + 1], r8 * float(PEER_TOPK) + float(b),
            keep=r8 >= 4.0)
    add(v1[8:16], i1[8:16], v2[0:1], i2[0:1], (r8 + 8.0) * float(PEER_TOPK))
    return (jnp.concatenate(sc, axis=0), jnp.concatenate(ex, axis=0),
            jnp.concatenate(ps, axis=0))


def _topk_kernel(qh_ref, k1_ref, k2_ref, idx_ref, gate_ref):
    all_experts, all_gates = [], []
    for h in range(PEER_HEADS):
        base = h * 2 * PEER_HALF
        q1 = qh_ref[:, base:base + PEER_HALF]
        q2 = qh_ref[:, base + PEER_HALF:base + 2 * PEER_HALF]
        s1 = _dot3_nt(k1_ref[h], q1)
        s2 = _dot3_nt(k2_ref[h], q2)
        v1, i1 = _top16(s1)
        v2, i2 = _top16(s2)
        cand, cexp, cpos = _pair_candidates(v1, i1, v2, i2)
        top_s, experts = _top16(cand, pos=cpos, payload=cexp)
        e = jnp.exp(top_s - top_s[0:1, :])
        all_experts.append(experts)
        all_gates.append(e / jnp.sum(e, axis=0, keepdims=True))
    idx_ref[...] = jnp.concatenate(all_experts, axis=0).T.astype(I32) * CHUNKS
    gate_ref[...] = jnp.concatenate(all_gates, axis=0).T


def _topk(qh, k1, k2, *, tm):
    T, E = qh.shape
    out_spec = pl.BlockSpec((tm, SEL), lambda i: (i, 0))
    return pl.pallas_call(
        _topk_kernel,
        grid=(T // tm,),
        in_specs=[
            pl.BlockSpec((tm, E), lambda i: (i, 0)),
            pl.BlockSpec(k1.shape, lambda i: (0, 0, 0)),
            pl.BlockSpec(k2.shape, lambda i: (0, 0, 0)),
        ],
        out_specs=[out_spec, out_spec],
        out_shape=[jax.ShapeDtypeStruct((T, SEL), I32), jax.ShapeDtypeStruct((T, SEL), F32)],
        compiler_params=pltpu.CompilerParams(
            dimension_semantics=("arbitrary",), vmem_limit_bytes=VMEM_LIMIT),
        name="peer_topk",
    )(qh, k1, k2)


CHUNKS = 4
CHUNK_STRIDE = SEL + 8


def _unpack(word):
    lo = lax.bitcast_convert_type(word << 16, F32)
    hi = lax.bitcast_convert_type(word & HI_MASK, F32)
    return lo, hi


def _gather_group(idx_ref, tab_ref, buf_ref, t0, slots, part=0, parts=1):
    ids = [idx_ref.at[0, 0, pl.ds((t0 + j) * SEL, SEL)] for j in range(len(slots))]
    for k in range(part * SEL // parts, (part + 1) * SEL // parts):
        for j, s in enumerate(slots):
            r = pl.multiple_of(ids[j][k], CHUNKS)
            buf_ref[s, pl.ds(k, CHUNKS, stride=CHUNK_STRIDE), :] = tab_ref[pl.ds(r, CHUNKS), :]


def _cols(c):
    return slice(c * 128, (c + 1) * 128)


def _chunk(buf_ref, s, c):
    return _unpack(buf_ref[s, c * CHUNK_STRIDE:c * CHUNK_STRIDE + SEL, :])


def _lane_replicated_sum(a):
    ones = jnp.ones((128, 128), BF16)
    hi, lo = _split_bf16(a)
    return _dot(hi, ones) + _dot(lo, ones)


def _diag_mask():
    return (lax.broadcasted_iota(I32, (SEL, 128), 0) == lax.broadcasted_iota(I32, (SEL, 128), 1))


GROUP = 4


def _grouped_tokens(tb, gather, reduce):
    half = (tuple(range(GROUP)), tuple(range(GROUP, 2 * GROUP)))
    gather(0, half[0], 0, 1)

    def trip(i, carry):
        t = 2 * GROUP * i
        for j in range(GROUP):
            gather(t + GROUP, half[1], j, GROUP)
            reduce(t + j, half[0][j])
        nxt = jnp.minimum(t + 2 * GROUP, tb - GROUP)
        for j in range(GROUP):
            gather(nxt, half[0], j, GROUP)
            reduce(t + GROUP + j, half[1][j])
        return carry

    lax.fori_loop(0, tb // (2 * GROUP), trip, 0)


def _peer_u_kernel(idx_ref, x_ref, gate_ref, tab_ref, w_ref, buf_ref, act_ref):
    tb = x_ref.shape[0]
    diag = _diag_mask()

    def gather(t0, slots, part, parts):
        _gather_group(idx_ref, tab_ref, buf_ref, t0, slots, part, parts)

    def reduce(t, slot):
        total = None
        for c in range(CHUNKS):
            lo, hi = _chunk(buf_ref, slot, c)
            term = lo * x_ref[t, c:c + 1, :] + hi * x_ref[t, c + CHUNKS:c + CHUNKS + 1, :]
            total = term if total is None else total + term
        act_rep = _lane_replicated_sum(total)
        act_ref[pl.ds(t, 1), :] = jnp.sum(jnp.where(diag, act_rep, 0.0), axis=0, keepdims=True)

    _grouped_tokens(tb, gather, reduce)
    act = act_ref[...]
    gelu = 0.5 * act * (1.0 + lax.erf(act * (2.0 ** -0.5)))
    w_ref[...] = gate_ref[...] * gelu


def _gather_scratch():
    return pltpu.VMEM((2 * GROUP, CHUNKS * CHUNK_STRIDE, 128), I32)


def _index_spec(tb):
    return pl.BlockSpec((1, 1, tb * SEL), lambda i: (i, 0, 0), memory_space=pltpu.SMEM)


def _peer_u(idx3, x3, gates, tab, *, tb, tokens):
    return pl.pallas_call(
        _peer_u_kernel,
        grid=(tokens // tb,),
        in_specs=[
            _index_spec(tb),
            pl.BlockSpec((tb, 2 * CHUNKS, 128), lambda i: (i, 0, 0)),
            pl.BlockSpec((tb, SEL), lambda i: (i, 0)),
            pl.BlockSpec(memory_space=pltpu.VMEM),
        ],
        out_specs=pl.BlockSpec((tb, SEL), lambda i: (i, 0)),
        out_shape=jax.ShapeDtypeStruct((tokens, SEL), F32),
        scratch_shapes=[_gather_scratch(), pltpu.VMEM((tb, SEL), F32)],
        compiler_params=pltpu.CompilerParams(
            dimension_semantics=("arbitrary",), vmem_limit_bytes=VMEM_LIMIT),
        name="peer_u",
    )(idx3, x3, gates, tab)


def _peer_v_kernel(idx_ref, w_ref, h_ref, tab_ref, y_ref, buf_ref):
    tb = h_ref.shape[0]
    diag = _diag_mask()

    def gather(t0, slots, part, parts):
        _gather_group(idx_ref, tab_ref, buf_ref, t0, slots, part, parts)

    def reduce(t, slot):
        w_rep = _lane_replicated_sum(jnp.where(diag, w_ref[pl.ds(t, 1), :], 0.0))
        lows, highs = [], []
        for c in range(CHUNKS):
            lo, hi = _chunk(buf_ref, slot, c)
            lows.append(jnp.sum(lo * w_rep, axis=0, keepdims=True))
            highs.append(jnp.sum(hi * w_rep, axis=0, keepdims=True))
        y_ref[pl.ds(t, 1), :] = h_ref[pl.ds(t, 1), :] + jnp.concatenate(lows + highs, axis=-1)

    _grouped_tokens(tb, gather, reduce)


def _peer_v(idx3, w, h2, tab, *, tb, tokens):
    D = h2.shape[1]
    return pl.pallas_call(
        _peer_v_kernel,
        grid=(tokens // tb,),
        in_specs=[
            _index_spec(tb),
            pl.BlockSpec((tb, SEL), lambda i: (i, 0)),
            pl.BlockSpec((tb, D), lambda i: (i, 0)),
            pl.BlockSpec(memory_space=pltpu.VMEM),
        ],
        out_specs=pl.BlockSpec((tb, D), lambda i: (i, 0)),
        out_shape=jax.ShapeDtypeStruct((tokens, D), F32),
        scratch_shapes=[_gather_scratch()],
        compiler_params=pltpu.CompilerParams(
            dimension_semantics=("arbitrary",), vmem_limit_bytes=VMEM_LIMIT),
        name="peer_v",
    )(idx3, w, h2, tab)


SC_CORES = 2
SC_WORKERS = 32
SC_LANES = 16
ROW_WORDS = 512
HALF_WORDS = ROW_WORDS // 2
HALF_VECS = HALF_WORDS // SC_LANES


def _sc_lanes(j):
    return pl.ds(j * SC_LANES, SC_LANES)


def _sc_token_pipeline(ids_hbm, tab_hbm, ids_v, rows_v, sems, *, first, per, load_extra, process):
    base = (lax.axis_index("s") * SC_CORES + lax.axis_index("c")) * per

    def gather(slot, half):
        return pltpu.make_async_copy(tab_hbm.at[ids_v.at[slot, half]], rows_v.at[half],
                                     sems.at[half])

    def load_token(i, slot):
        pltpu.sync_copy(ids_hbm.at[first + base + i], ids_v.at[slot])
        load_extra(base + i, slot)

    load_token(0, 0)
    gather(0, 0).start()

    @pl.loop(0, per)
    def _(i):
        slot = i % 2
        gather(slot, 1).start()
        gather(slot, 0).wait()
        process(base + i, slot, 0)

        @pl.when(i + 1 < per)
        def _():
            load_token(i + 1, 1 - slot)
            gather(1 - slot, 0).start()

        gather(slot, 1).wait()
        process(base + i, slot, 1)


def _peer_u_sc_kernel(ids_hbm, x_hbm, tab_hbm, act_hbm, ids_v, x_v, part_v, act_v, rows_v, sems,
                      *, first, per):
    def load_extra(local, slot):
        pltpu.sync_copy(x_hbm.at[first + local], x_v.at[slot])

    def process(local, slot, half):
        off = half * HALF_VECS
        x_lo = [x_v[slot, _sc_lanes(off + j)] for j in range(HALF_VECS)]
        x_hi = [x_v[slot, _sc_lanes(2 * HALF_VECS + off + j)] for j in range(HALF_VECS)]

        @pl.loop(0, SEL)
        def _(k):
            sums = [None] * 4
            for j in range(HALF_VECS):
                lo, hi = _unpack(rows_v[half, k, _sc_lanes(j)])
                term = lo * x_lo[j] + hi * x_hi[j]
                sums[j % 4] = term if sums[j % 4] is None else sums[j % 4] + term
            part_v[half * SEL + k, :] = (sums[0] + sums[1]) + (sums[2] + sums[3])

        if half == 1:
            lane_ids = lax.broadcasted_iota(I32, (SC_LANES,), 0)
            for kb in range(SEL // SC_LANES):
                rows0 = lane_ids + kb * SC_LANES
                total = None
                for l in range(SC_LANES):
                    col = jnp.full((SC_LANES,), l, I32)
                    both = (plsc.load_gather(part_v, [rows0, col])
                            + plsc.load_gather(part_v, [rows0 + SEL, col]))
                    total = both if total is None else total + both
                act_v[_sc_lanes(kb)] = total
            pltpu.sync_copy(act_v, act_hbm.at[local])

    _sc_token_pipeline(ids_hbm, tab_hbm, ids_v, rows_v, sems, first=first, per=per,
                       load_extra=load_extra, process=process)


def _sc_call(body, out_cols, n, scratch, name):
    return pl.kernel(
        body,
        out_type=jax.ShapeDtypeStruct((n, out_cols), F32),
        mesh=plsc.VectorSubcoreMesh(core_axis_name="c", subcore_axis_name="s"),
        scratch_types=scratch + [pltpu.VMEM((2, SEL, HALF_WORDS), I32),
                                 pltpu.SemaphoreType.DMA((2,))],
        compiler_params=pltpu.CompilerParams(needs_layout_passes=False),
        name=name,
    )


def _peer_u_sc(half_ids, x2, tab_halves, *, first):
    T, D = x2.shape
    n = T - first
    body = functools.partial(_peer_u_sc_kernel, first=first, per=n // SC_WORKERS)
    scratch = [pltpu.VMEM((2, 2, SEL), I32), pltpu.VMEM((2, D), F32),
               pltpu.VMEM((2 * SEL, SC_LANES), F32), pltpu.VMEM((SEL,), F32)]
    return _sc_call(body, SEL, n, scratch, "peer_u_sc")(half_ids, x2, tab_halves)


def _peer_v_sc_kernel(ids_hbm, w_hbm, h_hbm, tab_hbm, y_hbm, ids_v, w_v, h_v, y_v, rows_v, sems,
                      *, first, per):
    def load_extra(local, slot):
        pltpu.sync_copy(w_hbm.at[local], w_v.at[slot])
        pltpu.sync_copy(h_hbm.at[first + local], h_v.at[slot])

    def process(local, slot, half):
        off = half * HALF_VECS
        init = (tuple(h_v[slot, _sc_lanes(off + j)] for j in range(HALF_VECS))
                + tuple(h_v[slot, _sc_lanes(2 * HALF_VECS + off + j)] for j in range(HALF_VECS)))
        slot_vec = jnp.full((SC_LANES,), slot, I32)

        def body(k, acc):
            wk = plsc.load_gather(w_v, [slot_vec, jnp.full((SC_LANES,), k, I32)])
            lows, highs = [], []
            for j in range(HALF_VECS):
                lo, hi = _unpack(rows_v[half, k, _sc_lanes(j)])
                lows.append(acc[j] + wk * lo)
                highs.append(acc[HALF_VECS + j] + wk * hi)
            return tuple(lows + highs)

        acc = lax.fori_loop(0, SEL, body, init)
        for j in range(HALF_VECS):
            y_v[_sc_lanes(off + j)] = acc[j]
            y_v[_sc_lanes(2 * HALF_VECS + off + j)] = acc[HALF_VECS + j]
        if half == 1:
            pltpu.sync_copy(y_v, y_hbm.at[local])

    _sc_token_pipeline(ids_hbm, tab_hbm, ids_v, rows_v, sems, first=first, per=per,
                       load_extra=load_extra, process=process)


def _peer_v_sc(half_ids, w_tail, h2, tab_halves, *, first):
    T, D = h2.shape
    n = T - first
    body = functools.partial(_peer_v_sc_kernel, first=first, per=n // SC_WORKERS)
    scratch = [pltpu.VMEM((2, 2, SEL), I32), pltpu.VMEM((2, SEL), F32),
               pltpu.VMEM((2, D), F32), pltpu.VMEM((D,), F32)]
    return _sc_call(body, D, n, scratch, "peer_v_sc")(half_ids, w_tail, h2, tab_halves)


def _gelu_gate_kernel(act_ref, gate_ref, w_ref):
    act = act_ref[...]
    w_ref[...] = gate_ref[...] * (0.5 * act * (1.0 + lax.erf(act * (2.0 ** -0.5))))


def _gelu_gate(act_tail, gates, *, first, tm):
    n = act_tail.shape[0]
    return pl.pallas_call(
        _gelu_gate_kernel,
        grid=(n // tm,),
        in_specs=[pl.BlockSpec((tm, SEL), lambda i: (i, 0)),
                  pl.BlockSpec((tm, SEL), lambda i: (i + first // tm, 0))],
        out_specs=pl.BlockSpec((tm, SEL), lambda i: (i, 0)),
        out_shape=jax.ShapeDtypeStruct((n, SEL), F32),
        name="peer_gelu_gate",
    )(act_tail, gates)


def _pack_table(tab):
    n, d = tab.shape
    bits = lax.bitcast_convert_type(tab.astype(BF16), jnp.uint16).astype(jnp.uint32)
    word = bits[:, :d // 2] | (bits[:, d // 2:] << 16)
    return lax.bitcast_convert_type(word, I32).reshape(n * 4, 128)


def _layer(x2, p, *, batch, seq, tm_in, tm_mix, tm_topk, tb, sc_tokens):
    T, D = x2.shape
    row = lambda a: a.reshape(1, -1)
    w_in_bf = p["w_in"].astype(BF16)
    gq_t = row(jnp.tile(p["q_norm_g"], ATTN_HEADS))
    gk_t = row(jnp.tile(p["k_norm_g"], ATTN_HEADS))
    qh, kh, vh, hg = _inproj(x2, row(p["norm_mix_g"]), w_in_bf, gq_t, gk_t, tm=tm_in)
    o = _attention(qh, kh, vh, batch=batch, seq=seq)

    wq = p["peer_wq"]
    wq_hi = wq.astype(BF16)
    wq_lo = (wq - wq_hi.astype(F32)).astype(BF16)
    h, xn, xn3, pq = _mix(x2, o, hg, p["conv_w"], row(p["conv_b"]), row(p["conv_ln_g"]),
                     row(p["conv_ln_b"]), row(p["attn_out_g"]), p["w_out"].astype(BF16),
                     row(p["norm_ffn_g"]), wq_hi, wq_lo, seq=seq, tm=tm_mix)

    idx, gates = _topk(pq, p["peer_k1"], p["peer_k2"], tm=tm_topk)
    u_tab = _pack_table(p["peer_u"])
    v_tab = _pack_table(p["peer_v"])
    head = T - sc_tokens
    half_row = idx // (CHUNKS // 2)
    half_ids = jnp.stack([half_row, half_row + 1], axis=1)
    idx3 = (half_ids[:, 0, :] * (CHUNKS // 2)).reshape(T // tb, 1, tb * SEL)
    act_tail = _peer_u_sc(half_ids, xn, u_tab.reshape(-1, HALF_WORDS), first=head)
    w_head = _peer_u(idx3, xn3, gates, u_tab, tb=tb, tokens=head)
    w_tail = _gelu_gate(act_tail, gates, first=head, tm=tm_topk)
    y_tail = _peer_v_sc(half_ids, w_tail, h, v_tab.reshape(-1, HALF_WORDS), first=head)
    y_head = _peer_v(idx3, w_head, h, v_tab, tb=tb, tokens=head)
    return jnp.concatenate([y_head, y_tail], axis=0)


def kernel(x, norm_mix_g, w_in, q_norm_g, k_norm_g, attn_out_g, conv_w, conv_b, conv_ln_g,
           conv_ln_b, w_out, norm_ffn_g, peer_wq, peer_k1, peer_k2, peer_u, peer_v):
    batch, seq, d = x.shape
    stacked = dict(norm_mix_g=norm_mix_g, w_in=w_in, q_norm_g=q_norm_g, k_norm_g=k_norm_g,
                   attn_out_g=attn_out_g, conv_w=conv_w, conv_b=conv_b, conv_ln_g=conv_ln_g,
                   conv_ln_b=conv_ln_b, w_out=w_out, norm_ffn_g=norm_ffn_g, peer_wq=peer_wq,
                   peer_k1=peer_k1, peer_k2=peer_k2, peer_u=peer_u, peer_v=peer_v)
    h = x.reshape(batch * seq, d)
    for l in range(w_in.shape[0]):
        p = {name: a[l] for name, a in stacked.items()}
        h = _layer(h, p, batch=batch, seq=seq, tm_in=256, tm_mix=256,
                   tm_topk=256, tb=64, sc_tokens=12288)
    return h.reshape(batch, seq, d)
```

```python
import functools

import jax
import jax.numpy as jnp
from jax import lax
from jax.experimental import pallas as pl
from jax.experimental.pallas import tpu as pltpu
from jax.experimental.pallas import tpu_sc as plsc

F32 = jnp.float32
BF16 = jnp.bfloat16
I32 = jnp.int32

EPS = 1e-6
HEAD_DIM = 64
ATTN_HEADS = 8
ATTN_WIDTH = ATTN_HEADS * HEAD_DIM
CONV_KERNEL = 31
CONV_HALO = 32
PEER_HEADS = 8
PEER_HALF = 64
N_KEYS = 128
PEER_TOPK = 16
SEL = PEER_HEADS * PEER_TOPK
EXP_UNDERFLOW = -88.0
HI_MASK = -65536

VMEM_LIMIT = 56 * 1024 * 1024


def _split_bf16(a):
    hi = a.astype(BF16)
    lo = (a - hi.astype(F32)).astype(BF16)
    return hi, lo


def _dot(a, b):
    return jnp.dot(a, b, preferred_element_type=F32)


def _dot_nt(a, b):
    return lax.dot_general(a, b, (((1,), (1,)), ((), ())), preferred_element_type=F32)


def _sigmoid(x):
    return 1.0 / (1.0 + jnp.exp(-x))


def _inproj_kernel(x_ref, g_ref, w_ref, gq_ref, gk_ref, q_ref, k_ref, v_ref, hg_ref):
    x = x_ref[...]
    ms = jnp.mean(x * x, axis=-1, keepdims=True)
    xn = (x * lax.rsqrt(ms + EPS) * g_ref[...]).astype(BF16)
    proj = _dot(xn, w_ref[...])

    r = lax.broadcasted_iota(I32, (ATTN_WIDTH, ATTN_WIDTH), 0) // HEAD_DIM
    c = lax.broadcasted_iota(I32, (ATTN_WIDTH, ATTN_WIDTH), 1) // HEAD_DIM
    same_head = (r == c).astype(BF16)

    def head_norm(t, g):
        hi, lo = _split_bf16(t * t)
        msq = (_dot(hi, same_head) + _dot(lo, same_head)) * (1.0 / HEAD_DIM)
        return t * lax.rsqrt(msq + EPS) * g

    q = head_norm(proj[:, :ATTN_WIDTH], gq_ref[...]) * (HEAD_DIM ** -0.5)
    k = head_norm(proj[:, ATTN_WIDTH:2 * ATTN_WIDTH], gk_ref[...])
    v = proj[:, 2 * ATTN_WIDTH:3 * ATTN_WIDTH]
    for h in range(ATTN_HEADS):
        sl = slice(h * HEAD_DIM, (h + 1) * HEAD_DIM)
        q_ref[h] = q[:, sl].astype(BF16)
        k_ref[h] = k[:, sl].astype(BF16)
        v_ref[h] = v[:, sl].astype(BF16)
    cw = (proj.shape[1] - 3 * ATTN_WIDTH) // 2
    a = proj[:, 3 * ATTN_WIDTH:3 * ATTN_WIDTH + cw]
    gate = proj[:, 3 * ATTN_WIDTH + cw:]
    hg_ref[...] = a * _sigmoid(gate)


def _inproj(x2, g, w_bf, gq_t, gk_t, *, tm):
    T, D = x2.shape
    E = w_bf.shape[1]
    cw = (E - 3 * ATTN_WIDTH) // 2
    head_shape = jax.ShapeDtypeStruct((ATTN_HEADS, T, HEAD_DIM), BF16)
    head_spec = pl.BlockSpec((ATTN_HEADS, tm, HEAD_DIM), lambda i: (0, i, 0))
    return pl.pallas_call(
        _inproj_kernel,
        grid=(T // tm,),
        in_specs=[
            pl.BlockSpec((tm, D), lambda i: (i, 0)),
            pl.BlockSpec((1, D), lambda i: (0, 0)),
            pl.BlockSpec((D, E), lambda i: (0, 0)),
            pl.BlockSpec((1, ATTN_WIDTH), lambda i: (0, 0)),
            pl.BlockSpec((1, ATTN_WIDTH), lambda i: (0, 0)),
        ],
        out_specs=[head_spec, head_spec, head_spec,
                   pl.BlockSpec((tm, cw), lambda i: (i, 0))],
        out_shape=[head_shape, head_shape, head_shape,
                   jax.ShapeDtypeStruct((T, cw), F32)],
        compiler_params=pltpu.CompilerParams(
            dimension_semantics=("arbitrary",), vmem_limit_bytes=VMEM_LIMIT),
        name="inproj",
    )(x2, g, w_bf, gq_t, gk_t)


def _attn_span(q, k, v, offset, carry, suffix):
    tw = suffix.shape[0]
    z = _dot_nt(q, k)
    sp = jnp.maximum(z, 0.0) + jnp.log(1.0 + jnp.exp(-jnp.abs(z)))
    col_minus_row = (lax.broadcasted_iota(I32, z.shape, 1)
                     - lax.broadcasted_iota(I32, z.shape, 0))
    mask = col_minus_row < offset
    log_keep = jnp.where(mask, -sp, 0.0)
    later = []
    for s in reversed(range(z.shape[1] // tw)):
        lk = log_keep[:, s * tw:(s + 1) * tw]
        hi, lo = _split_bf16(lk)
        later.append(carry + (_dot(hi, suffix) + _dot(lo, suffix)))
        carry = carry + jnp.sum(lk, axis=-1, keepdims=True)
    later = jnp.concatenate(later[::-1], axis=-1)
    att = jnp.where(mask, jnp.exp(z - sp + later), 0.0)
    return carry, _dot(att.astype(BF16), v)


def _attn_kernel(q_ref, k_ref, v_ref, o_ref, carry_ref, acc_ref, *, rows, span, tw):
    g = pl.program_id(2)
    hp = q_ref.shape[0]
    suffix = (lax.broadcasted_iota(I32, (tw, tw), 0)
              > lax.broadcasted_iota(I32, (tw, tw), 1)).astype(BF16)

    start = pl.multiple_of(jnp.maximum(g * rows - (span - rows), 0), rows)
    cmax = None
    for hh in range(hp):
        carry, acc = _attn_span(q_ref[hh], k_ref[hh, pl.ds(start, span), :],
                                v_ref[hh, pl.ds(start, span), :], g * rows - start,
                                jnp.zeros((rows, 1), F32), suffix)
        carry_ref[hh] = carry
        acc_ref[hh] = acc
        cmax = carry if cmax is None else jnp.maximum(cmax, carry)

    @pl.when(jnp.max(cmax) > EXP_UNDERFLOW)
    def _():
        for hh in range(hp):

            def cond(st):
                j, carry, _ = st
                return jnp.logical_and(j >= 0, jnp.max(carry) > EXP_UNDERFLOW)

            def body(st, hh=hh):
                j, carry, acc = st
                ks = pl.multiple_of(j * tw, tw)
                carry, out = _attn_span(q_ref[hh], k_ref[hh, pl.ds(ks, tw), :],
                                        v_ref[hh, pl.ds(ks, tw), :], rows + tw, carry, suffix)
                return j - 1, carry, acc + out

            init = (start // tw - 1, carry_ref[hh], acc_ref[hh])
            acc_ref[hh] = lax.while_loop(cond, body, init)[2]

    o_ref[...] = jnp.concatenate([acc_ref[hh] for hh in range(hp)], axis=-1)


def _attention(qh, kh, vh, *, batch, seq, rows=256, span=512, tw=256, heads_per_step=2):
    H, T, hd = qh.shape
    ng = seq // rows
    hp = heads_per_step
    return pl.pallas_call(
        functools.partial(_attn_kernel, rows=rows, span=span, tw=tw),
        grid=(H // hp, batch, ng),
        in_specs=[
            pl.BlockSpec((hp, rows, hd), lambda h, b, i: (h, b * ng + i, 0)),
            pl.BlockSpec((hp, seq, hd), lambda h, b, i: (h, b, 0)),
            pl.BlockSpec((hp, seq, hd), lambda h, b, i: (h, b, 0)),
        ],
        out_specs=pl.BlockSpec((rows, hp * hd), lambda h, b, i: (b * ng + i, h)),
        out_shape=jax.ShapeDtypeStruct((T, H * hd), F32),
        scratch_shapes=[pltpu.VMEM((hp, rows, 1), F32),
                        pltpu.VMEM((hp, rows, hd), F32)],
        compiler_params=pltpu.CompilerParams(
            dimension_semantics=("arbitrary", "arbitrary", "arbitrary"),
            vmem_limit_bytes=VMEM_LIMIT),
        name="sb_attention",
    )(qh, kh, vh)


def _mix_kernel(x_ref, o_ref, hg_ref, hgp_ref, cw_ref, cb_ref, lg_ref, lb_ref, ag_ref,
                wo_ref, g2_ref, wqh_ref, wql_ref, h_ref, xn_ref, xn3_ref, qh_ref, ext_ref,
                *, blocks_per_seq):
    tm = x_ref.shape[0]
    first = (pl.program_id(0) % blocks_per_seq) == 0
    ext_ref[0:CONV_HALO, :] = jnp.where(first, 0.0, hgp_ref[...])
    ext_ref[CONV_HALO:, :] = hg_ref[...]
    conv = jnp.zeros(hg_ref.shape, F32)
    for j in range(CONV_KERNEL):
        off = CONV_HALO - (CONV_KERNEL - 1) + j
        conv = conv + cw_ref[j:j + 1, :] * ext_ref[pl.ds(off, tm), :]
    conv = conv + cb_ref[...]
    mu = jnp.mean(conv, axis=-1, keepdims=True)
    xc = conv - mu
    var = jnp.mean(xc * xc, axis=-1, keepdims=True)
    y = xc * lax.rsqrt(var + EPS) * lg_ref[...] + lb_ref[...]
    o_conv = y * _sigmoid(y)

    o = o_ref[...]
    o_attn = o * lax.rsqrt(jnp.mean(o * o, axis=-1, keepdims=True) + EPS) * ag_ref[...]
    mixed = jnp.concatenate([o_attn, o_conv], axis=-1).astype(BF16)
    h = x_ref[...] + _dot(mixed, wo_ref[...])
    h_ref[...] = h

    xn = h * lax.rsqrt(jnp.mean(h * h, axis=-1, keepdims=True) + EPS) * g2_ref[...]
    xn_ref[...] = xn
    for c in range(xn3_ref.shape[1]):
        xn3_ref[:, c, :] = xn[:, c * 128:(c + 1) * 128]
    hi, lo = _split_bf16(xn)
    wqh = wqh_ref[...]
    qh_ref[...] = _dot(hi, wqh) + _dot(lo, wqh) + _dot(hi, wql_ref[...])


def _mix(x2, o, hg, conv_w, conv_b, ln_g, ln_b, attn_g, wo_bf, g2, wq_hi, wq_lo, *, seq, tm):
    T, D = x2.shape
    cwid = hg.shape[1]
    E = wq_hi.shape[1]
    halo_per_block = tm // CONV_HALO
    row = lambda i: (i, 0)
    const = lambda i: (0, 0)
    return pl.pallas_call(
        functools.partial(_mix_kernel, blocks_per_seq=seq // tm),
        grid=(T // tm,),
        in_specs=[
            pl.BlockSpec((tm, D), row),
            pl.BlockSpec((tm, o.shape[1]), row),
            pl.BlockSpec((tm, cwid), row),
            pl.BlockSpec((CONV_HALO, cwid),
                         lambda i: (jnp.maximum(i * halo_per_block - 1, 0), 0)),
            pl.BlockSpec((CONV_KERNEL, cwid), const),
            pl.BlockSpec((1, cwid), const),
            pl.BlockSpec((1, cwid), const),
            pl.BlockSpec((1, cwid), const),
            pl.BlockSpec((1, o.shape[1]), const),
            pl.BlockSpec(wo_bf.shape, const),
            pl.BlockSpec((1, D), const),
            pl.BlockSpec(wq_hi.shape, const),
            pl.BlockSpec(wq_lo.shape, const),
        ],
        out_specs=[pl.BlockSpec((tm, D), row), pl.BlockSpec((tm, D), row),
                   pl.BlockSpec((tm, D // 128, 128), lambda i: (i, 0, 0)),
                   pl.BlockSpec((tm, E), row)],
        out_shape=[jax.ShapeDtypeStruct((T, D), F32), jax.ShapeDtypeStruct((T, D), F32),
                   jax.ShapeDtypeStruct((T, D // 128, 128), F32),
                   jax.ShapeDtypeStruct((T, E), F32)],
        scratch_shapes=[pltpu.VMEM((tm + CONV_HALO, cwid), F32)],
        compiler_params=pltpu.CompilerParams(
            dimension_semantics=("arbitrary",), vmem_limit_bytes=VMEM_LIMIT),
        name="mix_outproj",
    )(x2, o, hg, hg, conv_w, conv_b, ln_g, ln_b, attn_g, wo_bf, g2, wq_hi, wq_lo)


def _top16(s, pos=None, payload=None):
    if pos is None:
        pos = lax.broadcasted_iota(I32, s.shape, 0).astype(F32)
    vals, outs = [], []
    for _ in range(PEER_TOPK):
        m = jnp.max(s, axis=0, keepdims=True)
        idx = jnp.min(jnp.where(s == m, pos, 1e9), axis=0, keepdims=True)
        sel = pos == idx
        if payload is None:
            outs.append(idx)
        else:
            outs.append(jnp.sum(jnp.where(sel, payload, 0.0), axis=0, keepdims=True))
        s = jnp.where(sel, -jnp.inf, s)
        vals.append(m)
    return jnp.concatenate(vals, axis=0), jnp.concatenate(outs, axis=0)


def _dot3_nt(a, b):
    ah, al = _split_bf16(a)
    bh, bl = _split_bf16(b)
    return _dot_nt(ah, bh) + _dot_nt(al, bh) + _dot_nt(ah, bl)


def _pair_candidates(v1, i1, v2, i2):
    r8 = lax.broadcasted_iota(I32, (8, v1.shape[1]), 0).astype(F32)
    sc, ex, ps = [], [], []

    def add(s1, e1, s2, e2, p, keep=None):
        s = s1 + s2
        sc.append(s if keep is None else jnp.where(keep, s, -jnp.inf))
        ex.append(e1 * float(N_KEYS) + e2)
        ps.append(p if keep is None else jnp.where(keep, p, 2e9))

    for a, half in ((0, 0), (0, 1), (1, 0), (2, 0), (3, 0)):
        b = slice(8 * half, 8 * half + 8)
        add(v1[a:a + 1], i1[a:a + 1], v2[b], i2[b], r8 + float(a * PEER_TOPK + 8 * half))
    for b in range(3):
        add(v1[0:8], i1[0:8], v2[b:b + 1], i2[b:b + 1], r8 * float(PEER_TOPK) + float(b),
            keep=r8 >= 4.0)
    add(v1[8:16], i1[8:16], v2[0:1], i2[0:1], (r8 + 8.0) * float(PEER_TOPK))
    return (jnp.concatenate(sc, axis=0), jnp.concatenate(ex, axis=0),
            jnp.concatenate(ps, axis=0))


def _topk_kernel(qh_ref, k1_ref, k2_ref, idx_ref, gate_ref):
    all_experts, all_gates = [], []
    for h in range(PEER_HEADS):
        base = h * 2 * PEER_HALF
        q1 = qh_ref[:, base:base + PEER_HALF]
        q2 = qh_ref[:, base + PEER_HALF:base + 2 * PEER_HALF]
        s1 = _dot3_nt(k1_ref[h], q1)
        s2 = _dot3_nt(k2_ref[h], q2)
        v1, i1 = _top16(s1)
        v2, i2 = _top16(s2)
        cand, cexp, cpos = _pair_candidates(v1, i1, v2, i2)
        top_s, experts = _top16(cand, pos=cpos, payload=cexp)
        e = jnp.exp(top_s - top_s[0:1, :])
        all_experts.append(experts)
        all_gates.append(e / jnp.sum(e, axis=0, keepdims=True))
    idx_ref[...] = jnp.concatenate(all_experts, axis=0).T.astype(I32) * CHUNKS
    gate_ref[...] = jnp.concatenate(all_gates, axis=0).T


def _topk(qh, k1, k2, *, tm):
    T, E = qh.shape
    out_spec = pl.BlockSpec((tm, SEL), lambda i: (i, 0))
    return pl.pallas_call(
        _topk_kernel,
        grid=(T // tm,),
        in_specs=[
            pl.BlockSpec((tm, E), lambda i: (i, 0)),
            pl.BlockSpec(k1.shape, lambda i: (0, 0, 0)),
            pl.BlockSpec(k2.shape, lambda i: (0, 0, 0)),
        ],
        out_specs=[out_spec, out_spec],
        out_shape=[jax.ShapeDtypeStruct((T, SEL), I32), jax.ShapeDtypeStruct((T, SEL), F32)],
        compiler_params=pltpu.CompilerParams(
            dimension_semantics=("arbitrary",), vmem_limit_bytes=VMEM_LIMIT),
        name="peer_topk",
    )(qh, k1, k2)


CHUNKS = 4
CHUNK_STRIDE = SEL + 8


def _unpack(word):
    lo = lax.bitcast_convert_type(word << 16, F32)
    hi = lax.bitcast_convert_type(word & HI_MASK, F32)
    return lo, hi


def _gather_group(idx_ref, tab_ref, buf_ref, t0, slots, part=0, parts=1):
    ids = [idx_ref.at[0, 0, pl.ds((t0 + j) * SEL, SEL)] for j in range(len(slots))]
    for k in range(part * SEL // parts, (part + 1) * SEL // parts):
        for j, s in enumerate(slots):
            r = pl.multiple_of(ids[j][k], CHUNKS)
            buf_ref[s, pl.ds(k, CHUNKS, stride=CHUNK_STRIDE), :] = tab_ref[pl.ds(r, CHUNKS), :]


def _cols(c):
    return slice(c * 128, (c + 1) * 128)


def _chunk(buf_ref, s, c):
    return _unpack(buf_ref[s, c * CHUNK_STRIDE:c * CHUNK_STRIDE + SEL, :])


def _lane_replicated_sum(a):
    ones = jnp.ones((128, 128), BF16)
    hi, lo = _split_bf16(a)
    return _dot(hi, ones) + _dot(lo, ones)


def _diag_mask():
    return (lax.broadcasted_iota(I32, (SEL, 128), 0) == lax.broadcasted_iota(I32, (SEL, 128), 1))


GROUP = 4


def _grouped_tokens(tb, gather, reduce):
    half = (tuple(range(GROUP)), tuple(range(GROUP, 2 * GROUP)))
    gather(0, half[0], 0, 1)

    def trip(i, carry):
        t = 2 * GROUP * i
        for j in range(GROUP):
            gather(t + GROUP, half[1], j, GROUP)
            reduce(t + j, half[0][j])
        nxt = jnp.minimum(t + 2 * GROUP, tb - GROUP)
        for j in range(GROUP):
            gather(nxt, half[0], j, GROUP)
            reduce(t + GROUP + j, half[1][j])
        return carry

    lax.fori_loop(0, tb // (2 * GROUP), trip, 0)


def _peer_u_kernel(idx_ref, x_ref, gate_ref, tab_ref, w_ref, buf_ref, act_ref):
    tb = x_ref.shape[0]
    diag = _diag_mask()

    def gather(t0, slots, part, parts):
        _gather_group(idx_ref, tab_ref, buf_ref, t0, slots, part, parts)

    def reduce(t, slot):
        total = None
        for c in range(CHUNKS):
            lo, hi = _chunk(buf_ref, slot, c)
            term = lo * x_ref[t, c:c + 1, :] + hi * x_ref[t, c + CHUNKS:c + CHUNKS + 1, :]
            total = term if total is None else total + term
        act_rep = _lane_replicated_sum(total)
        act_ref[pl.ds(t, 1), :] = jnp.sum(jnp.where(diag, act_rep, 0.0), axis=0, keepdims=True)

    _grouped_tokens(tb, gather, reduce)
    act = act_ref[...]
    gelu = 0.5 * act * (1.0 + lax.erf(act * (2.0 ** -0.5)))
    w_ref[...] = gate_ref[...] * gelu


def _gather_scratch():
    return pltpu.VMEM((2 * GROUP, CHUNKS * CHUNK_STRIDE, 128), I32)


def _index_spec(tb):
    return pl.BlockSpec((1, 1, tb * SEL), lambda i: (i, 0, 0), memory_space=pltpu.SMEM)


def _peer_u(idx3, x3, gates, tab, *, tb, tokens):
    return pl.pallas_call(
        _peer_u_kernel,
        grid=(tokens // tb,),
        in_specs=[
            _index_spec(tb),
            pl.BlockSpec((tb, 2 * CHUNKS, 128), lambda i: (i, 0, 0)),
            pl.BlockSpec((tb, SEL), lambda i: (i, 0)),
            pl.BlockSpec(memory_space=pltpu.VMEM),
        ],
        out_specs=pl.BlockSpec((tb, SEL), lambda i: (i, 0)),
        out_shape=jax.ShapeDtypeStruct((tokens, SEL), F32),
        scratch_shapes=[_gather_scratch(), pltpu.VMEM((tb, SEL), F32)],
        compiler_params=pltpu.CompilerParams(
            dimension_semantics=("arbitrary",), vmem_limit_bytes=VMEM_LIMIT),
        name="peer_u",
    )(idx3, x3, gates, tab)


def _peer_v_kernel(idx_ref, w_ref, h_ref, tab_ref, y_ref, buf_ref):
    tb = h_ref.shape[0]
    diag = _diag_mask()

    def gather(t0, slots, part, parts):
        _gather_group(idx_ref, tab_ref, buf_ref, t0, slots, part, parts)

    def reduce(t, slot):
        w_rep = _lane_replicated_sum(jnp.where(diag, w_ref[pl.ds(t, 1), :], 0.0))
        lows, highs = [], []
        for c in range(CHUNKS):
            lo, hi = _chunk(buf_ref, slot, c)
            lows.append(jnp.sum(lo * w_rep, axis=0, keepdims=True))
            highs.append(jnp.sum(hi * w_rep, axis=0, keepdims=True))
        y_ref[pl.ds(t, 1), :] = h_ref[pl.ds(t, 1), :] + jnp.concatenate(lows + highs, axis=-1)

    _grouped_tokens(tb, gather, reduce)


def _peer_v(idx3, w, h2, tab, *, tb, tokens):
    D = h2.shape[1]
    return pl.pallas_call(
        _peer_v_kernel,
        grid=(tokens // tb,),
        in_specs=[
            _index_spec(tb),
            pl.BlockSpec((tb, SEL), lambda i: (i, 0)),
            pl.BlockSpec((tb, D), lambda i: (i, 0)),
            pl.BlockSpec(memory_space=pltpu.VMEM),
        ],
        out_specs=pl.BlockSpec((tb, D), lambda i: (i, 0)),
        out_shape=jax.ShapeDtypeStruct((tokens, D), F32),
        scratch_shapes=[_gather_scratch()],
        compiler_params=pltpu.CompilerParams(
            dimension_semantics=("arbitrary",), vmem_limit_bytes=VMEM_LIMIT),
        name="peer_v",
    )(idx3, w, h2, tab)


SC_CORES = 2
SC_WORKERS = 32
SC_LANES = 16
ROW_WORDS = 512
HALF_WORDS = ROW_WORDS // 2
HALF_VECS = HALF_WORDS // SC_LANES


def _sc_lanes(j):
    return pl.ds(j * SC_LANES, SC_LANES)


def _sc_token_pipeline(ids_hbm, tab_hbm, ids_v, rows_v, sems, *, first, per, load_extra, process):
    base = (lax.axis_index("s") * SC_CORES + lax.axis_index("c")) * per

    def gather(slot, half):
        return pltpu.make_async_copy(tab_hbm.at[ids_v.at[slot, half]], rows_v.at[half],
                                     sems.at[half])

    def load_token(i, slot):
        pltpu.sync_copy(ids_hbm.at[first + base + i], ids_v.at[slot])
        load_extra(base + i, slot)

    load_token(0, 0)
    gather(0, 0).start()

    @pl.loop(0, per)
    def _(i):
        slot = i % 2
        gather(slot, 1).start()
        gather(slot, 0).wait()
        process(base + i, slot, 0)

        @pl.when(i + 1 < per)
        def _():
            load_token(i + 1, 1 - slot)
            gather(1 - slot, 0).start()

        gather(slot, 1).wait()
        process(base + i, slot, 1)


def _peer_u_sc_kernel(ids_hbm, x_hbm, tab_hbm, act_hbm, ids_v, x_v, part_v, act_v, rows_v, sems,
                      *, first, per):
    def load_extra(local, slot):
        pltpu.sync_copy(x_hbm.at[first + local], x_v.at[slot])

    def process(local, slot, half):
        off = half * HALF_VECS
        x_lo = [x_v[slot, _sc_lanes(off + j)] for j in range(HALF_VECS)]
        x_hi = [x_v[slot, _sc_lanes(2 * HALF_VECS + off + j)] for j in range(HALF_VECS)]

        @pl.loop(0, SEL)
        def _(k):
            sums = [None] * 4
            for j in range(HALF_VECS):
                lo, hi = _unpack(rows_v[half, k, _sc_lanes(j)])
                term = lo * x_lo[j] + hi * x_hi[j]
                sums[j % 4] = term if sums[j % 4] is None else sums[j % 4] + term
            part_v[half * SEL + k, :] = (sums[0] + sums[1]) + (sums[2] + sums[3])

        if half == 1:
            lane_ids = lax.broadcasted_iota(I32, (SC_LANES,), 0)
            for kb in range(SEL // SC_LANES):
                rows0 = lane_ids + kb * SC_LANES
                total = None
                for l in range(SC_LANES):
                    col = jnp.full((SC_LANES,), l, I32)
                    both = (plsc.load_gather(part_v, [rows0, col])
                            + plsc.load_gather(part_v, [rows0 + SEL, col]))
                    total = both if total is None else total + both
                act_v[_sc_lanes(kb)] = total
            pltpu.sync_copy(act_v, act_hbm.at[local])

    _sc_token_pipeline(ids_hbm, tab_hbm, ids_v, rows_v, sems, first=first, per=per,
                       load_extra=load_extra, process=process)


def _sc_call(body, out_cols, n, scratch, name):
    return pl.kernel(
        body,
        out_type=jax.ShapeDtypeStruct((n, out_cols), F32),
        mesh=plsc.VectorSubcoreMesh(core_axis_name="c", subcore_axis_name="s"),
        scratch_types=scratch + [pltpu.VMEM((2, SEL, HALF_WORDS), I32),
                                 pltpu.SemaphoreType.DMA((2,))],
        compiler_params=pltpu.CompilerParams(needs_layout_passes=False),
        name=name,
    )


def _peer_u_sc(half_ids, x2, tab_halves, *, first):
    T, D = x2.shape
    n = T - first
    body = functools.partial(_peer_u_sc_kernel, first=first, per=n // SC_WORKERS)
    scratch = [pltpu.VMEM((2, 2, SEL), I32), pltpu.VMEM((2, D), F32),
               pltpu.VMEM((2 * SEL, SC_LANES), F32), pltpu.VMEM((SEL,), F32)]
    return _sc_call(body, SEL, n, scratch, "peer_u_sc")(half_ids, x2, tab_halves)


def _peer_v_sc_kernel(ids_hbm, w_hbm, h_hbm, tab_hbm, y_hbm, ids_v, w_v, h_v, y_v, rows_v, sems,
                      *, first, per):
    def load_extra(local, slot):
        pltpu.sync_copy(w_hbm.at[local], w_v.at[slot])
        pltpu.sync_copy(h_hbm.at[first + local], h_v.at[slot])

    def process(local, slot, half):
        off = half * HALF_VECS
        init = (tuple(h_v[slot, _sc_lanes(off + j)] for j in range(HALF_VECS))
                + tuple(h_v[slot, _sc_lanes(2 * HALF_VECS + off + j)] for j in range(HALF_VECS)))
        slot_vec = jnp.full((SC_LANES,), slot, I32)

        def body(k, acc):
            wk = plsc.load_gather(w_v, [slot_vec, jnp.full((SC_LANES,), k, I32)])
            lows, highs = [], []
            for j in range(HALF_VECS):
                lo, hi = _unpack(rows_v[half, k, _sc_lanes(j)])
                lows.append(acc[j] + wk * lo)
                highs.append(acc[HALF_VECS + j] + wk * hi)
            return tuple(lows + highs)

        acc = lax.fori_loop(0, SEL, body, init)
        for j in range(HALF_VECS):
            y_v[_sc_lanes(off + j)] = acc[j]
            y_v[_sc_lanes(2 * HALF_VECS + off + j)] = acc[HALF_VECS + j]
        if half == 1:
            pltpu.sync_copy(y_v, y_hbm.at[local])

    _sc_token_pipeline(ids_hbm, tab_hbm, ids_v, rows_v, sems, first=first, per=per,
                       load_extra=load_extra, process=process)


def _peer_v_sc(half_ids, w_tail, h2, tab_halves, *, first):
    T, D = h2.shape
    n = T - first
    body = functools.partial(_peer_v_sc_kernel, first=first, per=n // SC_WORKERS)
    scratch = [pltpu.VMEM((2, 2, SEL), I32), pltpu.VMEM((2, SEL), F32),
               pltpu.VMEM((2, D), F32), pltpu.VMEM((D,), F32)]
    return _sc_call(body, D, n, scratch, "peer_v_sc")(half_ids, w_tail, h2, tab_halves)


def _gelu_gate_kernel(act_ref, gate_ref, after_ref, w_ref):
    del after_ref
    act = act_ref[...]
    w_ref[...] = gate_ref[...] * (0.5 * act * (1.0 + lax.erf(act * (2.0 ** -0.5))))


def _gelu_gate(act_tail, gates, after, *, first, tm):
    n = act_tail.shape[0]
    return pl.pallas_call(
        _gelu_gate_kernel,
        grid=(n // tm,),
        in_specs=[pl.BlockSpec((tm, SEL), lambda i: (i, 0)),
                  pl.BlockSpec((tm, SEL), lambda i: (i + first // tm, 0)),
                  pl.BlockSpec(memory_space=pl.ANY)],
        out_specs=pl.BlockSpec((tm, SEL), lambda i: (i, 0)),
        out_shape=jax.ShapeDtypeStruct((n, SEL), F32),
        name="peer_gelu_gate",
    )(act_tail, gates, after)


def _pack_table(tab):
    n, d = tab.shape
    bits = lax.bitcast_convert_type(tab.astype(BF16), jnp.uint16).astype(jnp.uint32)
    word = bits[:, :d // 2] | (bits[:, d // 2:] << 16)
    return lax.bitcast_convert_type(word, I32).reshape(n * 4, 128)


def _layer(x2, p, *, batch, seq, tm_in, tm_mix, tm_topk, tb, sc_tokens):
    T, D = x2.shape
    row = lambda a: a.reshape(1, -1)
    w_in_bf = p["w_in"].astype(BF16)
    gq_t = row(jnp.tile(p["q_norm_g"], ATTN_HEADS))
    gk_t = row(jnp.tile(p["k_norm_g"], ATTN_HEADS))
    qh, kh, vh, hg = _inproj(x2, row(p["norm_mix_g"]), w_in_bf, gq_t, gk_t, tm=tm_in)
    o = _attention(qh, kh, vh, batch=batch, seq=seq)

    wq = p["peer_wq"]
    wq_hi = wq.astype(BF16)
    wq_lo = (wq - wq_hi.astype(F32)).astype(BF16)
    h, xn, xn3, pq = _mix(x2, o, hg, p["conv_w"], row(p["conv_b"]), row(p["conv_ln_g"]),
                     row(p["conv_ln_b"]), row(p["attn_out_g"]), p["w_out"].astype(BF16),
                     row(p["norm_ffn_g"]), wq_hi, wq_lo, seq=seq, tm=tm_mix)

    idx, gates = _topk(pq, p["peer_k1"], p["peer_k2"], tm=tm_topk)
    u_tab = _pack_table(p["peer_u"])
    v_tab = _pack_table(p["peer_v"])
    head_u, head_v = T - sc_tokens[0], T - sc_tokens[1]
    half_row = idx // (CHUNKS // 2)
    half_ids = jnp.stack([half_row, half_row + 1], axis=1)
    idx3 = (half_ids[:, 0, :] * (CHUNKS // 2)).reshape(T // tb, 1, tb * SEL)
    act_tail = _peer_u_sc(half_ids, xn, u_tab.reshape(-1, HALF_WORDS), first=head_u)
    w_head = _peer_u(idx3, xn3, gates, u_tab, tb=tb, tokens=head_u)
    w_tail = _gelu_gate(act_tail, gates, w_head, first=head_u, tm=tm_topk)
    w_sc = jnp.concatenate([w_head[head_v:], w_tail], axis=0)
    y_tail = _peer_v_sc(half_ids, w_sc, h, v_tab.reshape(-1, HALF_WORDS), first=head_v)
    y_head = _peer_v(idx3, w_head, h, v_tab, tb=tb, tokens=head_v)
    return jnp.concatenate([y_head, y_tail], axis=0)


def kernel(x, norm_mix_g, w_in, q_norm_g, k_norm_g, attn_out_g, conv_w, conv_b, conv_ln_g,
           conv_ln_b, w_out, norm_ffn_g, peer_wq, peer_k1, peer_k2, peer_u, peer_v):
    batch, seq, d = x.shape
    stacked = dict(norm_mix_g=norm_mix_g, w_in=w_in, q_norm_g=q_norm_g, k_norm_g=k_norm_g,
                   attn_out_g=attn_out_g, conv_w=conv_w, conv_b=conv_b, conv_ln_g=conv_ln_g,
                   conv_ln_b=conv_ln_b, w_out=w_out, norm_ffn_g=norm_ffn_g, peer_wq=peer_wq,
                   peer_k1=peer_k1, peer_k2=peer_k2, peer_u=peer_u, peer_v=peer_v)
    h = x.reshape(batch * seq, d)
    for l in range(w_in.shape[0]):
        p = {name: a[l] for name, a in stacked.items()}
        h = _layer(h, p, batch=batch, seq=seq, tm_in=256, tm_mix=256,
                   tm_topk=256, tb=64, sc_tokens=(10240, 12288))
    return h.reshape(batch, seq, d)
```

```python
import functools

import jax
import jax.numpy as jnp
from jax import lax
from jax.experimental import pallas as pl
from jax.experimental.pallas import tpu as pltpu
from jax.experimental.pallas import tpu_sc as plsc

F32 = jnp.float32
BF16 = jnp.bfloat16
I32 = jnp.int32

EPS = 1e-6
HEAD_DIM = 64
ATTN_HEADS = 8
ATTN_WIDTH = ATTN_HEADS * HEAD_DIM
CONV_KERNEL = 31
CONV_HALO = 32
PEER_HEADS = 8
PEER_HALF = 64
N_KEYS = 128
PEER_TOPK = 16
SEL = PEER_HEADS * PEER_TOPK
EXP_UNDERFLOW = -88.0
HI_MASK = -65536

VMEM_LIMIT = 56 * 1024 * 1024


def _split_bf16(a):
    hi = a.astype(BF16)
    lo = (a - hi.astype(F32)).astype(BF16)
    return hi, lo


def _dot(a, b):
    return jnp.dot(a, b, preferred_element_type=F32)


def _dot_nt(a, b):
    return lax.dot_general(a, b, (((1,), (1,)), ((), ())), preferred_element_type=F32)


def _sigmoid(x):
    return 1.0 / (1.0 + jnp.exp(-x))


def _inproj_kernel(x_ref, g_ref, w_ref, gq_ref, gk_ref, q_ref, k_ref, v_ref, hg_ref):
    x = x_ref[...]
    ms = jnp.mean(x * x, axis=-1, keepdims=True)
    xn = (x * lax.rsqrt(ms + EPS) * g_ref[...]).astype(BF16)
    proj = _dot(xn, w_ref[...])

    r = lax.broadcasted_iota(I32, (ATTN_WIDTH, ATTN_WIDTH), 0) // HEAD_DIM
    c = lax.broadcasted_iota(I32, (ATTN_WIDTH, ATTN_WIDTH), 1) // HEAD_DIM
    same_head = (r == c).astype(BF16)

    def head_norm(t, g):
        hi, lo = _split_bf16(t * t)
        msq = (_dot(hi, same_head) + _dot(lo, same_head)) * (1.0 / HEAD_DIM)
        return t * lax.rsqrt(msq + EPS) * g

    q = head_norm(proj[:, :ATTN_WIDTH], gq_ref[...]) * (HEAD_DIM ** -0.5)
    k = head_norm(proj[:, ATTN_WIDTH:2 * ATTN_WIDTH], gk_ref[...])
    v = proj[:, 2 * ATTN_WIDTH:3 * ATTN_WIDTH]
    for h in range(ATTN_HEADS):
        sl = slice(h * HEAD_DIM, (h + 1) * HEAD_DIM)
        q_ref[h] = q[:, sl].astype(BF16)
        k_ref[h] = k[:, sl].astype(BF16)
        v_ref[h] = v[:, sl].astype(BF16)
    cw = (proj.shape[1] - 3 * ATTN_WIDTH) // 2
    a = proj[:, 3 * ATTN_WIDTH:3 * ATTN_WIDTH + cw]
    gate = proj[:, 3 * ATTN_WIDTH + cw:]
    hg_ref[...] = a * _sigmoid(gate)


def _inproj(x2, g, w_bf, gq_t, gk_t, *, tm):
    T, D = x2.shape
    E = w_bf.shape[1]
    cw = (E - 3 * ATTN_WIDTH) // 2
    head_shape = jax.ShapeDtypeStruct((ATTN_HEADS, T, HEAD_DIM), BF16)
    head_spec = pl.BlockSpec((ATTN_HEADS, tm, HEAD_DIM), lambda i: (0, i, 0))
    return pl.pallas_call(
        _inproj_kernel,
        grid=(T // tm,),
        in_specs=[
            pl.BlockSpec((tm, D), lambda i: (i, 0)),
            pl.BlockSpec((1, D), lambda i: (0, 0)),
            pl.BlockSpec((D, E), lambda i: (0, 0)),
            pl.BlockSpec((1, ATTN_WIDTH), lambda i: (0, 0)),
            pl.BlockSpec((1, ATTN_WIDTH), lambda i: (0, 0)),
        ],
        out_specs=[head_spec, head_spec, head_spec,
                   pl.BlockSpec((tm, cw), lambda i: (i, 0))],
        out_shape=[head_shape, head_shape, head_shape,
                   jax.ShapeDtypeStruct((T, cw), F32)],
        compiler_params=pltpu.CompilerParams(
            dimension_semantics=("arbitrary",), vmem_limit_bytes=VMEM_LIMIT),
        name="inproj",
    )(x2, g, w_bf, gq_t, gk_t)


def _attn_span(q, k, v, offset, carry, suffix):
    tw = suffix.shape[0]
    z = _dot_nt(q, k)
    sp = jnp.maximum(z, 0.0) + jnp.log(1.0 + jnp.exp(-jnp.abs(z)))
    col_minus_row = (lax.broadcasted_iota(I32, z.shape, 1)
                     - lax.broadcasted_iota(I32, z.shape, 0))
    mask = col_minus_row < offset
    log_keep = jnp.where(mask, -sp, 0.0)
    later = []
    for s in reversed(range(z.shape[1] // tw)):
        lk = log_keep[:, s * tw:(s + 1) * tw]
        hi, lo = _split_bf16(lk)
        later.append(carry + (_dot(hi, suffix) + _dot(lo, suffix)))
        carry = carry + jnp.sum(lk, axis=-1, keepdims=True)
    later = jnp.concatenate(later[::-1], axis=-1)
    att = jnp.where(mask, jnp.exp(z - sp + later), 0.0)
    return carry, _dot(att.astype(BF16), v)


def _attn_kernel(q_ref, k_ref, v_ref, o_ref, carry_ref, acc_ref, *, rows, span, tw):
    g = pl.program_id(2)
    hp = q_ref.shape[0]
    suffix = (lax.broadcasted_iota(I32, (tw, tw), 0)
              > lax.broadcasted_iota(I32, (tw, tw), 1)).astype(BF16)

    start = pl.multiple_of(jnp.maximum(g * rows - (span - rows), 0), rows)
    cmax = None
    for hh in range(hp):
        carry, acc = _attn_span(q_ref[hh], k_ref[hh, pl.ds(start, span), :],
                                v_ref[hh, pl.ds(start, span), :], g * rows - start,
                                jnp.zeros((rows, 1), F32), suffix)
        carry_ref[hh] = carry
        acc_ref[hh] = acc
        cmax = carry if cmax is None else jnp.maximum(cmax, carry)

    @pl.when(jnp.max(cmax) > EXP_UNDERFLOW)
    def _():
        for hh in range(hp):

            def cond(st):
                j, carry, _ = st
                return jnp.logical_and(j >= 0, jnp.max(carry) > EXP_UNDERFLOW)

            def body(st, hh=hh):
                j, carry, acc = st
                ks = pl.multiple_of(j * tw, tw)
                carry, out = _attn_span(q_ref[hh], k_ref[hh, pl.ds(ks, tw), :],
                                        v_ref[hh, pl.ds(ks, tw), :], rows + tw, carry, suffix)
                return j - 1, carry, acc + out

            init = (start // tw - 1, carry_ref[hh], acc_ref[hh])
            acc_ref[hh] = lax.while_loop(cond, body, init)[2]

    o_ref[...] = jnp.concatenate([acc_ref[hh] for hh in range(hp)], axis=-1)


def _attention(qh, kh, vh, *, batch, seq, rows=256, span=512, tw=256, heads_per_step=2):
    H, T, hd = qh.shape
    ng = seq // rows
    hp = heads_per_step
    return pl.pallas_call(
        functools.partial(_attn_kernel, rows=rows, span=span, tw=tw),
        grid=(H // hp, batch, ng),
        in_specs=[
            pl.BlockSpec((hp, rows, hd), lambda h, b, i: (h, b * ng + i, 0)),
            pl.BlockSpec((hp, seq, hd), lambda h, b, i: (h, b, 0)),
            pl.BlockSpec((hp, seq, hd), lambda h, b, i: (h, b, 0)),
        ],
        out_specs=pl.BlockSpec((rows, hp * hd), lambda h, b, i: (b * ng + i, h)),
        out_shape=jax.ShapeDtypeStruct((T, H * hd), F32),
        scratch_shapes=[pltpu.VMEM((hp, rows, 1), F32),
                        pltpu.VMEM((hp, rows, hd), F32)],
        compiler_params=pltpu.CompilerParams(
            dimension_semantics=("arbitrary", "arbitrary", "arbitrary"),
            vmem_limit_bytes=VMEM_LIMIT),
        name="sb_attention",
    )(qh, kh, vh)


def _mix_kernel(x_ref, o_ref, hg_ref, hgp_ref, cw_ref, cb_ref, lg_ref, lb_ref, ag_ref,
                wo_ref, g2_ref, wqh_ref, wql_ref, h_ref, xn_ref, xn3_ref, qh_ref, ext_ref,
                *, blocks_per_seq):
    tm = x_ref.shape[0]
    first = (pl.program_id(0) % blocks_per_seq) == 0
    ext_ref[0:CONV_HALO, :] = jnp.where(first, 0.0, hgp_ref[...])
    ext_ref[CONV_HALO:, :] = hg_ref[...]
    conv = jnp.zeros(hg_ref.shape, F32)
    for j in range(CONV_KERNEL):
        off = CONV_HALO - (CONV_KERNEL - 1) + j
        conv = conv + cw_ref[j:j + 1, :] * ext_ref[pl.ds(off, tm), :]
    conv = conv + cb_ref[...]
    mu = jnp.mean(conv, axis=-1, keepdims=True)
    xc = conv - mu
    var = jnp.mean(xc * xc, axis=-1, keepdims=True)
    y = xc * lax.rsqrt(var + EPS) * lg_ref[...] + lb_ref[...]
    o_conv = y * _sigmoid(y)

    o = o_ref[...]
    o_attn = o * lax.rsqrt(jnp.mean(o * o, axis=-1, keepdims=True) + EPS) * ag_ref[...]
    mixed = jnp.concatenate([o_attn, o_conv], axis=-1).astype(BF16)
    h = x_ref[...] + _dot(mixed, wo_ref[...])
    h_ref[...] = h

    xn = h * lax.rsqrt(jnp.mean(h * h, axis=-1, keepdims=True) + EPS) * g2_ref[...]
    xn_ref[...] = xn
    for c in range(xn3_ref.shape[1]):
        xn3_ref[:, c, :] = xn[:, c * 128:(c + 1) * 128]
    hi, lo = _split_bf16(xn)
    wqh = wqh_ref[...]
    qh_ref[...] = _dot(hi, wqh) + _dot(lo, wqh) + _dot(hi, wql_ref[...])


def _mix(x2, o, hg, conv_w, conv_b, ln_g, ln_b, attn_g, wo_bf, g2, wq_hi, wq_lo, *, seq, tm):
    T, D = x2.shape
    cwid = hg.shape[1]
    E = wq_hi.shape[1]
    halo_per_block = tm // CONV_HALO
    row = lambda i: (i, 0)
    const = lambda i: (0, 0)
    return pl.pallas_call(
        functools.partial(_mix_kernel, blocks_per_seq=seq // tm),
        grid=(T // tm,),
        in_specs=[
            pl.BlockSpec((tm, D), row),
            pl.BlockSpec((tm, o.shape[1]), row),
            pl.BlockSpec((tm, cwid), row),
            pl.BlockSpec((CONV_HALO, cwid),
                         lambda i: (jnp.maximum(i * halo_per_block - 1, 0), 0)),
            pl.BlockSpec((CONV_KERNEL, cwid), const),
            pl.BlockSpec((1, cwid), const),
            pl.BlockSpec((1, cwid), const),
            pl.BlockSpec((1, cwid), const),
            pl.BlockSpec((1, o.shape[1]), const),
            pl.BlockSpec(wo_bf.shape, const),
            pl.BlockSpec((1, D), const),
            pl.BlockSpec(wq_hi.shape, const),
            pl.BlockSpec(wq_lo.shape, const),
        ],
        out_specs=[pl.BlockSpec((tm, D), row), pl.BlockSpec((tm, D), row),
                   pl.BlockSpec((tm, D // 128, 128), lambda i: (i, 0, 0)),
                   pl.BlockSpec((tm, E), row)],
        out_shape=[jax.ShapeDtypeStruct((T, D), F32), jax.ShapeDtypeStruct((T, D), F32),
                   jax.ShapeDtypeStruct((T, D // 128, 128), F32),
                   jax.ShapeDtypeStruct((T, E), F32)],
        scratch_shapes=[pltpu.VMEM((tm + CONV_HALO, cwid), F32)],
        compiler_params=pltpu.CompilerParams(
            dimension_semantics=("arbitrary",), vmem_limit_bytes=VMEM_LIMIT),
        name="mix_outproj",
    )(x2, o, hg, hg, conv_w, conv_b, ln_g, ln_b, attn_g, wo_bf, g2, wq_hi, wq_lo)


def _top16(s, pos=None, payload=None):
    if pos is None:
        pos = lax.broadcasted_iota(I32, s.shape, 0).astype(F32)
    vals, outs = [], []
    for _ in range(PEER_TOPK):
        m = jnp.max(s, axis=0, keepdims=True)
        idx = jnp.min(jnp.where(s == m, pos, 1e9), axis=0, keepdims=True)
        sel = pos == idx
        if payload is None:
            outs.append(idx)
        else:
            outs.append(jnp.sum(jnp.where(sel, payload, 0.0), axis=0, keepdims=True))
        s = jnp.where(sel, -jnp.inf, s)
        vals.append(m)
    return jnp.concatenate(vals, axis=0), jnp.concatenate(outs, axis=0)


def _dot3_nt(a, b):
    ah, al = _split_bf16(a)
    bh, bl = _split_bf16(b)
    return _dot_nt(ah, bh) + _dot_nt(al, bh) + _dot_nt(ah, bl)


def _pair_candidates(v1, i1, v2, i2):
    r8 = lax.broadcasted_iota(I32, (8, v1.shape[1]), 0).astype(F32)
    sc, ex, ps = [], [], []

    def add(s1, e1, s2, e2, p, keep=None):
        s = s1 + s2
        sc.append(s if keep is None else jnp.where(keep, s, -jnp.inf))
        ex.append(e1 * float(N_KEYS) + e2)
        ps.append(p if keep is None else jnp.where(keep, p, 2e9))

    for a, half in ((0, 0), (0, 1), (1, 0), (2, 0), (3, 0)):
        b = slice(8 * half, 8 * half + 8)
        add(v1[a:a + 1], i1[a:a + 1], v2[b], i2[b], r8 + float(a * PEER_TOPK + 8 * half))
    for b in range(3):
        add(v1[0:8], i1[0:8], v2[b:b + 1], i2[b:b + 1], r8 * float(PEER_TOPK) + float(b),
            keep=r8 >= 4.0)
    add(v1[8:16], i1[8:16], v2[0:1], i2[0:1], (r8 + 8.0) * float(PEER_TOPK))
    return (jnp.concatenate(sc, axis=0), jnp.concatenate(ex, axis=0),
            jnp.concatenate(ps, axis=0))


def _topk_kernel(qh_ref, k1_ref, k2_ref, idx_ref, gate_ref):
    all_experts, all_gates = [], []
    for h in range(PEER_HEADS):
        base = h * 2 * PEER_HALF
        q1 = qh_ref[:, base:base + PEER_HALF]
        q2 = qh_ref[:, base + PEER_HALF:base + 2 * PEER_HALF]
        s1 = _dot3_nt(k1_ref[h], q1)
        s2 = _dot3_nt(k2_ref[h], q2)
        v1, i1 = _top16(s1)
        v2, i2 = _top16(s2)
        cand, cexp, cpos = _pair_candidates(v1, i1, v2, i2)
        top_s, experts = _top16(cand, pos=cpos, payload=cexp)
        e = jnp.exp(top_s - top_s[0:1, :])
        all_experts.append(experts)
        all_gates.append(e / jnp.sum(e, axis=0, keepdims=True))
    idx_ref[...] = jnp.concatenate(all_experts, axis=0).T.astype(I32) * CHUNKS
    gate_ref[...] = jnp.concatenate(all_gates, axis=0).T


def _topk(qh, k1, k2, *, tm):
    T, E = qh.shape
    out_spec = pl.BlockSpec((tm, SEL), lambda i: (i, 0))
    return pl.pallas_call(
        _topk_kernel,
        grid=(T // tm,),
        in_specs=[
            pl.BlockSpec((tm, E), lambda i: (i, 0)),
            pl.BlockSpec(k1.shape, lambda i: (0, 0, 0)),
            pl.BlockSpec(k2.shape, lambda i: (0, 0, 0)),
        ],
        out_specs=[out_spec, out_spec],
        out_shape=[jax.ShapeDtypeStruct((T, SEL), I32), jax.ShapeDtypeStruct((T, SEL), F32)],
        compiler_params=pltpu.CompilerParams(
            dimension_semantics=("arbitrary",), vmem_limit_bytes=VMEM_LIMIT),
        name="peer_topk",
    )(qh, k1, k2)


CHUNKS = 4
CHUNK_STRIDE = SEL + 8


def _unpack(word):
    lo = lax.bitcast_convert_type(word << 16, F32)
    hi = lax.bitcast_convert_type(word & HI_MASK, F32)
    return lo, hi


def _gather_group(idx_ref, tab_ref, buf_ref, t0, slots, part=0, parts=1):
    ids = [idx_ref.at[0, 0, pl.ds((t0 + j) * SEL, SEL)] for j in range(len(slots))]
    for k in range(part * SEL // parts, (part + 1) * SEL // parts):
        for j, s in enumerate(slots):
            r = pl.multiple_of(ids[j][k], CHUNKS)
            buf_ref[s, pl.ds(k, CHUNKS, stride=CHUNK_STRIDE), :] = tab_ref[pl.ds(r, CHUNKS), :]


def _cols(c):
    return slice(c * 128, (c + 1) * 128)


def _chunk(buf_ref, s, c):
    return _unpack(buf_ref[s, c * CHUNK_STRIDE:c * CHUNK_STRIDE + SEL, :])


def _lane_replicated_sum(a):
    ones = jnp.ones((128, 128), BF16)
    hi, lo = _split_bf16(a)
    return _dot(hi, ones) + _dot(lo, ones)


def _diag_mask():
    return (lax.broadcasted_iota(I32, (SEL, 128), 0) == lax.broadcasted_iota(I32, (SEL, 128), 1))


GROUP = 4


def _grouped_tokens(tb, gather, reduce):
    half = (tuple(range(GROUP)), tuple(range(GROUP, 2 * GROUP)))
    gather(0, half[0], 0, 1)

    def trip(i, carry):
        t = 2 * GROUP * i
        for j in range(GROUP):
            gather(t + GROUP, half[1], j, GROUP)
            reduce(t + j, half[0][j])
        nxt = jnp.minimum(t + 2 * GROUP, tb - GROUP)
        for j in range(GROUP):
            gather(nxt, half[0], j, GROUP)
            reduce(t + GROUP + j, half[1][j])
        return carry

    lax.fori_loop(0, tb // (2 * GROUP), trip, 0)


def _peer_u_kernel(idx_ref, x_ref, gate_ref, tab_ref, w_ref, buf_ref, act_ref):
    tb = x_ref.shape[0]
    diag = _diag_mask()

    def gather(t0, slots, part, parts):
        _gather_group(idx_ref, tab_ref, buf_ref, t0, slots, part, parts)

    def reduce(t, slot):
        total = None
        for c in range(CHUNKS):
            lo, hi = _chunk(buf_ref, slot, c)
            term = lo * x_ref[t, c:c + 1, :] + hi * x_ref[t, c + CHUNKS:c + CHUNKS + 1, :]
            total = term if total is None else total + term
        act_rep = _lane_replicated_sum(total)
        act_ref[pl.ds(t, 1), :] = jnp.sum(jnp.where(diag, act_rep, 0.0), axis=0, keepdims=True)

    _grouped_tokens(tb, gather, reduce)
    act = act_ref[...]
    gelu = 0.5 * act * (1.0 + lax.erf(act * (2.0 ** -0.5)))
    w_ref[...] = gate_ref[...] * gelu


def _gather_scratch():
    return pltpu.VMEM((2 * GROUP, CHUNKS * CHUNK_STRIDE, 128), I32)


def _index_spec(tb):
    return pl.BlockSpec((1, 1, tb * SEL), lambda i: (i, 0, 0), memory_space=pltpu.SMEM)


def _peer_u(idx3, x3, gates, tab, *, tb, tokens):
    return pl.pallas_call(
        _peer_u_kernel,
        grid=(tokens // tb,),
        in_specs=[
            _index_spec(tb),
            pl.BlockSpec((tb, 2 * CHUNKS, 128), lambda i: (i, 0, 0)),
            pl.BlockSpec((tb, SEL), lambda i: (i, 0)),
            pl.BlockSpec(memory_space=pltpu.VMEM),
        ],
        out_specs=pl.BlockSpec((tb, SEL), lambda i: (i, 0)),
        out_shape=jax.ShapeDtypeStruct((tokens, SEL), F32),
        scratch_shapes=[_gather_scratch(), pltpu.VMEM((tb, SEL), F32)],
        compiler_params=pltpu.CompilerParams(
            dimension_semantics=("arbitrary",), vmem_limit_bytes=VMEM_LIMIT),
        name="peer_u",
    )(idx3, x3, gates, tab)


def _peer_v_kernel(idx_ref, w_ref, h_ref, tab_ref, y_ref, buf_ref):
    tb = h_ref.shape[0]
    diag = _diag_mask()

    def gather(t0, slots, part, parts):
        _gather_group(idx_ref, tab_ref, buf_ref, t0, slots, part, parts)

    def reduce(t, slot):
        w_rep = _lane_replicated_sum(jnp.where(diag, w_ref[pl.ds(t, 1), :], 0.0))
        lows, highs = [], []
        for c in range(CHUNKS):
            lo, hi = _chunk(buf_ref, slot, c)
            lows.append(jnp.sum(lo * w_rep, axis=0, keepdims=True))
            highs.append(jnp.sum(hi * w_rep, axis=0, keepdims=True))
        y_ref[pl.ds(t, 1), :] = h_ref[pl.ds(t, 1), :] + jnp.concatenate(lows + highs, axis=-1)

    _grouped_tokens(tb, gather, reduce)


def _peer_v(idx3, w, h2, tab, *, tb, tokens):
    D = h2.shape[1]
    return pl.pallas_call(
        _peer_v_kernel,
        grid=(tokens // tb,),
        in_specs=[
            _index_spec(tb),
            pl.BlockSpec((tb, SEL), lambda i: (i, 0)),
            pl.BlockSpec((tb, D), lambda i: (i, 0)),
            pl.BlockSpec(memory_space=pltpu.VMEM),
        ],
        out_specs=pl.BlockSpec((tb, D), lambda i: (i, 0)),
        out_shape=jax.ShapeDtypeStruct((tokens, D), F32),
        scratch_shapes=[_gather_scratch()],
        compiler_params=pltpu.CompilerParams(
            dimension_semantics=("arbitrary",), vmem_limit_bytes=VMEM_LIMIT),
        name="peer_v",
    )(idx3, w, h2, tab)


SC_CORES = 2
SC_WORKERS = 32
SC_LANES = 16
ROW_WORDS = 512
HALF_WORDS = ROW_WORDS // 2
HALF_VECS = HALF_WORDS // SC_LANES


def _sc_lanes(j):
    return pl.ds(j * SC_LANES, SC_LANES)


def _sc_token_pipeline(ids_hbm, tab_hbm, ids_v, rows_v, sems, *, first, per, load_extra, process):
    base = (lax.axis_index("s") * SC_CORES + lax.axis_index("c")) * per

    def gather(slot, half):
        return pltpu.make_async_copy(tab_hbm.at[ids_v.at[slot, half]], rows_v.at[half],
                                     sems.at[half])

    def load_token(i, slot):
        pltpu.sync_copy(ids_hbm.at[first + base + i], ids_v.at[slot])
        load_extra(base + i, slot)

    load_token(0, 0)
    gather(0, 0).start()

    @pl.loop(0, per)
    def _(i):
        slot = i % 2
        gather(slot, 1).start()
        gather(slot, 0).wait()
        process(base + i, slot, 0)

        @pl.when(i + 1 < per)
        def _():
            load_token(i + 1, 1 - slot)
            gather(1 - slot, 0).start()

        gather(slot, 1).wait()
        process(base + i, slot, 1)


def _peer_u_sc_kernel(ids_hbm, x_hbm, tab_hbm, act_hbm, ids_v, x_v, part_v, act_v, rows_v, sems,
                      *, first, per):
    def load_extra(local, slot):
        pltpu.sync_copy(x_hbm.at[first + local], x_v.at[slot])

    def process(local, slot, half):
        off = half * HALF_VECS
        x_lo = [x_v[slot, _sc_lanes(off + j)] for j in range(HALF_VECS)]
        x_hi = [x_v[slot, _sc_lanes(2 * HALF_VECS + off + j)] for j in range(HALF_VECS)]

        @pl.loop(0, SEL, step=2)
        def _(k0):
            for k in (k0, k0 + 1):
                sums = [None] * 4
                for j in range(HALF_VECS):
                    lo, hi = _unpack(rows_v[half, k, _sc_lanes(j)])
                    term = lo * x_lo[j] + hi * x_hi[j]
                    sums[j % 4] = term if sums[j % 4] is None else sums[j % 4] + term
                part_v[half * SEL + k, :] = (sums[0] + sums[1]) + (sums[2] + sums[3])

        if half == 1:
            lane_ids = lax.broadcasted_iota(I32, (SC_LANES,), 0)
            for kb in range(SEL // SC_LANES):
                rows0 = lane_ids + kb * SC_LANES
                total = None
                for l in range(SC_LANES):
                    col = jnp.full((SC_LANES,), l, I32)
                    both = (plsc.load_gather(part_v, [rows0, col])
                            + plsc.load_gather(part_v, [rows0 + SEL, col]))
                    total = both if total is None else total + both
                act_v[_sc_lanes(kb)] = total
            pltpu.sync_copy(act_v, act_hbm.at[local])

    _sc_token_pipeline(ids_hbm, tab_hbm, ids_v, rows_v, sems, first=first, per=per,
                       load_extra=load_extra, process=process)


def _sc_call(body, out_cols, n, scratch, name):
    return pl.kernel(
        body,
        out_type=jax.ShapeDtypeStruct((n, out_cols), F32),
        mesh=plsc.VectorSubcoreMesh(core_axis_name="c", subcore_axis_name="s"),
        scratch_types=scratch + [pltpu.VMEM((2, SEL, HALF_WORDS), I32),
                                 pltpu.SemaphoreType.DMA((2,))],
        compiler_params=pltpu.CompilerParams(needs_layout_passes=False),
        name=name,
    )


def _peer_u_sc(half_ids, x2, tab_halves, *, first):
    T, D = x2.shape
    n = T - first
    body = functools.partial(_peer_u_sc_kernel, first=first, per=n // SC_WORKERS)
    scratch = [pltpu.VMEM((2, 2, SEL), I32), pltpu.VMEM((2, D), F32),
               pltpu.VMEM((2 * SEL, SC_LANES), F32), pltpu.VMEM((SEL,), F32)]
    return _sc_call(body, SEL, n, scratch, "peer_u_sc")(half_ids, x2, tab_halves)


def _peer_v_sc_kernel(ids_hbm, w_hbm, h_hbm, tab_hbm, y_hbm, ids_v, w_v, h_v, y_v, rows_v, sems,
                      *, first, per):
    def load_extra(local, slot):
        pltpu.sync_copy(w_hbm.at[local], w_v.at[slot])
        pltpu.sync_copy(h_hbm.at[first + local], h_v.at[slot])

    def process(local, slot, half):
        off = half * HALF_VECS
        init = (tuple(h_v[slot, _sc_lanes(off + j)] for j in range(HALF_VECS))
                + tuple(h_v[slot, _sc_lanes(2 * HALF_VECS + off + j)] for j in range(HALF_VECS)))
        slot_vec = jnp.full((SC_LANES,), slot, I32)

        def body(k, acc):
            wk = plsc.load_gather(w_v, [slot_vec, jnp.full((SC_LANES,), k, I32)])
            lows, highs = [], []
            for j in range(HALF_VECS):
                lo, hi = _unpack(rows_v[half, k, _sc_lanes(j)])
                lows.append(acc[j] + wk * lo)
                highs.append(acc[HALF_VECS + j] + wk * hi)
            return tuple(lows + highs)

        acc = lax.fori_loop(0, SEL, body, init)
        for j in range(HALF_VECS):
            y_v[_sc_lanes(off + j)] = acc[j]
            y_v[_sc_lanes(2 * HALF_VECS + off + j)] = acc[HALF_VECS + j]
        if half == 1:
            pltpu.sync_copy(y_v, y_hbm.at[local])

    _sc_token_pipeline(ids_hbm, tab_hbm, ids_v, rows_v, sems, first=first, per=per,
                       load_extra=load_extra, process=process)


def _peer_v_sc(half_ids, w_tail, h2, tab_halves, *, first):
    T, D = h2.shape
    n = T - first
    body = functools.partial(_peer_v_sc_kernel, first=first, per=n // SC_WORKERS)
    scratch = [pltpu.VMEM((2, 2, SEL), I32), pltpu.VMEM((2, SEL), F32),
               pltpu.VMEM((2, D), F32), pltpu.VMEM((D,), F32)]
    return _sc_call(body, D, n, scratch, "peer_v_sc")(half_ids, w_tail, h2, tab_halves)


def _gelu_gate_kernel(act_ref, gate_ref, after_ref, w_ref):
    del after_ref
    act = act_ref[...]
    w_ref[...] = gate_ref[...] * (0.5 * act * (1.0 + lax.erf(act * (2.0 ** -0.5))))


def _gelu_gate(act_tail, gates, after, *, first, tm):
    n = act_tail.shape[0]
    return pl.pallas_call(
        _gelu_gate_kernel,
        grid=(n // tm,),
        in_specs=[pl.BlockSpec((tm, SEL), lambda i: (i, 0)),
                  pl.BlockSpec((tm, SEL), lambda i: (i + first // tm, 0)),
                  pl.BlockSpec(memory_space=pl.ANY)],
        out_specs=pl.BlockSpec((tm, SEL), lambda i: (i, 0)),
        out_shape=jax.ShapeDtypeStruct((n, SEL), F32),
        name="peer_gelu_gate",
    )(act_tail, gates, after)


def _pack_table(tab):
    n, d = tab.shape
    bits = lax.bitcast_convert_type(tab.astype(BF16), jnp.uint16).astype(jnp.uint32)
    word = bits[:, :d // 2] | (bits[:, d // 2:] << 16)
    return lax.bitcast_convert_type(word, I32).reshape(n * 4, 128)


def _layer(x2, p, *, batch, seq, tm_in, tm_mix, tm_topk, tb, sc_tokens):
    T, D = x2.shape
    row = lambda a: a.reshape(1, -1)
    w_in_bf = p["w_in"].astype(BF16)
    gq_t = row(jnp.tile(p["q_norm_g"], ATTN_HEADS))
    gk_t = row(jnp.tile(p["k_norm_g"], ATTN_HEADS))
    qh, kh, vh, hg = _inproj(x2, row(p["norm_mix_g"]), w_in_bf, gq_t, gk_t, tm=tm_in)
    o = _attention(qh, kh, vh, batch=batch, seq=seq)

    wq = p["peer_wq"]
    wq_hi = wq.astype(BF16)
    wq_lo = (wq - wq_hi.astype(F32)).astype(BF16)
    h, xn, xn3, pq = _mix(x2, o, hg, p["conv_w"], row(p["conv_b"]), row(p["conv_ln_g"]),
                     row(p["conv_ln_b"]), row(p["attn_out_g"]), p["w_out"].astype(BF16),
                     row(p["norm_ffn_g"]), wq_hi, wq_lo, seq=seq, tm=tm_mix)

    idx, gates = _topk(pq, p["peer_k1"], p["peer_k2"], tm=tm_topk)
    u_tab = _pack_table(p["peer_u"])
    v_tab = _pack_table(p["peer_v"])
    head_u, head_v = T - sc_tokens[0], T - sc_tokens[1]
    half_row = idx // (CHUNKS // 2)
    half_ids = jnp.stack([half_row, half_row + 1], axis=1)
    idx3 = (half_ids[:, 0, :] * (CHUNKS // 2)).reshape(T // tb, 1, tb * SEL)
    act_tail = _peer_u_sc(half_ids, xn, u_tab.reshape(-1, HALF_WORDS), first=head_u)
    w_head = _peer_u(idx3, xn3, gates, u_tab, tb=tb, tokens=head_u)
    w_tail = _gelu_gate(act_tail, gates, w_head, first=head_u, tm=tm_topk)
    w_sc = jnp.concatenate([w_head[head_v:], w_tail], axis=0)
    y_tail = _peer_v_sc(half_ids, w_sc, h, v_tab.reshape(-1, HALF_WORDS), first=head_v)
    y_head = _peer_v(idx3, w_head, h, v_tab, tb=tb, tokens=head_v)
    return jnp.concatenate([y_head, y_tail], axis=0)


def kernel(x, norm_mix_g, w_in, q_norm_g, k_norm_g, attn_out_g, conv_w, conv_b, conv_ln_g,
           conv_ln_b, w_out, norm_ffn_g, peer_wq, peer_k1, peer_k2, peer_u, peer_v):
    batch, seq, d = x.shape
    stacked = dict(norm_mix_g=norm_mix_g, w_in=w_in, q_norm_g=q_norm_g, k_norm_g=k_norm_g,
                   attn_out_g=attn_out_g, conv_w=conv_w, conv_b=conv_b, conv_ln_g=conv_ln_g,
                   conv_ln_b=conv_ln_b, w_out=w_out, norm_ffn_g=norm_ffn_g, peer_wq=peer_wq,
                   peer_k1=peer_k1, peer_k2=peer_k2, peer_u=peer_u, peer_v=peer_v)
    h = x.reshape(batch * seq, d)
    for l in range(w_in.shape[0]):
        p = {name: a[l] for name, a in stacked.items()}
        h = _layer(h, p, batch=batch, seq=seq, tm_in=256, tm_mix=256,
                   tm_topk=256, tb=64, sc_tokens=(10240, 12288))
    return h.reshape(batch, seq, d)
```

```python
import functools

import jax
import jax.numpy as jnp
from jax import lax
from jax.experimental import pallas as pl
from jax.experimental.pallas import tpu as pltpu
from jax.experimental.pallas import tpu_sc as plsc

F32 = jnp.float32
BF16 = jnp.bfloat16
I32 = jnp.int32

EPS = 1e-6
HEAD_DIM = 64
ATTN_HEADS = 8
ATTN_WIDTH = ATTN_HEADS * HEAD_DIM
CONV_KERNEL = 31
CONV_HALO = 32
PEER_HEADS = 8
PEER_HALF = 64
N_KEYS = 128
PEER_TOPK = 16
SEL = PEER_HEADS * PEER_TOPK
EXP_UNDERFLOW = -88.0
HI_MASK = -65536

VMEM_LIMIT = 56 * 1024 * 1024


def _split_bf16(a):
    hi = a.astype(BF16)
    lo = (a - hi.astype(F32)).astype(BF16)
    return hi, lo


def _dot(a, b):
    return jnp.dot(a, b, preferred_element_type=F32)


def _dot_nt(a, b):
    return lax.dot_general(a, b, (((1,), (1,)), ((), ())), preferred_element_type=F32)


def _sigmoid(x):
    return 1.0 / (1.0 + jnp.exp(-x))


def _inproj_kernel(x_ref, g_ref, w_ref, gq_ref, gk_ref, q_ref, k_ref, v_ref, hg_ref):
    x = x_ref[...]
    ms = jnp.mean(x * x, axis=-1, keepdims=True)
    xn = (x * lax.rsqrt(ms + EPS) * g_ref[...]).astype(BF16)
    proj = _dot(xn, w_ref[...])

    r = lax.broadcasted_iota(I32, (ATTN_WIDTH, ATTN_WIDTH), 0) // HEAD_DIM
    c = lax.broadcasted_iota(I32, (ATTN_WIDTH, ATTN_WIDTH), 1) // HEAD_DIM
    same_head = (r == c).astype(BF16)

    def head_norm(t, g):
        hi, lo = _split_bf16(t * t)
        msq = (_dot(hi, same_head) + _dot(lo, same_head)) * (1.0 / HEAD_DIM)
        return t * lax.rsqrt(msq + EPS) * g

    q = head_norm(proj[:, :ATTN_WIDTH], gq_ref[...]) * (HEAD_DIM ** -0.5)
    k = head_norm(proj[:, ATTN_WIDTH:2 * ATTN_WIDTH], gk_ref[...])
    v = proj[:, 2 * ATTN_WIDTH:3 * ATTN_WIDTH]
    for h in range(ATTN_HEADS):
        sl = slice(h * HEAD_DIM, (h + 1) * HEAD_DIM)
        q_ref[h] = q[:, sl].astype(BF16)
        k_ref[h] = k[:, sl].astype(BF16)
        v_ref[h] = v[:, sl].astype(BF16)
    cw = (proj.shape[1] - 3 * ATTN_WIDTH) // 2
    a = proj[:, 3 * ATTN_WIDTH:3 * ATTN_WIDTH + cw]
    gate = proj[:, 3 * ATTN_WIDTH + cw:]
    hg_ref[...] = a * _sigmoid(gate)


def _inproj(x2, g, w_bf, gq_t, gk_t, *, tm):
    T, D = x2.shape
    E = w_bf.shape[1]
    cw = (E - 3 * ATTN_WIDTH) // 2
    head_shape = jax.ShapeDtypeStruct((ATTN_HEADS, T, HEAD_DIM), BF16)
    head_spec = pl.BlockSpec((ATTN_HEADS, tm, HEAD_DIM), lambda i: (0, i, 0))
    return pl.pallas_call(
        _inproj_kernel,
        grid=(T // tm,),
        in_specs=[
            pl.BlockSpec((tm, D), lambda i: (i, 0)),
            pl.BlockSpec((1, D), lambda i: (0, 0)),
            pl.BlockSpec((D, E), lambda i: (0, 0)),
            pl.BlockSpec((1, ATTN_WIDTH), lambda i: (0, 0)),
            pl.BlockSpec((1, ATTN_WIDTH), lambda i: (0, 0)),
        ],
        out_specs=[head_spec, head_spec, head_spec,
                   pl.BlockSpec((tm, cw), lambda i: (i, 0))],
        out_shape=[head_shape, head_shape, head_shape,
                   jax.ShapeDtypeStruct((T, cw), F32)],
        compiler_params=pltpu.CompilerParams(
            dimension_semantics=("arbitrary",), vmem_limit_bytes=VMEM_LIMIT),
        name="inproj",
    )(x2, g, w_bf, gq_t, gk_t)


def _attn_span(q, k, v, offset, carry, suffix):
    tw = suffix.shape[0]
    z = _dot_nt(q, k)
    sp = jnp.maximum(z, 0.0) + jnp.log(1.0 + jnp.exp(-jnp.abs(z)))
    col_minus_row = (lax.broadcasted_iota(I32, z.shape, 1)
                     - lax.broadcasted_iota(I32, z.shape, 0))
    mask = col_minus_row < offset
    log_keep = jnp.where(mask, -sp, 0.0)
    later = []
    for s in reversed(range(z.shape[1] // tw)):
        lk = log_keep[:, s * tw:(s + 1) * tw]
        hi, lo = _split_bf16(lk)
        later.append(carry + (_dot(hi, suffix) + _dot(lo, suffix)))
        carry = carry + jnp.sum(lk, axis=-1, keepdims=True)
    later = jnp.concatenate(later[::-1], axis=-1)
    att = jnp.where(mask, jnp.exp(z - sp + later), 0.0)
    return carry, _dot(att.astype(BF16), v)


def _attn_kernel(q_ref, k_ref, v_ref, o_ref, carry_ref, acc_ref, *, rows, span, tw):
    g = pl.program_id(2)
    hp = q_ref.shape[0]
    suffix = (lax.broadcasted_iota(I32, (tw, tw), 0)
              > lax.broadcasted_iota(I32, (tw, tw), 1)).astype(BF16)

    start = pl.multiple_of(jnp.maximum(g * rows - (span - rows), 0), rows)
    cmax = None
    for hh in range(hp):
        carry, acc = _attn_span(q_ref[hh], k_ref[hh, pl.ds(start, span), :],
                                v_ref[hh, pl.ds(start, span), :], g * rows - start,
                                jnp.zeros((rows, 1), F32), suffix)
        carry_ref[hh] = carry
        acc_ref[hh] = acc
        cmax = carry if cmax is None else jnp.maximum(cmax, carry)

    @pl.when(jnp.max(cmax) > EXP_UNDERFLOW)
    def _():
        for hh in range(hp):

            def cond(st):
                j, carry, _ = st
                return jnp.logical_and(j >= 0, jnp.max(carry) > EXP_UNDERFLOW)

            def body(st, hh=hh):
                j, carry, acc = st
                ks = pl.multiple_of(j * tw, tw)
                carry, out = _attn_span(q_ref[hh], k_ref[hh, pl.ds(ks, tw), :],
                                        v_ref[hh, pl.ds(ks, tw), :], rows + tw, carry, suffix)
                return j - 1, carry, acc + out

            init = (start // tw - 1, carry_ref[hh], acc_ref[hh])
            acc_ref[hh] = lax.while_loop(cond, body, init)[2]

    o_ref[...] = jnp.concatenate([acc_ref[hh] for hh in range(hp)], axis=-1)


def _attention(qh, kh, vh, *, batch, seq, rows=256, span=512, tw=256, heads_per_step=2):
    H, T, hd = qh.shape
    ng = seq // rows
    hp = heads_per_step
    return pl.pallas_call(
        functools.partial(_attn_kernel, rows=rows, span=span, tw=tw),
        grid=(H // hp, batch, ng),
        in_specs=[
            pl.BlockSpec((hp, rows, hd), lambda h, b, i: (h, b * ng + i, 0)),
            pl.BlockSpec((hp, seq, hd), lambda h, b, i: (h, b, 0)),
            pl.BlockSpec((hp, seq, hd), lambda h, b, i: (h, b, 0)),
        ],
        out_specs=pl.BlockSpec((rows, hp * hd), lambda h, b, i: (b * ng + i, h)),
        out_shape=jax.ShapeDtypeStruct((T, H * hd), F32),
        scratch_shapes=[pltpu.VMEM((hp, rows, 1), F32),
                        pltpu.VMEM((hp, rows, hd), F32)],
        compiler_params=pltpu.CompilerParams(
            dimension_semantics=("arbitrary", "arbitrary", "arbitrary"),
            vmem_limit_bytes=VMEM_LIMIT),
        name="sb_attention",
    )(qh, kh, vh)


def _mix_kernel(x_ref, o_ref, hg_ref, hgp_ref, cw_ref, cb_ref, lg_ref, lb_ref, ag_ref,
                wo_ref, g2_ref, wqh_ref, wql_ref, h_ref, xn_ref, xn3_ref, qh_ref, ext_ref,
                *, blocks_per_seq):
    tm = x_ref.shape[0]
    first = (pl.program_id(0) % blocks_per_seq) == 0
    ext_ref[0:CONV_HALO, :] = jnp.where(first, 0.0, hgp_ref[...])
    ext_ref[CONV_HALO:, :] = hg_ref[...]
    conv = jnp.zeros(hg_ref.shape, F32)
    for j in range(CONV_KERNEL):
        off = CONV_HALO - (CONV_KERNEL - 1) + j
        conv = conv + cw_ref[j:j + 1, :] * ext_ref[pl.ds(off, tm), :]
    conv = conv + cb_ref[...]
    mu = jnp.mean(conv, axis=-1, keepdims=True)
    xc = conv - mu
    var = jnp.mean(xc * xc, axis=-1, keepdims=True)
    y = xc * lax.rsqrt(var + EPS) * lg_ref[...] + lb_ref[...]
    o_conv = y * _sigmoid(y)

    o = o_ref[...]
    o_attn = o * lax.rsqrt(jnp.mean(o * o, axis=-1, keepdims=True) + EPS) * ag_ref[...]
    mixed = jnp.concatenate([o_attn, o_conv], axis=-1).astype(BF16)
    h = x_ref[...] + _dot(mixed, wo_ref[...])
    h_ref[...] = h

    xn = h * lax.rsqrt(jnp.mean(h * h, axis=-1, keepdims=True) + EPS) * g2_ref[...]
    xn_ref[...] = xn
    for c in range(xn3_ref.shape[1]):
        xn3_ref[:, c, :] = xn[:, c * 128:(c + 1) * 128]
    hi, lo = _split_bf16(xn)
    wqh = wqh_ref[...]
    qh_ref[...] = _dot(hi, wqh) + _dot(lo, wqh) + _dot(hi, wql_ref[...])


def _mix(x2, o, hg, conv_w, conv_b, ln_g, ln_b, attn_g, wo_bf, g2, wq_hi, wq_lo, *, seq, tm):
    T, D = x2.shape
    cwid = hg.shape[1]
    E = wq_hi.shape[1]
    halo_per_block = tm // CONV_HALO
    row = lambda i: (i, 0)
    const = lambda i: (0, 0)
    return pl.pallas_call(
        functools.partial(_mix_kernel, blocks_per_seq=seq // tm),
        grid=(T // tm,),
        in_specs=[
            pl.BlockSpec((tm, D), row),
            pl.BlockSpec((tm, o.shape[1]), row),
            pl.BlockSpec((tm, cwid), row),
            pl.BlockSpec((CONV_HALO, cwid),
                         lambda i: (jnp.maximum(i * halo_per_block - 1, 0), 0)),
            pl.BlockSpec((CONV_KERNEL, cwid), const),
            pl.BlockSpec((1, cwid), const),
            pl.BlockSpec((1, cwid), const),
            pl.BlockSpec((1, cwid), const),
            pl.BlockSpec((1, o.shape[1]), const),
            pl.BlockSpec(wo_bf.shape, const),
            pl.BlockSpec((1, D), const),
            pl.BlockSpec(wq_hi.shape, const),
            pl.BlockSpec(wq_lo.shape, const),
        ],
        out_specs=[pl.BlockSpec((tm, D), row), pl.BlockSpec((tm, D), row),
                   pl.BlockSpec((tm, D // 128, 128), lambda i: (i, 0, 0)),
                   pl.BlockSpec((tm, E), row)],
        out_shape=[jax.ShapeDtypeStruct((T, D), F32), jax.ShapeDtypeStruct((T, D), F32),
                   jax.ShapeDtypeStruct((T, D // 128, 128), F32),
                   jax.ShapeDtypeStruct((T, E), F32)],
        scratch_shapes=[pltpu.VMEM((tm + CONV_HALO, cwid), F32)],
        compiler_params=pltpu.CompilerParams(
            dimension_semantics=("arbitrary",), vmem_limit_bytes=VMEM_LIMIT),
        name="mix_outproj",
    )(x2, o, hg, hg, conv_w, conv_b, ln_g, ln_b, attn_g, wo_bf, g2, wq_hi, wq_lo)


def _top16(s, pos=None, payload=None):
    if pos is None:
        pos = lax.broadcasted_iota(I32, s.shape, 0).astype(F32)
    vals, outs = [], []
    for _ in range(PEER_TOPK):
        m = jnp.max(s, axis=0, keepdims=True)
        idx = jnp.min(jnp.where(s == m, pos, 1e9), axis=0, keepdims=True)
        sel = pos == idx
        if payload is None:
            outs.append(idx)
        else:
            outs.append(jnp.sum(jnp.where(sel, payload, 0.0), axis=0, keepdims=True))
        s = jnp.where(sel, -jnp.inf, s)
        vals.append(m)
    return jnp.concatenate(vals, axis=0), jnp.concatenate(outs, axis=0)


def _dot3_nt(a, b):
    ah, al = _split_bf16(a)
    bh, bl = _split_bf16(b)
    return _dot_nt(ah, bh) + _dot_nt(al, bh) + _dot_nt(ah, bl)


def _pair_candidates(v1, i1, v2, i2):
    r8 = lax.broadcasted_iota(I32, (8, v1.shape[1]), 0).astype(F32)
    sc, ex, ps = [], [], []

    def add(s1, e1, s2, e2, p, keep=None):
        s = s1 + s2
        sc.append(s if keep is None else jnp.where(keep, s, -jnp.inf))
        ex.append(e1 * float(N_KEYS) + e2)
        ps.append(p if keep is None else jnp.where(keep, p, 2e9))

    for a, half in ((0, 0), (0, 1), (1, 0), (2, 0), (3, 0)):
        b = slice(8 * half, 8 * half + 8)
        add(v1[a:a + 1], i1[a:a + 1], v2[b], i2[b], r8 + float(a * PEER_TOPK + 8 * half))
    for b in range(3):
        add(v1[0:8], i1[0:8], v2[b:b + 1], i2[b:b + 1], r8 * float(PEER_TOPK) + float(b),
            keep=r8 >= 4.0)
    add(v1[8:16], i1[8:16], v2[0:1], i2[0:1], (r8 + 8.0) * float(PEER_TOPK))
    return (jnp.concatenate(sc, axis=0), jnp.concatenate(ex, axis=0),
            jnp.concatenate(ps, axis=0))


def _topk_kernel(qh_ref, k1_ref, k2_ref, idx_ref, gate_ref):
    all_experts, all_gates = [], []
    for h in range(PEER_HEADS):
        base = h * 2 * PEER_HALF
        q1 = qh_ref[:, base:base + PEER_HALF]
        q2 = qh_ref[:, base + PEER_HALF:base + 2 * PEER_HALF]
        s1 = _dot3_nt(k1_ref[h], q1)
        s2 = _dot3_nt(k2_ref[h], q2)
        v1, i1 = _top16(s1)
        v2, i2 = _top16(s2)
        cand, cexp, cpos = _pair_candidates(v1, i1, v2, i2)
        top_s, experts = _top16(cand, pos=cpos, payload=cexp)
        e = jnp.exp(top_s - top_s[0:1, :])
        all_experts.append(experts)
        all_gates.append(e / jnp.sum(e, axis=0, keepdims=True))
    idx_ref[...] = jnp.concatenate(all_experts, axis=0).T.astype(I32) * CHUNKS
    gate_ref[...] = jnp.concatenate(all_gates, axis=0).T


def _topk(qh, k1, k2, *, tm):
    T, E = qh.shape
    out_spec = pl.BlockSpec((tm, SEL), lambda i: (i, 0))
    return pl.pallas_call(
        _topk_kernel,
        grid=(T // tm,),
        in_specs=[
            pl.BlockSpec((tm, E), lambda i: (i, 0)),
            pl.BlockSpec(k1.shape, lambda i: (0, 0, 0)),
            pl.BlockSpec(k2.shape, lambda i: (0, 0, 0)),
        ],
        out_specs=[out_spec, out_spec],
        out_shape=[jax.ShapeDtypeStruct((T, SEL), I32), jax.ShapeDtypeStruct((T, SEL), F32)],
        compiler_params=pltpu.CompilerParams(
            dimension_semantics=("arbitrary",), vmem_limit_bytes=VMEM_LIMIT),
        name="peer_topk",
    )(qh, k1, k2)


CHUNKS = 4
CHUNK_STRIDE = SEL + 8


def _unpack(word):
    lo = lax.bitcast_convert_type(word << 16, F32)
    hi = lax.bitcast_convert_type(word & HI_MASK, F32)
    return lo, hi


def _gather_group(idx_ref, tab_ref, buf_ref, t0, slots, part=0, parts=1):
    ids = [idx_ref.at[0, 0, pl.ds((t0 + j) * SEL, SEL)] for j in range(len(slots))]
    for k in range(part * SEL // parts, (part + 1) * SEL // parts):
        for j, s in enumerate(slots):
            r = pl.multiple_of(ids[j][k], CHUNKS)
            buf_ref[s, pl.ds(k, CHUNKS, stride=CHUNK_STRIDE), :] = tab_ref[pl.ds(r, CHUNKS), :]


def _cols(c):
    return slice(c * 128, (c + 1) * 128)


def _chunk(buf_ref, s, c):
    return _unpack(buf_ref[s, c * CHUNK_STRIDE:c * CHUNK_STRIDE + SEL, :])


def _lane_replicated_sum(a):
    ones = jnp.ones((128, 128), BF16)
    hi, lo = _split_bf16(a)
    return _dot(hi, ones) + _dot(lo, ones)


def _diag_mask():
    return (lax.broadcasted_iota(I32, (SEL, 128), 0) == lax.broadcasted_iota(I32, (SEL, 128), 1))


GROUP = 4


def _grouped_tokens(tb, gather, reduce):
    half = (tuple(range(GROUP)), tuple(range(GROUP, 2 * GROUP)))
    gather(0, half[0], 0, 1)

    def trip(i, carry):
        t = 2 * GROUP * i
        for j in range(GROUP):
            gather(t + GROUP, half[1], j, GROUP)
            reduce(t + j, half[0][j])
        nxt = jnp.minimum(t + 2 * GROUP, tb - GROUP)
        for j in range(GROUP):
            gather(nxt, half[0], j, GROUP)
            reduce(t + GROUP + j, half[1][j])
        return carry

    lax.fori_loop(0, tb // (2 * GROUP), trip, 0)


def _peer_u_kernel(idx_ref, x_ref, gate_ref, tab_ref, w_ref, buf_ref, act_ref):
    tb = x_ref.shape[0]
    diag = _diag_mask()

    def gather(t0, slots, part, parts):
        _gather_group(idx_ref, tab_ref, buf_ref, t0, slots, part, parts)

    def reduce(t, slot):
        total = None
        for c in range(CHUNKS):
            lo, hi = _chunk(buf_ref, slot, c)
            term = lo * x_ref[t, c:c + 1, :] + hi * x_ref[t, c + CHUNKS:c + CHUNKS + 1, :]
            total = term if total is None else total + term
        act_rep = _lane_replicated_sum(total)
        act_ref[pl.ds(t, 1), :] = jnp.sum(jnp.where(diag, act_rep, 0.0), axis=0, keepdims=True)

    _grouped_tokens(tb, gather, reduce)
    act = act_ref[...]
    gelu = 0.5 * act * (1.0 + lax.erf(act * (2.0 ** -0.5)))
    w_ref[...] = gate_ref[...] * gelu


def _gather_scratch():
    return pltpu.VMEM((2 * GROUP, CHUNKS * CHUNK_STRIDE, 128), I32)


def _index_spec(tb):
    return pl.BlockSpec((1, 1, tb * SEL), lambda i: (i, 0, 0), memory_space=pltpu.SMEM)


def _peer_u(idx3, x3, gates, tab, *, tb, tokens):
    return pl.pallas_call(
        _peer_u_kernel,
        grid=(tokens // tb,),
        in_specs=[
            _index_spec(tb),
            pl.BlockSpec((tb, 2 * CHUNKS, 128), lambda i: (i, 0, 0)),
            pl.BlockSpec((tb, SEL), lambda i: (i, 0)),
            pl.BlockSpec(memory_space=pltpu.VMEM),
        ],
        out_specs=pl.BlockSpec((tb, SEL), lambda i: (i, 0)),
        out_shape=jax.ShapeDtypeStruct((tokens, SEL), F32),
        scratch_shapes=[_gather_scratch(), pltpu.VMEM((tb, SEL), F32)],
        compiler_params=pltpu.CompilerParams(
            dimension_semantics=("arbitrary",), vmem_limit_bytes=VMEM_LIMIT),
        name="peer_u",
    )(idx3, x3, gates, tab)


def _peer_v_kernel(idx_ref, w_ref, h_ref, tab_ref, y_ref, buf_ref):
    tb = h_ref.shape[0]
    diag = _diag_mask()

    def gather(t0, slots, part, parts):
        _gather_group(idx_ref, tab_ref, buf_ref, t0, slots, part, parts)

    def reduce(t, slot):
        w_rep = _lane_replicated_sum(jnp.where(diag, w_ref[pl.ds(t, 1), :], 0.0))
        lows, highs = [], []
        for c in range(CHUNKS):
            lo, hi = _chunk(buf_ref, slot, c)
            lows.append(jnp.sum(lo * w_rep, axis=0, keepdims=True))
            highs.append(jnp.sum(hi * w_rep, axis=0, keepdims=True))
        y_ref[pl.ds(t, 1), :] = h_ref[pl.ds(t, 1), :] + jnp.concatenate(lows + highs, axis=-1)

    _grouped_tokens(tb, gather, reduce)


def _peer_v(idx3, w, h2, tab, *, tb, tokens):
    D = h2.shape[1]
    return pl.pallas_call(
        _peer_v_kernel,
        grid=(tokens // tb,),
        in_specs=[
            _index_spec(tb),
            pl.BlockSpec((tb, SEL), lambda i: (i, 0)),
            pl.BlockSpec((tb, D), lambda i: (i, 0)),
            pl.BlockSpec(memory_space=pltpu.VMEM),
        ],
        out_specs=pl.BlockSpec((tb, D), lambda i: (i, 0)),
        out_shape=jax.ShapeDtypeStruct((tokens, D), F32),
        scratch_shapes=[_gather_scratch()],
        compiler_params=pltpu.CompilerParams(
            dimension_semantics=("arbitrary",), vmem_limit_bytes=VMEM_LIMIT),
        name="peer_v",
    )(idx3, w, h2, tab)


SC_CORES = 2
SC_WORKERS = 32
SC_LANES = 16
ROW_WORDS = 512
HALF_WORDS = ROW_WORDS // 2
HALF_VECS = HALF_WORDS // SC_LANES


def _sc_lanes(j):
    return pl.ds(j * SC_LANES, SC_LANES)


def _sc_unpack(word):
    pair = plsc.bitcast(word, BF16)
    return plsc.unpack(pair, format=plsc.PackFormat.INTERLEAVED, preferred_element_type=F32)


def _sc_token_pipeline(ids_hbm, tab_hbm, ids_v, rows_v, sems, *, first, per, load_extra, process):
    base = (lax.axis_index("s") * SC_CORES + lax.axis_index("c")) * per

    def gather(slot, half):
        return pltpu.make_async_copy(tab_hbm.at[ids_v.at[slot, half]], rows_v.at[half],
                                     sems.at[half])

    def load_token(i, slot):
        pltpu.sync_copy(ids_hbm.at[first + base + i], ids_v.at[slot])
        load_extra(base + i, slot)

    load_token(0, 0)
    gather(0, 0).start()

    @pl.loop(0, per)
    def _(i):
        slot = i % 2
        gather(slot, 1).start()
        gather(slot, 0).wait()
        process(base + i, slot, 0)

        @pl.when(i + 1 < per)
        def _():
            load_token(i + 1, 1 - slot)
            gather(1 - slot, 0).start()

        gather(slot, 1).wait()
        process(base + i, slot, 1)


def _peer_u_sc_kernel(ids_hbm, x_hbm, tab_hbm, act_hbm, ids_v, x_v, part_v, act_v, rows_v, sems,
                      *, first, per):
    def load_extra(local, slot):
        pltpu.sync_copy(x_hbm.at[first + local], x_v.at[slot])

    def process(local, slot, half):
        off = half * HALF_VECS
        x_lo = [x_v[slot, _sc_lanes(off + j)] for j in range(HALF_VECS)]
        x_hi = [x_v[slot, _sc_lanes(2 * HALF_VECS + off + j)] for j in range(HALF_VECS)]

        @pl.loop(0, SEL, step=2)
        def _(k0):
            for k in (k0, k0 + 1):
                sums = [None] * 4
                for j in range(HALF_VECS):
                    lo, hi = _sc_unpack(rows_v[half, k, _sc_lanes(j)])
                    term = lo * x_lo[j] + hi * x_hi[j]
                    sums[j % 4] = term if sums[j % 4] is None else sums[j % 4] + term
                part_v[half * SEL + k, :] = (sums[0] + sums[1]) + (sums[2] + sums[3])

        if half == 1:
            lane_ids = lax.broadcasted_iota(I32, (SC_LANES,), 0)
            for kb in range(SEL // SC_LANES):
                rows0 = lane_ids + kb * SC_LANES
                total = None
                for l in range(SC_LANES):
                    col = jnp.full((SC_LANES,), l, I32)
                    both = (plsc.load_gather(part_v, [rows0, col])
                            + plsc.load_gather(part_v, [rows0 + SEL, col]))
                    total = both if total is None else total + both
                act_v[_sc_lanes(kb)] = total
            pltpu.sync_copy(act_v, act_hbm.at[local])

    _sc_token_pipeline(ids_hbm, tab_hbm, ids_v, rows_v, sems, first=first, per=per,
                       load_extra=load_extra, process=process)


def _sc_call(body, out_cols, n, scratch, name):
    return pl.kernel(
        body,
        out_type=jax.ShapeDtypeStruct((n, out_cols), F32),
        mesh=plsc.VectorSubcoreMesh(core_axis_name="c", subcore_axis_name="s"),
        scratch_types=scratch + [pltpu.VMEM((2, SEL, HALF_WORDS), I32),
                                 pltpu.SemaphoreType.DMA((2,))],
        compiler_params=pltpu.CompilerParams(needs_layout_passes=False),
        name=name,
    )


def _peer_u_sc(half_ids, x2, tab_halves, *, first):
    T, D = x2.shape
    n = T - first
    body = functools.partial(_peer_u_sc_kernel, first=first, per=n // SC_WORKERS)
    scratch = [pltpu.VMEM((2, 2, SEL), I32), pltpu.VMEM((2, D), F32),
               pltpu.VMEM((2 * SEL, SC_LANES), F32), pltpu.VMEM((SEL,), F32)]
    return _sc_call(body, SEL, n, scratch, "peer_u_sc")(half_ids, x2, tab_halves)


def _peer_v_sc_kernel(ids_hbm, w_hbm, h_hbm, tab_hbm, y_hbm, ids_v, w_v, h_v, y_v, rows_v, sems,
                      *, first, per):
    def load_extra(local, slot):
        pltpu.sync_copy(w_hbm.at[local], w_v.at[slot])
        pltpu.sync_copy(h_hbm.at[first + local], h_v.at[slot])

    def process(local, slot, half):
        off = half * HALF_VECS
        init = (tuple(h_v[slot, _sc_lanes(off + j)] for j in range(HALF_VECS))
                + tuple(h_v[slot, _sc_lanes(2 * HALF_VECS + off + j)] for j in range(HALF_VECS)))
        slot_vec = jnp.full((SC_LANES,), slot, I32)

        def body(k, acc):
            wk = plsc.load_gather(w_v, [slot_vec, jnp.full((SC_LANES,), k, I32)])
            lows, highs = [], []
            for j in range(HALF_VECS):
                lo, hi = _sc_unpack(rows_v[half, k, _sc_lanes(j)])
                lows.append(acc[j] + wk * lo)
                highs.append(acc[HALF_VECS + j] + wk * hi)
            return tuple(lows + highs)

        acc = lax.fori_loop(0, SEL, body, init)
        for j in range(HALF_VECS):
            y_v[_sc_lanes(off + j)] = acc[j]
            y_v[_sc_lanes(2 * HALF_VECS + off + j)] = acc[HALF_VECS + j]
        if half == 1:
            pltpu.sync_copy(y_v, y_hbm.at[local])

    _sc_token_pipeline(ids_hbm, tab_hbm, ids_v, rows_v, sems, first=first, per=per,
                       load_extra=load_extra, process=process)


def _peer_v_sc(half_ids, w_tail, h2, tab_halves, *, first):
    T, D = h2.shape
    n = T - first
    body = functools.partial(_peer_v_sc_kernel, first=first, per=n // SC_WORKERS)
    scratch = [pltpu.VMEM((2, 2, SEL), I32), pltpu.VMEM((2, SEL), F32),
               pltpu.VMEM((2, D), F32), pltpu.VMEM((D,), F32)]
    return _sc_call(body, D, n, scratch, "peer_v_sc")(half_ids, w_tail, h2, tab_halves)


def _gelu_gate_kernel(act_ref, gate_ref, after_ref, w_ref):
    del after_ref
    act = act_ref[...]
    w_ref[...] = gate_ref[...] * (0.5 * act * (1.0 + lax.erf(act * (2.0 ** -0.5))))


def _gelu_gate(act_tail, gates, after, *, first, tm):
    n = act_tail.shape[0]
    return pl.pallas_call(
        _gelu_gate_kernel,
        grid=(n // tm,),
        in_specs=[pl.BlockSpec((tm, SEL), lambda i: (i, 0)),
                  pl.BlockSpec((tm, SEL), lambda i: (i + first // tm, 0)),
                  pl.BlockSpec(memory_space=pl.ANY)],
        out_specs=pl.BlockSpec((tm, SEL), lambda i: (i, 0)),
        out_shape=jax.ShapeDtypeStruct((n, SEL), F32),
        name="peer_gelu_gate",
    )(act_tail, gates, after)


def _pack_table(tab):
    n, d = tab.shape
    bits = lax.bitcast_convert_type(tab.astype(BF16), jnp.uint16).astype(jnp.uint32)
    word = bits[:, :d // 2] | (bits[:, d // 2:] << 16)
    return lax.bitcast_convert_type(word, I32).reshape(n * 4, 128)


SHARE_UNITS = 32
SC_SHARE_U = 10
SC_SHARE_V = 13


def _layer(x2, p, *, batch, seq, tm_in, tm_mix, tm_topk, tb):
    T, D = x2.shape
    row = lambda a: a.reshape(1, -1)
    w_in_bf = p["w_in"].astype(BF16)
    gq_t = row(jnp.tile(p["q_norm_g"], ATTN_HEADS))
    gk_t = row(jnp.tile(p["k_norm_g"], ATTN_HEADS))
    qh, kh, vh, hg = _inproj(x2, row(p["norm_mix_g"]), w_in_bf, gq_t, gk_t, tm=tm_in)
    o = _attention(qh, kh, vh, batch=batch, seq=seq)

    wq = p["peer_wq"]
    wq_hi = wq.astype(BF16)
    wq_lo = (wq - wq_hi.astype(F32)).astype(BF16)
    h, xn, xn3, pq = _mix(x2, o, hg, p["conv_w"], row(p["conv_b"]), row(p["conv_ln_g"]),
                     row(p["conv_ln_b"]), row(p["attn_out_g"]), p["w_out"].astype(BF16),
                     row(p["norm_ffn_g"]), wq_hi, wq_lo, seq=seq, tm=tm_mix)

    idx, gates = _topk(pq, p["peer_k1"], p["peer_k2"], tm=tm_topk)
    u_tab = _pack_table(p["peer_u"])
    v_tab = _pack_table(p["peer_v"])
    head_u = T - T * SC_SHARE_U // SHARE_UNITS
    head_v = T - T * SC_SHARE_V // SHARE_UNITS
    assert head_v <= head_u
    for head in (head_u, head_v):
        assert head % tb == 0 and head % tm_topk == 0 and (T - head) % SC_WORKERS == 0
    half_row = idx // (CHUNKS // 2)
    half_ids = jnp.stack([half_row, half_row + 1], axis=1)
    idx3 = (half_ids[:, 0, :] * (CHUNKS // 2)).reshape(T // tb, 1, tb * SEL)
    act_tail = _peer_u_sc(half_ids, xn, u_tab.reshape(-1, HALF_WORDS), first=head_u)
    w_head = _peer_u(idx3, xn3, gates, u_tab, tb=tb, tokens=head_u)
    w_tail = _gelu_gate(act_tail, gates, w_head, first=head_u, tm=tm_topk)
    w_sc = jnp.concatenate([w_head[head_v:], w_tail], axis=0)
    y_tail = _peer_v_sc(half_ids, w_sc, h, v_tab.reshape(-1, HALF_WORDS), first=head_v)
    y_head = _peer_v(idx3, w_head, h, v_tab, tb=tb, tokens=head_v)
    return jnp.concatenate([y_head, y_tail], axis=0)


def kernel(x, norm_mix_g, w_in, q_norm_g, k_norm_g, attn_out_g, conv_w, conv_b, conv_ln_g,
           conv_ln_b, w_out, norm_ffn_g, peer_wq, peer_k1, peer_k2, peer_u, peer_v):
    batch, seq, d = x.shape
    stacked = dict(norm_mix_g=norm_mix_g, w_in=w_in, q_norm_g=q_norm_g, k_norm_g=k_norm_g,
                   attn_out_g=attn_out_g, conv_w=conv_w, conv_b=conv_b, conv_ln_g=conv_ln_g,
                   conv_ln_b=conv_ln_b, w_out=w_out, norm_ffn_g=norm_ffn_g, peer_wq=peer_wq,
                   peer_k1=peer_k1, peer_k2=peer_k2, peer_u=peer_u, peer_v=peer_v)
    h = x.reshape(batch * seq, d)
    for l in range(w_in.shape[0]):
        p = {name: a[l] for name, a in stacked.items()}
        h = _layer(h, p, batch=batch, seq=seq, tm_in=256, tm_mix=256,
                   tm_topk=256, tb=64)
    return h.reshape(batch, seq, d)
```

```python
import functools

import jax
import jax.numpy as jnp
from jax import lax
from jax.experimental import pallas as pl
from jax.experimental.pallas import tpu as pltpu
from jax.experimental.pallas import tpu_sc as plsc

F32 = jnp.float32
BF16 = jnp.bfloat16
I32 = jnp.int32

EPS = 1e-6
HEAD_DIM = 64
ATTN_HEADS = 8
ATTN_WIDTH = ATTN_HEADS * HEAD_DIM
CONV_KERNEL = 31
CONV_HALO = 32
PEER_HEADS = 8
PEER_HALF = 64
N_KEYS = 128
PEER_TOPK = 16
SEL = PEER_HEADS * PEER_TOPK
EXP_UNDERFLOW = -88.0
HI_MASK = -65536

VMEM_LIMIT = 56 * 1024 * 1024


def _split_bf16(a):
    hi = a.astype(BF16)
    lo = (a - hi.astype(F32)).astype(BF16)
    return hi, lo


def _dot(a, b):
    return jnp.dot(a, b, preferred_element_type=F32)


def _dot_nt(a, b):
    return lax.dot_general(a, b, (((1,), (1,)), ((), ())), preferred_element_type=F32)


def _sigmoid(x):
    return 1.0 / (1.0 + jnp.exp(-x))


def _inproj_kernel(x_ref, g_ref, w_ref, gq_ref, gk_ref, q_ref, k_ref, v_ref, hg_ref):
    x = x_ref[...]
    ms = jnp.mean(x * x, axis=-1, keepdims=True)
    xn = (x * lax.rsqrt(ms + EPS) * g_ref[...]).astype(BF16)
    proj = _dot(xn, w_ref[...])

    r = lax.broadcasted_iota(I32, (ATTN_WIDTH, ATTN_WIDTH), 0) // HEAD_DIM
    c = lax.broadcasted_iota(I32, (ATTN_WIDTH, ATTN_WIDTH), 1) // HEAD_DIM
    same_head = (r == c).astype(BF16)

    def head_norm(t, g):
        hi, lo = _split_bf16(t * t)
        msq = (_dot(hi, same_head) + _dot(lo, same_head)) * (1.0 / HEAD_DIM)
        return t * lax.rsqrt(msq + EPS) * g

    q = head_norm(proj[:, :ATTN_WIDTH], gq_ref[...]) * (HEAD_DIM ** -0.5)
    k = head_norm(proj[:, ATTN_WIDTH:2 * ATTN_WIDTH], gk_ref[...])
    v = proj[:, 2 * ATTN_WIDTH:3 * ATTN_WIDTH]
    for h in range(ATTN_HEADS):
        sl = slice(h * HEAD_DIM, (h + 1) * HEAD_DIM)
        q_ref[h] = q[:, sl].astype(BF16)
        k_ref[h] = k[:, sl].astype(BF16)
        v_ref[h] = v[:, sl].astype(BF16)
    cw = (proj.shape[1] - 3 * ATTN_WIDTH) // 2
    a = proj[:, 3 * ATTN_WIDTH:3 * ATTN_WIDTH + cw]
    gate = proj[:, 3 * ATTN_WIDTH + cw:]
    hg_ref[...] = a * _sigmoid(gate)


def _inproj(x2, g, w_bf, gq_t, gk_t, *, tm):
    T, D = x2.shape
    E = w_bf.shape[1]
    cw = (E - 3 * ATTN_WIDTH) // 2
    head_shape = jax.ShapeDtypeStruct((ATTN_HEADS, T, HEAD_DIM), BF16)
    head_spec = pl.BlockSpec((ATTN_HEADS, tm, HEAD_DIM), lambda i: (0, i, 0))
    return pl.pallas_call(
        _inproj_kernel,
        grid=(T // tm,),
        in_specs=[
            pl.BlockSpec((tm, D), lambda i: (i, 0)),
            pl.BlockSpec((1, D), lambda i: (0, 0)),
            pl.BlockSpec((D, E), lambda i: (0, 0)),
            pl.BlockSpec((1, ATTN_WIDTH), lambda i: (0, 0)),
            pl.BlockSpec((1, ATTN_WIDTH), lambda i: (0, 0)),
        ],
        out_specs=[head_spec, head_spec, head_spec,
                   pl.BlockSpec((tm, cw), lambda i: (i, 0))],
        out_shape=[head_shape, head_shape, head_shape,
                   jax.ShapeDtypeStruct((T, cw), F32)],
        compiler_params=pltpu.CompilerParams(
            dimension_semantics=("arbitrary",), vmem_limit_bytes=VMEM_LIMIT),
        name="inproj",
    )(x2, g, w_bf, gq_t, gk_t)


def _attn_span(q, k, v, offset, carry, suffix):
    tw = suffix.shape[0]
    z = _dot_nt(q, k)
    sp = jnp.maximum(z, 0.0) + jnp.log(1.0 + jnp.exp(-jnp.abs(z)))
    col_minus_row = (lax.broadcasted_iota(I32, z.shape, 1)
                     - lax.broadcasted_iota(I32, z.shape, 0))
    mask = col_minus_row < offset
    log_keep = jnp.where(mask, -sp, 0.0)
    later = []
    for s in reversed(range(z.shape[1] // tw)):
        lk = log_keep[:, s * tw:(s + 1) * tw]
        hi, lo = _split_bf16(lk)
        later.append(carry + (_dot(hi, suffix) + _dot(lo, suffix)))
        carry = carry + jnp.sum(lk, axis=-1, keepdims=True)
    later = jnp.concatenate(later[::-1], axis=-1)
    att = jnp.where(mask, jnp.exp(z - sp + later), 0.0)
    return carry, _dot(att.astype(BF16), v)


def _attn_kernel(q_ref, k_ref, v_ref, o_ref, carry_ref, acc_ref, *, rows, span, tw):
    g = pl.program_id(2)
    hp = q_ref.shape[0]
    suffix = (lax.broadcasted_iota(I32, (tw, tw), 0)
              > lax.broadcasted_iota(I32, (tw, tw), 1)).astype(BF16)

    start = pl.multiple_of(jnp.maximum(g * rows - (span - rows), 0), rows)
    cmax = None
    for hh in range(hp):
        carry, acc = _attn_span(q_ref[hh], k_ref[hh, pl.ds(start, span), :],
                                v_ref[hh, pl.ds(start, span), :], g * rows - start,
                                jnp.zeros((rows, 1), F32), suffix)
        carry_ref[hh] = carry
        acc_ref[hh] = acc
        cmax = carry if cmax is None else jnp.maximum(cmax, carry)

    @pl.when(jnp.max(cmax) > EXP_UNDERFLOW)
    def _():
        for hh in range(hp):

            def cond(st):
                j, carry, _ = st
                return jnp.logical_and(j >= 0, jnp.max(carry) > EXP_UNDERFLOW)

            def body(st, hh=hh):
                j, carry, acc = st
                ks = pl.multiple_of(j * tw, tw)
                carry, out = _attn_span(q_ref[hh], k_ref[hh, pl.ds(ks, tw), :],
                                        v_ref[hh, pl.ds(ks, tw), :], rows + tw, carry, suffix)
                return j - 1, carry, acc + out

            init = (start // tw - 1, carry_ref[hh], acc_ref[hh])
            acc_ref[hh] = lax.while_loop(cond, body, init)[2]

    o_ref[...] = jnp.concatenate([acc_ref[hh] for hh in range(hp)], axis=-1)


def _attention(qh, kh, vh, *, batch, seq, rows=256, span=512, tw=256, heads_per_step=4):
    H, T, hd = qh.shape
    ng = seq // rows
    hp = heads_per_step
    return pl.pallas_call(
        functools.partial(_attn_kernel, rows=rows, span=span, tw=tw),
        grid=(H // hp, batch, ng),
        in_specs=[
            pl.BlockSpec((hp, rows, hd), lambda h, b, i: (h, b * ng + i, 0)),
            pl.BlockSpec((hp, seq, hd), lambda h, b, i: (h, b, 0)),
            pl.BlockSpec((hp, seq, hd), lambda h, b, i: (h, b, 0)),
        ],
        out_specs=pl.BlockSpec((rows, hp * hd), lambda h, b, i: (b * ng + i, h)),
        out_shape=jax.ShapeDtypeStruct((T, H * hd), F32),
        scratch_shapes=[pltpu.VMEM((hp, rows, 1), F32),
                        pltpu.VMEM((hp, rows, hd), F32)],
        compiler_params=pltpu.CompilerParams(
            dimension_semantics=("arbitrary", "arbitrary", "arbitrary"),
            vmem_limit_bytes=VMEM_LIMIT),
        name="sb_attention",
    )(qh, kh, vh)


def _mix_kernel(x_ref, o_ref, hg_ref, hgp_ref, cw_ref, cb_ref, lg_ref, lb_ref, ag_ref,
                wo_ref, g2_ref, wqh_ref, wql_ref, h_ref, xn_ref, xn3_ref, qh_ref, ext_ref,
                *, blocks_per_seq):
    tm = x_ref.shape[0]
    first = (pl.program_id(0) % blocks_per_seq) == 0
    ext_ref[0:CONV_HALO, :] = jnp.where(first, 0.0, hgp_ref[...])
    ext_ref[CONV_HALO:, :] = hg_ref[...]
    conv = jnp.zeros(hg_ref.shape, F32)
    for j in range(CONV_KERNEL):
        off = CONV_HALO - (CONV_KERNEL - 1) + j
        conv = conv + cw_ref[j:j + 1, :] * ext_ref[pl.ds(off, tm), :]
    conv = conv + cb_ref[...]
    mu = jnp.mean(conv, axis=-1, keepdims=True)
    xc = conv - mu
    var = jnp.mean(xc * xc, axis=-1, keepdims=True)
    y = xc * lax.rsqrt(var + EPS) * lg_ref[...] + lb_ref[...]
    o_conv = y * _sigmoid(y)

    o = o_ref[...]
    o_attn = o * lax.rsqrt(jnp.mean(o * o, axis=-1, keepdims=True) + EPS) * ag_ref[...]
    mixed = jnp.concatenate([o_attn, o_conv], axis=-1).astype(BF16)
    h = x_ref[...] + _dot(mixed, wo_ref[...])
    h_ref[...] = h

    xn = h * lax.rsqrt(jnp.mean(h * h, axis=-1, keepdims=True) + EPS) * g2_ref[...]
    xn_ref[...] = xn
    for c in range(xn3_ref.shape[1]):
        xn3_ref[:, c, :] = xn[:, c * 128:(c + 1) * 128]
    hi, lo = _split_bf16(xn)
    wqh = wqh_ref[...]
    qh_ref[...] = _dot(hi, wqh) + _dot(lo, wqh) + _dot(hi, wql_ref[...])


def _mix(x2, o, hg, conv_w, conv_b, ln_g, ln_b, attn_g, wo_bf, g2, wq_hi, wq_lo, *, seq, tm):
    T, D = x2.shape
    cwid = hg.shape[1]
    E = wq_hi.shape[1]
    halo_per_block = tm // CONV_HALO
    row = lambda i: (i, 0)
    const = lambda i: (0, 0)
    return pl.pallas_call(
        functools.partial(_mix_kernel, blocks_per_seq=seq // tm),
        grid=(T // tm,),
        in_specs=[
            pl.BlockSpec((tm, D), row),
            pl.BlockSpec((tm, o.shape[1]), row),
            pl.BlockSpec((tm, cwid), row),
            pl.BlockSpec((CONV_HALO, cwid),
                         lambda i: (jnp.maximum(i * halo_per_block - 1, 0), 0)),
            pl.BlockSpec((CONV_KERNEL, cwid), const),
            pl.BlockSpec((1, cwid), const),
            pl.BlockSpec((1, cwid), const),
            pl.BlockSpec((1, cwid), const),
            pl.BlockSpec((1, o.shape[1]), const),
            pl.BlockSpec(wo_bf.shape, const),
            pl.BlockSpec((1, D), const),
            pl.BlockSpec(wq_hi.shape, const),
            pl.BlockSpec(wq_lo.shape, const),
        ],
        out_specs=[pl.BlockSpec((tm, D), row), pl.BlockSpec((tm, D), row),
                   pl.BlockSpec((tm, D // 128, 128), lambda i: (i, 0, 0)),
                   pl.BlockSpec((tm, E), row)],
        out_shape=[jax.ShapeDtypeStruct((T, D), F32), jax.ShapeDtypeStruct((T, D), F32),
                   jax.ShapeDtypeStruct((T, D // 128, 128), F32),
                   jax.ShapeDtypeStruct((T, E), F32)],
        scratch_shapes=[pltpu.VMEM((tm + CONV_HALO, cwid), F32)],
        compiler_params=pltpu.CompilerParams(
            dimension_semantics=("arbitrary",), vmem_limit_bytes=VMEM_LIMIT),
        name="mix_outproj",
    )(x2, o, hg, hg, conv_w, conv_b, ln_g, ln_b, attn_g, wo_bf, g2, wq_hi, wq_lo)


def _top16(s, pos=None, payload=None):
    if pos is None:
        pos = lax.broadcasted_iota(I32, s.shape, 0).astype(F32)
    vals, outs = [], []
    for _ in range(PEER_TOPK):
        m = jnp.max(s, axis=0, keepdims=True)
        idx = jnp.min(jnp.where(s == m, pos, 1e9), axis=0, keepdims=True)
        sel = pos == idx
        if payload is None:
            outs.append(idx)
        else:
            outs.append(jnp.sum(jnp.where(sel, payload, 0.0), axis=0, keepdims=True))
        s = jnp.where(sel, -jnp.inf, s)
        vals.append(m)
    return jnp.concatenate(vals, axis=0), jnp.concatenate(outs, axis=0)


def _dot3_nt(a, b):
    ah, al = _split_bf16(a)
    bh, bl = _split_bf16(b)
    return _dot_nt(ah, bh) + _dot_nt(al, bh) + _dot_nt(ah, bl)


def _pair_candidates(v1, i1, v2, i2):
    r8 = lax.broadcasted_iota(I32, (8, v1.shape[1]), 0).astype(F32)
    sc, ex, ps = [], [], []

    def add(s1, e1, s2, e2, p, keep=None):
        s = s1 + s2
        sc.append(s if keep is None else jnp.where(keep, s, -jnp.inf))
        ex.append(e1 * float(N_KEYS) + e2)
        ps.append(p if keep is None else jnp.where(keep, p, 2e9))

    for a, half in ((0, 0), (0, 1), (1, 0), (2, 0), (3, 0)):
        b = slice(8 * half, 8 * half + 8)
        add(v1[a:a + 1], i1[a:a + 1], v2[b], i2[b], r8 + float(a * PEER_TOPK + 8 * half))
    for b in range(3):
        add(v1[0:8], i1[0:8], v2[b:b + 1], i2[b:b + 1], r8 * float(PEER_TOPK) + float(b),
            keep=r8 >= 4.0)
    add(v1[8:16], i1[8:16], v2[0:1], i2[0:1], (r8 + 8.0) * float(PEER_TOPK))
    return (jnp.concatenate(sc, axis=0), jnp.concatenate(ex, axis=0),
            jnp.concatenate(ps, axis=0))


def _topk_kernel(qh_ref, k1_ref, k2_ref, idx_ref, gate_ref):
    all_experts, all_gates = [], []
    for h in range(PEER_HEADS):
        base = h * 2 * PEER_HALF
        q1 = qh_ref[:, base:base + PEER_HALF]
        q2 = qh_ref[:, base + PEER_HALF:base + 2 * PEER_HALF]
        s1 = _dot3_nt(k1_ref[h], q1)
        s2 = _dot3_nt(k2_ref[h], q2)
        v1, i1 = _top16(s1)
        v2, i2 = _top16(s2)
        cand, cexp, cpos = _pair_candidates(v1, i1, v2, i2)
        top_s, experts = _top16(cand, pos=cpos, payload=cexp)
        e = jnp.exp(top_s - top_s[0:1, :])
        all_experts.append(experts)
        all_gates.append(e / jnp.sum(e, axis=0, keepdims=True))
    idx_ref[...] = jnp.concatenate(all_experts, axis=0).T.astype(I32) * CHUNKS
    gate_ref[...] = jnp.concatenate(all_gates, axis=0).T


def _topk(qh, k1, k2, *, tm):
    T, E = qh.shape
    out_spec = pl.BlockSpec((tm, SEL), lambda i: (i, 0))
    return pl.pallas_call(
        _topk_kernel,
        grid=(T // tm,),
        in_specs=[
            pl.BlockSpec((tm, E), lambda i: (i, 0)),
            pl.BlockSpec(k1.shape, lambda i: (0, 0, 0)),
            pl.BlockSpec(k2.shape, lambda i: (0, 0, 0)),
        ],
        out_specs=[out_spec, out_spec],
        out_shape=[jax.ShapeDtypeStruct((T, SEL), I32), jax.ShapeDtypeStruct((T, SEL), F32)],
        compiler_params=pltpu.CompilerParams(
            dimension_semantics=("arbitrary",), vmem_limit_bytes=VMEM_LIMIT),
        name="peer_topk",
    )(qh, k1, k2)


CHUNKS = 4
CHUNK_STRIDE = SEL + 8


def _unpack(word):
    lo = lax.bitcast_convert_type(word << 16, F32)
    hi = lax.bitcast_convert_type(word & HI_MASK, F32)
    return lo, hi


def _gather_group(idx_ref, tab_ref, buf_ref, t0, slots, part=0, parts=1):
    ids = [idx_ref.at[0, 0, pl.ds((t0 + j) * SEL, SEL)] for j in range(len(slots))]
    for k in range(part * SEL // parts, (part + 1) * SEL // parts):
        for j, s in enumerate(slots):
            r = pl.multiple_of(ids[j][k], CHUNKS)
            buf_ref[s, pl.ds(k, CHUNKS, stride=CHUNK_STRIDE), :] = tab_ref[pl.ds(r, CHUNKS), :]


def _cols(c):
    return slice(c * 128, (c + 1) * 128)


def _chunk(buf_ref, s, c):
    return _unpack(buf_ref[s, c * CHUNK_STRIDE:c * CHUNK_STRIDE + SEL, :])


def _lane_replicated_sum(a):
    ones = jnp.ones((128, 128), BF16)
    hi, lo = _split_bf16(a)
    return _dot(hi, ones) + _dot(lo, ones)


def _diag_mask():
    return (lax.broadcasted_iota(I32, (SEL, 128), 0) == lax.broadcasted_iota(I32, (SEL, 128), 1))


GROUP = 4


def _grouped_tokens(tb, gather, reduce):
    half = (tuple(range(GROUP)), tuple(range(GROUP, 2 * GROUP)))
    gather(0, half[0], 0, 1)

    def trip(i, carry):
        t = 2 * GROUP * i
        for j in range(GROUP):
            gather(t + GROUP, half[1], j, GROUP)
            reduce(t + j, half[0][j])
        nxt = jnp.minimum(t + 2 * GROUP, tb - GROUP)
        for j in range(GROUP):
            gather(nxt, half[0], j, GROUP)
            reduce(t + GROUP + j, half[1][j])
        return carry

    lax.fori_loop(0, tb // (2 * GROUP), trip, 0)


def _peer_u_kernel(idx_ref, x_ref, gate_ref, tab_ref, w_ref, buf_ref, act_ref):
    tb = x_ref.shape[0]
    diag = _diag_mask()

    def gather(t0, slots, part, parts):
        _gather_group(idx_ref, tab_ref, buf_ref, t0, slots, part, parts)

    def reduce(t, slot):
        total = None
        for c in range(CHUNKS):
            lo, hi = _chunk(buf_ref, slot, c)
            term = lo * x_ref[t, c:c + 1, :] + hi * x_ref[t, c + CHUNKS:c + CHUNKS + 1, :]
            total = term if total is None else total + term
        act_rep = _lane_replicated_sum(total)
        act_ref[pl.ds(t, 1), :] = jnp.sum(jnp.where(diag, act_rep, 0.0), axis=0, keepdims=True)

    _grouped_tokens(tb, gather, reduce)
    act = act_ref[...]
    gelu = 0.5 * act * (1.0 + lax.erf(act * (2.0 ** -0.5)))
    w_ref[...] = gate_ref[...] * gelu


def _gather_scratch():
    return pltpu.VMEM((2 * GROUP, CHUNKS * CHUNK_STRIDE, 128), I32)


def _index_spec(tb):
    return pl.BlockSpec((1, 1, tb * SEL), lambda i: (i, 0, 0), memory_space=pltpu.SMEM)


def _peer_u(idx3, x3, gates, tab, *, tb, tokens):
    return pl.pallas_call(
        _peer_u_kernel,
        grid=(tokens // tb,),
        in_specs=[
            _index_spec(tb),
            pl.BlockSpec((tb, 2 * CHUNKS, 128), lambda i: (i, 0, 0)),
            pl.BlockSpec((tb, SEL), lambda i: (i, 0)),
            pl.BlockSpec(memory_space=pltpu.VMEM),
        ],
        out_specs=pl.BlockSpec((tb, SEL), lambda i: (i, 0)),
        out_shape=jax.ShapeDtypeStruct((tokens, SEL), F32),
        scratch_shapes=[_gather_scratch(), pltpu.VMEM((tb, SEL), F32)],
        compiler_params=pltpu.CompilerParams(
            dimension_semantics=("arbitrary",), vmem_limit_bytes=VMEM_LIMIT),
        name="peer_u",
    )(idx3, x3, gates, tab)


def _peer_v_kernel(idx_ref, w_ref, h_ref, tab_ref, y_ref, buf_ref):
    tb = h_ref.shape[0]
    diag = _diag_mask()

    def gather(t0, slots, part, parts):
        _gather_group(idx_ref, tab_ref, buf_ref, t0, slots, part, parts)

    def reduce(t, slot):
        w_rep = _lane_replicated_sum(jnp.where(diag, w_ref[pl.ds(t, 1), :], 0.0))
        lows, highs = [], []
        for c in range(CHUNKS):
            lo, hi = _chunk(buf_ref, slot, c)
            lows.append(jnp.sum(lo * w_rep, axis=0, keepdims=True))
            highs.append(jnp.sum(hi * w_rep, axis=0, keepdims=True))
        y_ref[pl.ds(t, 1), :] = h_ref[pl.ds(t, 1), :] + jnp.concatenate(lows + highs, axis=-1)

    _grouped_tokens(tb, gather, reduce)


def _peer_v(idx3, w, h2, tab, *, tb, tokens):
    D = h2.shape[1]
    return pl.pallas_call(
        _peer_v_kernel,
        grid=(tokens // tb,),
        in_specs=[
            _index_spec(tb),
            pl.BlockSpec((tb, SEL), lambda i: (i, 0)),
            pl.BlockSpec((tb, D), lambda i: (i, 0)),
            pl.BlockSpec(memory_space=pltpu.VMEM),
        ],
        out_specs=pl.BlockSpec((tb, D), lambda i: (i, 0)),
        out_shape=jax.ShapeDtypeStruct((tokens, D), F32),
        scratch_shapes=[_gather_scratch()],
        compiler_params=pltpu.CompilerParams(
            dimension_semantics=("arbitrary",), vmem_limit_bytes=VMEM_LIMIT),
        name="peer_v",
    )(idx3, w, h2, tab)


SC_CORES = 2
SC_WORKERS = 32
SC_LANES = 16
ROW_WORDS = 512
HALF_WORDS = ROW_WORDS // 2
HALF_VECS = HALF_WORDS // SC_LANES


def _sc_lanes(j):
    return pl.ds(j * SC_LANES, SC_LANES)


def _sc_unpack(word):
    pair = plsc.bitcast(word, BF16)
    return plsc.unpack(pair, format=plsc.PackFormat.INTERLEAVED, preferred_element_type=F32)


def _sc_token_pipeline(ids_hbm, tab_hbm, ids_v, rows_v, sems, *, first, per, load_extra, process):
    base = (lax.axis_index("s") * SC_CORES + lax.axis_index("c")) * per

    def gather(slot, half):
        return pltpu.make_async_copy(tab_hbm.at[ids_v.at[slot, half]], rows_v.at[half],
                                     sems.at[half])

    def load_token(i, slot):
        pltpu.sync_copy(ids_hbm.at[first + base + i], ids_v.at[slot])
        load_extra(base + i, slot)

    load_token(0, 0)
    gather(0, 0).start()

    @pl.loop(0, per)
    def _(i):
        slot = i % 2
        gather(slot, 1).start()
        gather(slot, 0).wait()
        process(base + i, slot, 0)

        @pl.when(i + 1 < per)
        def _():
            load_token(i + 1, 1 - slot)
            gather(1 - slot, 0).start()

        gather(slot, 1).wait()
        process(base + i, slot, 1)


def _peer_u_sc_kernel(ids_hbm, x_hbm, tab_hbm, act_hbm, ids_v, x_v, part_v, act_v, rows_v, sems,
                      *, first, per):
    def load_extra(local, slot):
        pltpu.sync_copy(x_hbm.at[first + local], x_v.at[slot])

    def process(local, slot, half):
        off = half * HALF_VECS
        x_lo = [x_v[slot, _sc_lanes(off + j)] for j in range(HALF_VECS)]
        x_hi = [x_v[slot, _sc_lanes(2 * HALF_VECS + off + j)] for j in range(HALF_VECS)]

        @pl.loop(0, SEL, step=2)
        def _(k0):
            for k in (k0, k0 + 1):
                sums = [None] * 4
                for j in range(HALF_VECS):
                    lo, hi = _sc_unpack(rows_v[half, k, _sc_lanes(j)])
                    term = lo * x_lo[j] + hi * x_hi[j]
                    sums[j % 4] = term if sums[j % 4] is None else sums[j % 4] + term
                part_v[half * SEL + k, :] = (sums[0] + sums[1]) + (sums[2] + sums[3])

        if half == 1:
            lane_ids = lax.broadcasted_iota(I32, (SC_LANES,), 0)
            for kb in range(SEL // SC_LANES):
                rows0 = lane_ids + kb * SC_LANES
                total = None
                for l in range(SC_LANES):
                    col = jnp.full((SC_LANES,), l, I32)
                    both = (plsc.load_gather(part_v, [rows0, col])
                            + plsc.load_gather(part_v, [rows0 + SEL, col]))
                    total = both if total is None else total + both
                act_v[_sc_lanes(kb)] = total
            pltpu.sync_copy(act_v, act_hbm.at[local])

    _sc_token_pipeline(ids_hbm, tab_hbm, ids_v, rows_v, sems, first=first, per=per,
                       load_extra=load_extra, process=process)


def _sc_call(body, out_cols, n, scratch, name):
    return pl.kernel(
        body,
        out_type=jax.ShapeDtypeStruct((n, out_cols), F32),
        mesh=plsc.VectorSubcoreMesh(core_axis_name="c", subcore_axis_name="s"),
        scratch_types=scratch + [pltpu.VMEM((2, SEL, HALF_WORDS), I32),
                                 pltpu.SemaphoreType.DMA((2,))],
        compiler_params=pltpu.CompilerParams(needs_layout_passes=False),
        name=name,
    )


def _peer_u_sc(half_ids, x2, tab_halves, *, first):
    T, D = x2.shape
    n = T - first
    body = functools.partial(_peer_u_sc_kernel, first=first, per=n // SC_WORKERS)
    scratch = [pltpu.VMEM((2, 2, SEL), I32), pltpu.VMEM((2, D), F32),
               pltpu.VMEM((2 * SEL, SC_LANES), F32), pltpu.VMEM((SEL,), F32)]
    return _sc_call(body, SEL, n, scratch, "peer_u_sc")(half_ids, x2, tab_halves)


def _peer_v_sc_kernel(ids_hbm, w_hbm, h_hbm, tab_hbm, y_hbm, ids_v, w_v, h_v, y_v, rows_v, sems,
                      *, first, per):
    def load_extra(local, slot):
        pltpu.sync_copy(w_hbm.at[local], w_v.at[slot])
        pltpu.sync_copy(h_hbm.at[first + local], h_v.at[slot])

    def process(local, slot, half):
        off = half * HALF_VECS
        init = (tuple(h_v[slot, _sc_lanes(off + j)] for j in range(HALF_VECS))
                + tuple(h_v[slot, _sc_lanes(2 * HALF_VECS + off + j)] for j in range(HALF_VECS)))
        slot_vec = jnp.full((SC_LANES,), slot, I32)

        def body(k, acc):
            wk = plsc.load_gather(w_v, [slot_vec, jnp.full((SC_LANES,), k, I32)])
            lows, highs = [], []
            for j in range(HALF_VECS):
                lo, hi = _sc_unpack(rows_v[half, k, _sc_lanes(j)])
                lows.append(acc[j] + wk * lo)
                highs.append(acc[HALF_VECS + j] + wk * hi)
            return tuple(lows + highs)

        acc = lax.fori_loop(0, SEL, body, init)
        for j in range(HALF_VECS):
            y_v[_sc_lanes(off + j)] = acc[j]
            y_v[_sc_lanes(2 * HALF_VECS + off + j)] = acc[HALF_VECS + j]
        if half == 1:
            pltpu.sync_copy(y_v, y_hbm.at[local])

    _sc_token_pipeline(ids_hbm, tab_hbm, ids_v, rows_v, sems, first=first, per=per,
                       load_extra=load_extra, process=process)


def _peer_v_sc(half_ids, w_tail, h2, tab_halves, *, first):
    T, D = h2.shape
    n = T - first
    body = functools.partial(_peer_v_sc_kernel, first=first, per=n // SC_WORKERS)
    scratch = [pltpu.VMEM((2, 2, SEL), I32), pltpu.VMEM((2, SEL), F32),
               pltpu.VMEM((2, D), F32), pltpu.VMEM((D,), F32)]
    return _sc_call(body, D, n, scratch, "peer_v_sc")(half_ids, w_tail, h2, tab_halves)


def _gelu_gate_kernel(act_ref, gate_ref, after_ref, w_ref):
    del after_ref
    act = act_ref[...]
    w_ref[...] = gate_ref[...] * (0.5 * act * (1.0 + lax.erf(act * (2.0 ** -0.5))))


def _gelu_gate(act_tail, gates, after, *, first, tm):
    n = act_tail.shape[0]
    return pl.pallas_call(
        _gelu_gate_kernel,
        grid=(n // tm,),
        in_specs=[pl.BlockSpec((tm, SEL), lambda i: (i, 0)),
                  pl.BlockSpec((tm, SEL), lambda i: (i + first // tm, 0)),
                  pl.BlockSpec(memory_space=pl.ANY)],
        out_specs=pl.BlockSpec((tm, SEL), lambda i: (i, 0)),
        out_shape=jax.ShapeDtypeStruct((n, SEL), F32),
        name="peer_gelu_gate",
    )(act_tail, gates, after)


def _pack_table(tab):
    n, d = tab.shape
    bits = lax.bitcast_convert_type(tab.astype(BF16), jnp.uint16).astype(jnp.uint32)
    word = bits[:, :d // 2] | (bits[:, d // 2:] << 16)
    return lax.bitcast_convert_type(word, I32).reshape(n * 4, 128)


SHARE_UNITS = 32
SC_SHARE_U = 10
SC_SHARE_V = 13


def _layer(x2, p, *, batch, seq, tm_in, tm_mix, tm_topk, tb):
    T, D = x2.shape
    row = lambda a: a.reshape(1, -1)
    w_in_bf = p["w_in"].astype(BF16)
    gq_t = row(jnp.tile(p["q_norm_g"], ATTN_HEADS))
    gk_t = row(jnp.tile(p["k_norm_g"], ATTN_HEADS))
    qh, kh, vh, hg = _inproj(x2, row(p["norm_mix_g"]), w_in_bf, gq_t, gk_t, tm=tm_in)
    o = _attention(qh, kh, vh, batch=batch, seq=seq)

    wq = p["peer_wq"]
    wq_hi = wq.astype(BF16)
    wq_lo = (wq - wq_hi.astype(F32)).astype(BF16)
    h, xn, xn3, pq = _mix(x2, o, hg, p["conv_w"], row(p["conv_b"]), row(p["conv_ln_g"]),
                     row(p["conv_ln_b"]), row(p["attn_out_g"]), p["w_out"].astype(BF16),
                     row(p["norm_ffn_g"]), wq_hi, wq_lo, seq=seq, tm=tm_mix)

    idx, gates = _topk(pq, p["peer_k1"], p["peer_k2"], tm=tm_topk)
    u_tab = _pack_table(p["peer_u"])
    v_tab = _pack_table(p["peer_v"])
    head_u = T - T * SC_SHARE_U // SHARE_UNITS
    head_v = T - T * SC_SHARE_V // SHARE_UNITS
    assert head_v <= head_u
    for head in (head_u, head_v):
        assert head % tb == 0 and head % tm_topk == 0 and (T - head) % SC_WORKERS == 0
    half_row = idx // (CHUNKS // 2)
    half_ids = jnp.stack([half_row, half_row + 1], axis=1)
    idx3 = (half_ids[:, 0, :] * (CHUNKS // 2)).reshape(T // tb, 1, tb * SEL)
    act_tail = _peer_u_sc(half_ids, xn, u_tab.reshape(-1, HALF_WORDS), first=head_u)
    w_head = _peer_u(idx3, xn3, gates, u_tab, tb=tb, tokens=head_u)
    w_tail = _gelu_gate(act_tail, gates, w_head, first=head_u, tm=tm_topk)
    w_sc = jnp.concatenate([w_head[head_v:], w_tail], axis=0)
    y_tail = _peer_v_sc(half_ids, w_sc, h, v_tab.reshape(-1, HALF_WORDS), first=head_v)
    y_head = _peer_v(idx3, w_head, h, v_tab, tb=tb, tokens=head_v)
    return jnp.concatenate([y_head, y_tail], axis=0)


def kernel(x, norm_mix_g, w_in, q_norm_g, k_norm_g, attn_out_g, conv_w, conv_b, conv_ln_g,
           conv_ln_b, w_out, norm_ffn_g, peer_wq, peer_k1, peer_k2, peer_u, peer_v):
    batch, seq, d = x.shape
    stacked = dict(norm_mix_g=norm_mix_g, w_in=w_in, q_norm_g=q_norm_g, k_norm_g=k_norm_g,
                   attn_out_g=attn_out_g, conv_w=conv_w, conv_b=conv_b, conv_ln_g=conv_ln_g,
                   conv_ln_b=conv_ln_b, w_out=w_out, norm_ffn_g=norm_ffn_g, peer_wq=peer_wq,
                   peer_k1=peer_k1, peer_k2=peer_k2, peer_u=peer_u, peer_v=peer_v)
    h = x.reshape(batch * seq, d)
    for l in range(w_in.shape[0]):
        p = {name: a[l] for name, a in stacked.items()}
        h = _layer(h, p, batch=batch, seq=seq, tm_in=256, tm_mix=256,
                   tm_topk=256, tb=64)
    return h.reshape(batch, seq, d)
```

```python
import functools

import jax
import jax.numpy as jnp
from jax import lax
from jax.experimental import pallas as pl
from jax.experimental.pallas import tpu as pltpu
from jax.experimental.pallas import tpu_sc as plsc

F32 = jnp.float32
BF16 = jnp.bfloat16
I32 = jnp.int32

EPS = 1e-6
HEAD_DIM = 64
ATTN_HEADS = 8
ATTN_WIDTH = ATTN_HEADS * HEAD_DIM
CONV_KERNEL = 31
CONV_HALO = 32
PEER_HEADS = 8
PEER_HALF = 64
N_KEYS = 128
PEER_TOPK = 16
SEL = PEER_HEADS * PEER_TOPK
EXP_UNDERFLOW = -88.0
HI_MASK = -65536

VMEM_LIMIT = 56 * 1024 * 1024


def _split_bf16(a):
    hi = a.astype(BF16)
    lo = (a - hi.astype(F32)).astype(BF16)
    return hi, lo


def _dot(a, b):
    return jnp.dot(a, b, preferred_element_type=F32)


def _dot_nt(a, b):
    return lax.dot_general(a, b, (((1,), (1,)), ((), ())), preferred_element_type=F32)


def _sigmoid(x):
    return 1.0 / (1.0 + jnp.exp(-x))


def _inproj_kernel(x_ref, g_ref, w_ref, gq_ref, gk_ref, q_ref, k_ref, v_ref, hg_ref):
    x = x_ref[...]
    ms = jnp.mean(x * x, axis=-1, keepdims=True)
    xn = (x * lax.rsqrt(ms + EPS) * g_ref[...]).astype(BF16)
    proj = _dot(xn, w_ref[...])

    r = lax.broadcasted_iota(I32, (ATTN_WIDTH, ATTN_WIDTH), 0) // HEAD_DIM
    c = lax.broadcasted_iota(I32, (ATTN_WIDTH, ATTN_WIDTH), 1) // HEAD_DIM
    same_head = (r == c).astype(BF16)

    def head_norm(t, g):
        hi, lo = _split_bf16(t * t)
        msq = (_dot(hi, same_head) + _dot(lo, same_head)) * (1.0 / HEAD_DIM)
        return t * lax.rsqrt(msq + EPS) * g

    q = head_norm(proj[:, :ATTN_WIDTH], gq_ref[...]) * (HEAD_DIM ** -0.5)
    k = head_norm(proj[:, ATTN_WIDTH:2 * ATTN_WIDTH], gk_ref[...])
    v = proj[:, 2 * ATTN_WIDTH:3 * ATTN_WIDTH]
    for h in range(ATTN_HEADS):
        sl = slice(h * HEAD_DIM, (h + 1) * HEAD_DIM)
        q_ref[h] = q[:, sl].astype(BF16)
        k_ref[h] = k[:, sl].astype(BF16)
        v_ref[h] = v[:, sl].astype(BF16)
    cw = (proj.shape[1] - 3 * ATTN_WIDTH) // 2
    a = proj[:, 3 * ATTN_WIDTH:3 * ATTN_WIDTH + cw]
    gate = proj[:, 3 * ATTN_WIDTH + cw:]
    hg_ref[...] = a * _sigmoid(gate)


def _inproj(x2, g, w_bf, gq_t, gk_t, *, tm):
    T, D = x2.shape
    E = w_bf.shape[1]
    cw = (E - 3 * ATTN_WIDTH) // 2
    head_shape = jax.ShapeDtypeStruct((ATTN_HEADS, T, HEAD_DIM), BF16)
    head_spec = pl.BlockSpec((ATTN_HEADS, tm, HEAD_DIM), lambda i: (0, i, 0))
    return pl.pallas_call(
        _inproj_kernel,
        grid=(T // tm,),
        in_specs=[
            pl.BlockSpec((tm, D), lambda i: (i, 0)),
            pl.BlockSpec((1, D), lambda i: (0, 0)),
            pl.BlockSpec((D, E), lambda i: (0, 0)),
            pl.BlockSpec((1, ATTN_WIDTH), lambda i: (0, 0)),
            pl.BlockSpec((1, ATTN_WIDTH), lambda i: (0, 0)),
        ],
        out_specs=[head_spec, head_spec, head_spec,
                   pl.BlockSpec((tm, cw), lambda i: (i, 0))],
        out_shape=[head_shape, head_shape, head_shape,
                   jax.ShapeDtypeStruct((T, cw), F32)],
        compiler_params=pltpu.CompilerParams(
            dimension_semantics=("arbitrary",), vmem_limit_bytes=VMEM_LIMIT),
        name="inproj",
    )(x2, g, w_bf, gq_t, gk_t)


def _attn_span(q, k, v, offset, carry, suffix):
    tw = suffix.shape[0]
    z = _dot_nt(q, k)
    sp = jnp.maximum(z, 0.0) + jnp.log(1.0 + jnp.exp(-jnp.abs(z)))
    col_minus_row = (lax.broadcasted_iota(I32, z.shape, 1)
                     - lax.broadcasted_iota(I32, z.shape, 0))
    mask = col_minus_row < offset
    log_keep = jnp.where(mask, -sp, 0.0)
    later = []
    for s in reversed(range(z.shape[1] // tw)):
        lk = log_keep[:, s * tw:(s + 1) * tw]
        hi, lo = _split_bf16(lk)
        later.append(carry + (_dot(hi, suffix) + _dot(lo, suffix)))
        carry = carry + jnp.sum(lk, axis=-1, keepdims=True)
    later = jnp.concatenate(later[::-1], axis=-1)
    att = jnp.where(mask, jnp.exp(z - sp + later), 0.0)
    return carry, _dot(att.astype(BF16), v)


def _attn_kernel(q_ref, k_ref, v_ref, o_ref, carry_ref, acc_ref, *, rows, span, tw):
    g = pl.program_id(2)
    hp = q_ref.shape[0]
    suffix = (lax.broadcasted_iota(I32, (tw, tw), 0)
              > lax.broadcasted_iota(I32, (tw, tw), 1)).astype(BF16)

    start = pl.multiple_of(jnp.maximum(g * rows - (span - rows), 0), rows)
    cmax = None
    for hh in range(hp):
        carry, acc = _attn_span(q_ref[hh], k_ref[hh, pl.ds(start, span), :],
                                v_ref[hh, pl.ds(start, span), :], g * rows - start,
                                jnp.zeros((rows, 1), F32), suffix)
        carry_ref[hh] = carry
        acc_ref[hh] = acc
        cmax = carry if cmax is None else jnp.maximum(cmax, carry)

    @pl.when(jnp.max(cmax) > EXP_UNDERFLOW)
    def _():
        for hh in range(hp):

            def cond(st):
                j, carry, _ = st
                return jnp.logical_and(j >= 0, jnp.max(carry) > EXP_UNDERFLOW)

            def body(st, hh=hh):
                j, carry, acc = st
                ks = pl.multiple_of(j * tw, tw)
                carry, out = _attn_span(q_ref[hh], k_ref[hh, pl.ds(ks, tw), :],
                                        v_ref[hh, pl.ds(ks, tw), :], rows + tw, carry, suffix)
                return j - 1, carry, acc + out

            init = (start // tw - 1, carry_ref[hh], acc_ref[hh])
            acc_ref[hh] = lax.while_loop(cond, body, init)[2]

    o_ref[...] = jnp.concatenate([acc_ref[hh] for hh in range(hp)], axis=-1)


def _attention(qh, kh, vh, *, batch, seq, rows=256, span=512, tw=256, heads_per_step=8):
    H, T, hd = qh.shape
    ng = seq // rows
    hp = heads_per_step
    return pl.pallas_call(
        functools.partial(_attn_kernel, rows=rows, span=span, tw=tw),
        grid=(H // hp, batch, ng),
        in_specs=[
            pl.BlockSpec((hp, rows, hd), lambda h, b, i: (h, b * ng + i, 0)),
            pl.BlockSpec((hp, seq, hd), lambda h, b, i: (h, b, 0)),
            pl.BlockSpec((hp, seq, hd), lambda h, b, i: (h, b, 0)),
        ],
        out_specs=pl.BlockSpec((rows, hp * hd), lambda h, b, i: (b * ng + i, h)),
        out_shape=jax.ShapeDtypeStruct((T, H * hd), F32),
        scratch_shapes=[pltpu.VMEM((hp, rows, 1), F32),
                        pltpu.VMEM((hp, rows, hd), F32)],
        compiler_params=pltpu.CompilerParams(
            dimension_semantics=("arbitrary", "arbitrary", "arbitrary"),
            vmem_limit_bytes=VMEM_LIMIT),
        name="sb_attention",
    )(qh, kh, vh)


def _mix_kernel(x_ref, o_ref, hg_ref, hgp_ref, cw_ref, cb_ref, lg_ref, lb_ref, ag_ref,
                wo_ref, g2_ref, wqh_ref, wql_ref, h_ref, xn_ref, xn3_ref, qh_ref, ext_ref,
                *, blocks_per_seq):
    tm = x_ref.shape[0]
    first = (pl.program_id(0) % blocks_per_seq) == 0
    ext_ref[0:CONV_HALO, :] = jnp.where(first, 0.0, hgp_ref[...])
    ext_ref[CONV_HALO:, :] = hg_ref[...]
    conv = jnp.zeros(hg_ref.shape, F32)
    for j in range(CONV_KERNEL):
        off = CONV_HALO - (CONV_KERNEL - 1) + j
        conv = conv + cw_ref[j:j + 1, :] * ext_ref[pl.ds(off, tm), :]
    conv = conv + cb_ref[...]
    mu = jnp.mean(conv, axis=-1, keepdims=True)
    xc = conv - mu
    var = jnp.mean(xc * xc, axis=-1, keepdims=True)
    y = xc * lax.rsqrt(var + EPS) * lg_ref[...] + lb_ref[...]
    o_conv = y * _sigmoid(y)

    o = o_ref[...]
    o_attn = o * lax.rsqrt(jnp.mean(o * o, axis=-1, keepdims=True) + EPS) * ag_ref[...]
    mixed = jnp.concatenate([o_attn, o_conv], axis=-1).astype(BF16)
    h = x_ref[...] + _dot(mixed, wo_ref[...])
    h_ref[...] = h

    xn = h * lax.rsqrt(jnp.mean(h * h, axis=-1, keepdims=True) + EPS) * g2_ref[...]
    xn_ref[...] = xn
    for c in range(xn3_ref.shape[1]):
        xn3_ref[:, c, :] = xn[:, c * 128:(c + 1) * 128]
    hi, lo = _split_bf16(xn)
    wqh = wqh_ref[...]
    qh_ref[...] = _dot(hi, wqh) + _dot(lo, wqh) + _dot(hi, wql_ref[...])


def _mix(x2, o, hg, conv_w, conv_b, ln_g, ln_b, attn_g, wo_bf, g2, wq_hi, wq_lo, *, seq, tm):
    T, D = x2.shape
    cwid = hg.shape[1]
    E = wq_hi.shape[1]
    halo_per_block = tm // CONV_HALO
    row = lambda i: (i, 0)
    const = lambda i: (0, 0)
    return pl.pallas_call(
        functools.partial(_mix_kernel, blocks_per_seq=seq // tm),
        grid=(T // tm,),
        in_specs=[
            pl.BlockSpec((tm, D), row),
            pl.BlockSpec((tm, o.shape[1]), row),
            pl.BlockSpec((tm, cwid), row),
            pl.BlockSpec((CONV_HALO, cwid),
                         lambda i: (jnp.maximum(i * halo_per_block - 1, 0), 0)),
            pl.BlockSpec((CONV_KERNEL, cwid), const),
            pl.BlockSpec((1, cwid), const),
            pl.BlockSpec((1, cwid), const),
            pl.BlockSpec((1, cwid), const),
            pl.BlockSpec((1, o.shape[1]), const),
            pl.BlockSpec(wo_bf.shape, const),
            pl.BlockSpec((1, D), const),
            pl.BlockSpec(wq_hi.shape, const),
            pl.BlockSpec(wq_lo.shape, const),
        ],
        out_specs=[pl.BlockSpec((tm, D), row), pl.BlockSpec((tm, D), row),
                   pl.BlockSpec((tm, D // 128, 128), lambda i: (i, 0, 0)),
                   pl.BlockSpec((tm, E), row)],
        out_shape=[jax.ShapeDtypeStruct((T, D), F32), jax.ShapeDtypeStruct((T, D), F32),
                   jax.ShapeDtypeStruct((T, D // 128, 128), F32),
                   jax.ShapeDtypeStruct((T, E), F32)],
        scratch_shapes=[pltpu.VMEM((tm + CONV_HALO, cwid), F32)],
        compiler_params=pltpu.CompilerParams(
            dimension_semantics=("arbitrary",), vmem_limit_bytes=VMEM_LIMIT),
        name="mix_outproj",
    )(x2, o, hg, hg, conv_w, conv_b, ln_g, ln_b, attn_g, wo_bf, g2, wq_hi, wq_lo)


def _top16(s, pos=None, payload=None):
    if pos is None:
        pos = lax.broadcasted_iota(I32, s.shape, 0).astype(F32)
    vals, outs = [], []
    for _ in range(PEER_TOPK):
        m = jnp.max(s, axis=0, keepdims=True)
        idx = jnp.min(jnp.where(s == m, pos, 1e9), axis=0, keepdims=True)
        sel = pos == idx
        if payload is None:
            outs.append(idx)
        else:
            outs.append(jnp.sum(jnp.where(sel, payload, 0.0), axis=0, keepdims=True))
        s = jnp.where(sel, -jnp.inf, s)
        vals.append(m)
    return jnp.concatenate(vals, axis=0), jnp.concatenate(outs, axis=0)


def _dot3_nt(a, b):
    ah, al = _split_bf16(a)
    bh, bl = _split_bf16(b)
    return _dot_nt(ah, bh) + _dot_nt(al, bh) + _dot_nt(ah, bl)


def _pair_candidates(v1, i1, v2, i2):
    r8 = lax.broadcasted_iota(I32, (8, v1.shape[1]), 0).astype(F32)
    sc, ex, ps = [], [], []

    def add(s1, e1, s2, e2, p, keep=None):
        s = s1 + s2
        sc.append(s if keep is None else jnp.where(keep, s, -jnp.inf))
        ex.append(e1 * float(N_KEYS) + e2)
        ps.append(p if keep is None else jnp.where(keep, p, 2e9))

    for a, half in ((0, 0), (0, 1), (1, 0), (2, 0), (3, 0)):
        b = slice(8 * half, 8 * half + 8)
        add(v1[a:a + 1], i1[a:a + 1], v2[b], i2[b], r8 + float(a * PEER_TOPK + 8 * half))
    for b in range(3):
        add(v1[0:8], i1[0:8], v2[b:b + 1], i2[b:b + 1], r8 * float(PEER_TOPK) + float(b),
            keep=r8 >= 4.0)
    add(v1[8:16], i1[8:16], v2[0:1], i2[0:1], (r8 + 8.0) * float(PEER_TOPK))
    return (jnp.concatenate(sc, axis=0), jnp.concatenate(ex, axis=0),
            jnp.concatenate(ps, axis=0))


def _topk_kernel(qh_ref, k1_ref, k2_ref, idx_ref, gate_ref):
    all_experts, all_gates = [], []
    for h in range(PEER_HEADS):
        base = h * 2 * PEER_HALF
        q1 = qh_ref[:, base:base + PEER_HALF]
        q2 = qh_ref[:, base + PEER_HALF:base + 2 * PEER_HALF]
        s1 = _dot3_nt(k1_ref[h], q1)
        s2 = _dot3_nt(k2_ref[h], q2)
        v1, i1 = _top16(s1)
        v2, i2 = _top16(s2)
        cand, cexp, cpos = _pair_candidates(v1, i1, v2, i2)
        top_s, experts = _top16(cand, pos=cpos, payload=cexp)
        e = jnp.exp(top_s - top_s[0:1, :])
        all_experts.append(experts)
        all_gates.append(e / jnp.sum(e, axis=0, keepdims=True))
    idx_ref[...] = jnp.concatenate(all_experts, axis=0).T.astype(I32) * CHUNKS
    gate_ref[...] = jnp.concatenate(all_gates, axis=0).T


def _topk(qh, k1, k2, *, tm):
    T, E = qh.shape
    out_spec = pl.BlockSpec((tm, SEL), lambda i: (i, 0))
    return pl.pallas_call(
        _topk_kernel,
        grid=(T // tm,),
        in_specs=[
            pl.BlockSpec((tm, E), lambda i: (i, 0)),
            pl.BlockSpec(k1.shape, lambda i: (0, 0, 0)),
            pl.BlockSpec(k2.shape, lambda i: (0, 0, 0)),
        ],
        out_specs=[out_spec, out_spec],
        out_shape=[jax.ShapeDtypeStruct((T, SEL), I32), jax.ShapeDtypeStruct((T, SEL), F32)],
        compiler_params=pltpu.CompilerParams(
            dimension_semantics=("arbitrary",), vmem_limit_bytes=VMEM_LIMIT),
        name="peer_topk",
    )(qh, k1, k2)


CHUNKS = 4
CHUNK_STRIDE = SEL + 8


def _unpack(word):
    lo = lax.bitcast_convert_type(word << 16, F32)
    hi = lax.bitcast_convert_type(word & HI_MASK, F32)
    return lo, hi


def _gather_group(idx_ref, tab_ref, buf_ref, t0, slots, part=0, parts=1):
    ids = [idx_ref.at[0, 0, pl.ds((t0 + j) * SEL, SEL)] for j in range(len(slots))]
    for k in range(part * SEL // parts, (part + 1) * SEL // parts):
        for j, s in enumerate(slots):
            r = pl.multiple_of(ids[j][k], CHUNKS)
            buf_ref[s, pl.ds(k, CHUNKS, stride=CHUNK_STRIDE), :] = tab_ref[pl.ds(r, CHUNKS), :]


def _cols(c):
    return slice(c * 128, (c + 1) * 128)


def _chunk(buf_ref, s, c):
    return _unpack(buf_ref[s, c * CHUNK_STRIDE:c * CHUNK_STRIDE + SEL, :])


def _lane_replicated_sum(a):
    ones = jnp.ones((128, 128), BF16)
    hi, lo = _split_bf16(a)
    return _dot(hi, ones) + _dot(lo, ones)


def _diag_mask():
    return (lax.broadcasted_iota(I32, (SEL, 128), 0) == lax.broadcasted_iota(I32, (SEL, 128), 1))


GROUP = 4


def _grouped_tokens(tb, gather, reduce):
    half = (tuple(range(GROUP)), tuple(range(GROUP, 2 * GROUP)))
    gather(0, half[0], 0, 1)

    def trip(i, carry):
        t = 2 * GROUP * i
        for j in range(GROUP):
            gather(t + GROUP, half[1], j, GROUP)
            reduce(t + j, half[0][j])
        nxt = jnp.minimum(t + 2 * GROUP, tb - GROUP)
        for j in range(GROUP):
            gather(nxt, half[0], j, GROUP)
            reduce(t + GROUP + j, half[1][j])
        return carry

    lax.fori_loop(0, tb // (2 * GROUP), trip, 0)


def _peer_u_kernel(idx_ref, x_ref, gate_ref, tab_ref, w_ref, buf_ref, act_ref):
    tb = x_ref.shape[0]
    diag = _diag_mask()

    def gather(t0, slots, part, parts):
        _gather_group(idx_ref, tab_ref, buf_ref, t0, slots, part, parts)

    def reduce(t, slot):
        total = None
        for c in range(CHUNKS):
            lo, hi = _chunk(buf_ref, slot, c)
            term = lo * x_ref[t, c:c + 1, :] + hi * x_ref[t, c + CHUNKS:c + CHUNKS + 1, :]
            total = term if total is None else total + term
        act_rep = _lane_replicated_sum(total)
        act_ref[pl.ds(t, 1), :] = jnp.sum(jnp.where(diag, act_rep, 0.0), axis=0, keepdims=True)

    _grouped_tokens(tb, gather, reduce)
    act = act_ref[...]
    gelu = 0.5 * act * (1.0 + lax.erf(act * (2.0 ** -0.5)))
    w_ref[...] = gate_ref[...] * gelu


def _gather_scratch():
    return pltpu.VMEM((2 * GROUP, CHUNKS * CHUNK_STRIDE, 128), I32)


def _index_spec(tb):
    return pl.BlockSpec((1, 1, tb * SEL), lambda i: (i, 0, 0), memory_space=pltpu.SMEM)


def _peer_u(idx3, x3, gates, tab, *, tb, tokens):
    return pl.pallas_call(
        _peer_u_kernel,
        grid=(tokens // tb,),
        in_specs=[
            _index_spec(tb),
            pl.BlockSpec((tb, 2 * CHUNKS, 128), lambda i: (i, 0, 0)),
            pl.BlockSpec((tb, SEL), lambda i: (i, 0)),
            pl.BlockSpec(memory_space=pltpu.VMEM),
        ],
        out_specs=pl.BlockSpec((tb, SEL), lambda i: (i, 0)),
        out_shape=jax.ShapeDtypeStruct((tokens, SEL), F32),
        scratch_shapes=[_gather_scratch(), pltpu.VMEM((tb, SEL), F32)],
        compiler_params=pltpu.CompilerParams(
            dimension_semantics=("arbitrary",), vmem_limit_bytes=VMEM_LIMIT),
        name="peer_u",
    )(idx3, x3, gates, tab)


def _peer_v_kernel(idx_ref, w_ref, h_ref, tab_ref, y_ref, buf_ref):
    tb = h_ref.shape[0]
    diag = _diag_mask()

    def gather(t0, slots, part, parts):
        _gather_group(idx_ref, tab_ref, buf_ref, t0, slots, part, parts)

    def reduce(t, slot):
        w_rep = _lane_replicated_sum(jnp.where(diag, w_ref[pl.ds(t, 1), :], 0.0))
        lows, highs = [], []
        for c in range(CHUNKS):
            lo, hi = _chunk(buf_ref, slot, c)
            lows.append(jnp.sum(lo * w_rep, axis=0, keepdims=True))
            highs.append(jnp.sum(hi * w_rep, axis=0, keepdims=True))
        y_ref[pl.ds(t, 1), :] = h_ref[pl.ds(t, 1), :] + jnp.concatenate(lows + highs, axis=-1)

    _grouped_tokens(tb, gather, reduce)


def _peer_v(idx3, w, h2, tab, *, tb, tokens):
    D = h2.shape[1]
    return pl.pallas_call(
        _peer_v_kernel,
        grid=(tokens // tb,),
        in_specs=[
            _index_spec(tb),
            pl.BlockSpec((tb, SEL), lambda i: (i, 0)),
            pl.BlockSpec((tb, D), lambda i: (i, 0)),
            pl.BlockSpec(memory_space=pltpu.VMEM),
        ],
        out_specs=pl.BlockSpec((tb, D), lambda i: (i, 0)),
        out_shape=jax.ShapeDtypeStruct((tokens, D), F32),
        scratch_shapes=[_gather_scratch()],
        compiler_params=pltpu.CompilerParams(
            dimension_semantics=("arbitrary",), vmem_limit_bytes=VMEM_LIMIT),
        name="peer_v",
    )(idx3, w, h2, tab)


SC_CORES = 2
SC_WORKERS = 32
SC_LANES = 16
ROW_WORDS = 512
HALF_WORDS = ROW_WORDS // 2
HALF_VECS = HALF_WORDS // SC_LANES


def _sc_lanes(j):
    return pl.ds(j * SC_LANES, SC_LANES)


def _sc_unpack(word):
    pair = plsc.bitcast(word, BF16)
    return plsc.unpack(pair, format=plsc.PackFormat.INTERLEAVED, preferred_element_type=F32)


def _sc_token_pipeline(ids_hbm, tab_hbm, ids_v, rows_v, sems, *, first, per, load_extra, process):
    base = (lax.axis_index("s") * SC_CORES + lax.axis_index("c")) * per

    def gather(slot, half):
        return pltpu.make_async_copy(tab_hbm.at[ids_v.at[slot, half]], rows_v.at[half],
                                     sems.at[half])

    def load_token(i, slot):
        pltpu.sync_copy(ids_hbm.at[first + base + i], ids_v.at[slot])
        load_extra(base + i, slot)

    load_token(0, 0)
    gather(0, 0).start()

    @pl.loop(0, per)
    def _(i):
        slot = i % 2
        gather(slot, 1).start()
        gather(slot, 0).wait()
        process(base + i, slot, 0)

        @pl.when(i + 1 < per)
        def _():
            load_token(i + 1, 1 - slot)
            gather(1 - slot, 0).start()

        gather(slot, 1).wait()
        process(base + i, slot, 1)


def _peer_u_sc_kernel(ids_hbm, x_hbm, tab_hbm, act_hbm, ids_v, x_v, part_v, act_v, rows_v, sems,
                      *, first, per):
    def load_extra(local, slot):
        pltpu.sync_copy(x_hbm.at[first + local], x_v.at[slot])

    def process(local, slot, half):
        off = half * HALF_VECS
        x_lo = [x_v[slot, _sc_lanes(off + j)] for j in range(HALF_VECS)]
        x_hi = [x_v[slot, _sc_lanes(2 * HALF_VECS + off + j)] for j in range(HALF_VECS)]

        @pl.loop(0, SEL, step=2)
        def _(k0):
            for k in (k0, k0 + 1):
                sums = [None] * 4
                for j in range(HALF_VECS):
                    lo, hi = _sc_unpack(rows_v[half, k, _sc_lanes(j)])
                    term = lo * x_lo[j] + hi * x_hi[j]
                    sums[j % 4] = term if sums[j % 4] is None else sums[j % 4] + term
                part_v[half * SEL + k, :] = (sums[0] + sums[1]) + (sums[2] + sums[3])

        if half == 1:
            lane_ids = lax.broadcasted_iota(I32, (SC_LANES,), 0)
            for kb in range(SEL // SC_LANES):
                rows0 = lane_ids + kb * SC_LANES
                total = None
                for l in range(SC_LANES):
                    col = jnp.full((SC_LANES,), l, I32)
                    both = (plsc.load_gather(part_v, [rows0, col])
                            + plsc.load_gather(part_v, [rows0 + SEL, col]))
                    total = both if total is None else total + both
                act_v[_sc_lanes(kb)] = total
            pltpu.sync_copy(act_v, act_hbm.at[local])

    _sc_token_pipeline(ids_hbm, tab_hbm, ids_v, rows_v, sems, first=first, per=per,
                       load_extra=load_extra, process=process)


def _sc_call(body, out_cols, n, scratch, name):
    return pl.kernel(
        body,
        out_type=jax.ShapeDtypeStruct((n, out_cols), F32),
        mesh=plsc.VectorSubcoreMesh(core_axis_name="c", subcore_axis_name="s"),
        scratch_types=scratch + [pltpu.VMEM((2, SEL, HALF_WORDS), I32),
                                 pltpu.SemaphoreType.DMA((2,))],
        compiler_params=pltpu.CompilerParams(needs_layout_passes=False),
        name=name,
    )


def _peer_u_sc(half_ids, x2, tab_halves, *, first):
    T, D = x2.shape
    n = T - first
    body = functools.partial(_peer_u_sc_kernel, first=first, per=n // SC_WORKERS)
    scratch = [pltpu.VMEM((2, 2, SEL), I32), pltpu.VMEM((2, D), F32),
               pltpu.VMEM((2 * SEL, SC_LANES), F32), pltpu.VMEM((SEL,), F32)]
    return _sc_call(body, SEL, n, scratch, "peer_u_sc")(half_ids, x2, tab_halves)


def _peer_v_sc_kernel(ids_hbm, w_hbm, h_hbm, tab_hbm, y_hbm, ids_v, w_v, h_v, y_v, rows_v, sems,
                      *, first, per):
    def load_extra(local, slot):
        pltpu.sync_copy(w_hbm.at[local], w_v.at[slot])
        pltpu.sync_copy(h_hbm.at[first + local], h_v.at[slot])

    def process(local, slot, half):
        off = half * HALF_VECS
        init = (tuple(h_v[slot, _sc_lanes(off + j)] for j in range(HALF_VECS))
                + tuple(h_v[slot, _sc_lanes(2 * HALF_VECS + off + j)] for j in range(HALF_VECS)))
        slot_vec = jnp.full((SC_LANES,), slot, I32)

        def body(k, acc):
            wk = plsc.load_gather(w_v, [slot_vec, jnp.full((SC_LANES,), k, I32)])
            lows, highs = [], []
            for j in range(HALF_VECS):
                lo, hi = _sc_unpack(rows_v[half, k, _sc_lanes(j)])
                lows.append(acc[j] + wk * lo)
                highs.append(acc[HALF_VECS + j] + wk * hi)
            return tuple(lows + highs)

        acc = lax.fori_loop(0, SEL, body, init)
        for j in range(HALF_VECS):
            y_v[_sc_lanes(off + j)] = acc[j]
            y_v[_sc_lanes(2 * HALF_VECS + off + j)] = acc[HALF_VECS + j]
        if half == 1:
            pltpu.sync_copy(y_v, y_hbm.at[local])

    _sc_token_pipeline(ids_hbm, tab_hbm, ids_v, rows_v, sems, first=first, per=per,
                       load_extra=load_extra, process=process)


def _peer_v_sc(half_ids, w_tail, h2, tab_halves, *, first):
    T, D = h2.shape
    n = T - first
    body = functools.partial(_peer_v_sc_kernel, first=first, per=n // SC_WORKERS)
    scratch = [pltpu.VMEM((2, 2, SEL), I32), pltpu.VMEM((2, SEL), F32),
               pltpu.VMEM((2, D), F32), pltpu.VMEM((D,), F32)]
    return _sc_call(body, D, n, scratch, "peer_v_sc")(half_ids, w_tail, h2, tab_halves)


def _gelu_gate_kernel(act_ref, gate_ref, after_ref, w_ref):
    del after_ref
    act = act_ref[...]
    w_ref[...] = gate_ref[...] * (0.5 * act * (1.0 + lax.erf(act * (2.0 ** -0.5))))


def _gelu_gate(act_tail, gates, after, *, first, tm):
    n = act_tail.shape[0]
    return pl.pallas_call(
        _gelu_gate_kernel,
        grid=(n // tm,),
        in_specs=[pl.BlockSpec((tm, SEL), lambda i: (i, 0)),
                  pl.BlockSpec((tm, SEL), lambda i: (i + first // tm, 0)),
                  pl.BlockSpec(memory_space=pl.ANY)],
        out_specs=pl.BlockSpec((tm, SEL), lambda i: (i, 0)),
        out_shape=jax.ShapeDtypeStruct((n, SEL), F32),
        name="peer_gelu_gate",
    )(act_tail, gates, after)


def _pack_table(tab):
    n, d = tab.shape
    bits = lax.bitcast_convert_type(tab.astype(BF16), jnp.uint16).astype(jnp.uint32)
    word = bits[:, :d // 2] | (bits[:, d // 2:] << 16)
    return lax.bitcast_convert_type(word, I32).reshape(n * 4, 128)


SHARE_UNITS = 32
SC_SHARE_U = 10
SC_SHARE_V = 13


def _layer(x2, p, *, batch, seq, tm_in, tm_mix, tm_topk, tb):
    T, D = x2.shape
    row = lambda a: a.reshape(1, -1)
    w_in_bf = p["w_in"].astype(BF16)
    gq_t = row(jnp.tile(p["q_norm_g"], ATTN_HEADS))
    gk_t = row(jnp.tile(p["k_norm_g"], ATTN_HEADS))
    qh, kh, vh, hg = _inproj(x2, row(p["norm_mix_g"]), w_in_bf, gq_t, gk_t, tm=tm_in)
    o = _attention(qh, kh, vh, batch=batch, seq=seq)

    wq = p["peer_wq"]
    wq_hi = wq.astype(BF16)
    wq_lo = (wq - wq_hi.astype(F32)).astype(BF16)
    h, xn, xn3, pq = _mix(x2, o, hg, p["conv_w"], row(p["conv_b"]), row(p["conv_ln_g"]),
                     row(p["conv_ln_b"]), row(p["attn_out_g"]), p["w_out"].astype(BF16),
                     row(p["norm_ffn_g"]), wq_hi, wq_lo, seq=seq, tm=tm_mix)

    idx, gates = _topk(pq, p["peer_k1"], p["peer_k2"], tm=tm_topk)
    u_tab = _pack_table(p["peer_u"])
    v_tab = _pack_table(p["peer_v"])
    head_u = T - T * SC_SHARE_U // SHARE_UNITS
    head_v = T - T * SC_SHARE_V // SHARE_UNITS
    assert head_v <= head_u
    for head in (head_u, head_v):
        assert head % tb == 0 and head % tm_topk == 0 and (T - head) % SC_WORKERS == 0
    half_row = idx // (CHUNKS // 2)
    half_ids = jnp.stack([half_row, half_row + 1], axis=1)
    idx3 = (half_ids[:, 0, :] * (CHUNKS // 2)).reshape(T // tb, 1, tb * SEL)
    act_tail = _peer_u_sc(half_ids, xn, u_tab.reshape(-1, HALF_WORDS), first=head_u)
    w_head = _peer_u(idx3, xn3, gates, u_tab, tb=tb, tokens=head_u)
    w_tail = _gelu_gate(act_tail, gates, w_head, first=head_u, tm=tm_topk)
    w_sc = jnp.concatenate([w_head[head_v:], w_tail], axis=0)
    y_tail = _peer_v_sc(half_ids, w_sc, h, v_tab.reshape(-1, HALF_WORDS), first=head_v)
    y_head = _peer_v(idx3, w_head, h, v_tab, tb=tb, tokens=head_v)
    return jnp.concatenate([y_head, y_tail], axis=0)


def kernel(x, norm_mix_g, w_in, q_norm_g, k_norm_g, attn_out_g, conv_w, conv_b, conv_ln_g,
           conv_ln_b, w_out, norm_ffn_g, peer_wq, peer_k1, peer_k2, peer_u, peer_v):
    batch, seq, d = x.shape
    stacked = dict(norm_mix_g=norm_mix_g, w_in=w_in, q_norm_g=q_norm_g, k_norm_g=k_norm_g,
                   attn_out_g=attn_out_g, conv_w=conv_w, conv_b=conv_b, conv_ln_g=conv_ln_g,
                   conv_ln_b=conv_ln_b, w_out=w_out, norm_ffn_g=norm_ffn_g, peer_wq=peer_wq,
                   peer_k1=peer_k1, peer_k2=peer_k2, peer_u=peer_u, peer_v=peer_v)
    h = x.reshape(batch * seq, d)
    for l in range(w_in.shape[0]):
        p = {name: a[l] for name, a in stacked.items()}
        h = _layer(h, p, batch=batch, seq=seq, tm_in=256, tm_mix=256,
                   tm_topk=256, tb=64)
    return h.reshape(batch, seq, d)
```

```python
import functools

import jax
import jax.numpy as jnp
from jax import lax
from jax.experimental import pallas as pl
from jax.experimental.pallas import tpu as pltpu
from jax.experimental.pallas import tpu_sc as plsc

F32 = jnp.float32
BF16 = jnp.bfloat16
I32 = jnp.int32

EPS = 1e-6
HEAD_DIM = 64
ATTN_HEADS = 8
ATTN_WIDTH = ATTN_HEADS * HEAD_DIM
CONV_KERNEL = 31
CONV_HALO = 32
PEER_HEADS = 8
PEER_HALF = 64
N_KEYS = 128
PEER_TOPK = 16
SEL = PEER_HEADS * PEER_TOPK
EXP_UNDERFLOW = -88.0
HI_MASK = -65536

VMEM_LIMIT = 56 * 1024 * 1024


def _split_bf16(a):
    hi = a.astype(BF16)
    lo = (a - hi.astype(F32)).astype(BF16)
    return hi, lo


def _dot(a, b):
    return jnp.dot(a, b, preferred_element_type=F32)


def _dot_nt(a, b):
    return lax.dot_general(a, b, (((1,), (1,)), ((), ())), preferred_element_type=F32)


def _sigmoid(x):
    return 1.0 / (1.0 + jnp.exp(-x))


def _inproj_kernel(x_ref, g_ref, w_ref, gq_ref, gk_ref, q_ref, k_ref, v_ref, hg_ref):
    x = x_ref[...]
    ms = jnp.mean(x * x, axis=-1, keepdims=True)
    xn = (x * lax.rsqrt(ms + EPS) * g_ref[...]).astype(BF16)
    proj = _dot(xn, w_ref[...])

    r = lax.broadcasted_iota(I32, (ATTN_WIDTH, ATTN_WIDTH), 0) // HEAD_DIM
    c = lax.broadcasted_iota(I32, (ATTN_WIDTH, ATTN_WIDTH), 1) // HEAD_DIM
    same_head = (r == c).astype(BF16)

    def head_norm(t, g):
        hi, lo = _split_bf16(t * t)
        msq = (_dot(hi, same_head) + _dot(lo, same_head)) * (1.0 / HEAD_DIM)
        return t * lax.rsqrt(msq + EPS) * g

    q = head_norm(proj[:, :ATTN_WIDTH], gq_ref[...]) * (HEAD_DIM ** -0.5)
    k = head_norm(proj[:, ATTN_WIDTH:2 * ATTN_WIDTH], gk_ref[...])
    v = proj[:, 2 * ATTN_WIDTH:3 * ATTN_WIDTH]
    for h in range(ATTN_HEADS):
        sl = slice(h * HEAD_DIM, (h + 1) * HEAD_DIM)
        q_ref[h] = q[:, sl].astype(BF16)
        k_ref[h] = k[:, sl].astype(BF16)
        v_ref[h] = v[:, sl].astype(BF16)
    cw = (proj.shape[1] - 3 * ATTN_WIDTH) // 2
    a = proj[:, 3 * ATTN_WIDTH:3 * ATTN_WIDTH + cw]
    gate = proj[:, 3 * ATTN_WIDTH + cw:]
    hg_ref[...] = a * _sigmoid(gate)


def _inproj(x2, g, w_bf, gq_t, gk_t, *, tm):
    T, D = x2.shape
    E = w_bf.shape[1]
    cw = (E - 3 * ATTN_WIDTH) // 2
    head_shape = jax.ShapeDtypeStruct((ATTN_HEADS, T, HEAD_DIM), BF16)
    head_spec = pl.BlockSpec((ATTN_HEADS, tm, HEAD_DIM), lambda i: (0, i, 0))
    return pl.pallas_call(
        _inproj_kernel,
        grid=(T // tm,),
        in_specs=[
            pl.BlockSpec((tm, D), lambda i: (i, 0)),
            pl.BlockSpec((1, D), lambda i: (0, 0)),
            pl.BlockSpec((D, E), lambda i: (0, 0)),
            pl.BlockSpec((1, ATTN_WIDTH), lambda i: (0, 0)),
            pl.BlockSpec((1, ATTN_WIDTH), lambda i: (0, 0)),
        ],
        out_specs=[head_spec, head_spec, head_spec,
                   pl.BlockSpec((tm, cw), lambda i: (i, 0))],
        out_shape=[head_shape, head_shape, head_shape,
                   jax.ShapeDtypeStruct((T, cw), F32)],
        compiler_params=pltpu.CompilerParams(
            dimension_semantics=("arbitrary",), vmem_limit_bytes=VMEM_LIMIT),
        name="inproj",
    )(x2, g, w_bf, gq_t, gk_t)


def _attn_span(q, k, v, offset, carry, suffix):
    tw = suffix.shape[0]
    z = _dot_nt(q, k)
    sp = jnp.maximum(z, 0.0) + jnp.log(1.0 + jnp.exp(-jnp.abs(z)))
    col_minus_row = (lax.broadcasted_iota(I32, z.shape, 1)
                     - lax.broadcasted_iota(I32, z.shape, 0))
    mask = col_minus_row < offset
    log_keep = jnp.where(mask, -sp, 0.0)
    later = []
    for s in reversed(range(z.shape[1] // tw)):
        lk = log_keep[:, s * tw:(s + 1) * tw]
        hi, lo = _split_bf16(lk)
        later.append(carry + (_dot(hi, suffix) + _dot(lo, suffix)))
        carry = carry + jnp.sum(lk, axis=-1, keepdims=True)
    later = jnp.concatenate(later[::-1], axis=-1)
    att = jnp.where(mask, jnp.exp(z - sp + later), 0.0)
    return carry, _dot(att.astype(BF16), v)


def _attn_kernel(q_ref, k_ref, v_ref, o_ref, carry_ref, acc_ref, *, rows, span, tw):
    g = pl.program_id(2)
    hp = q_ref.shape[0]
    suffix = (lax.broadcasted_iota(I32, (tw, tw), 0)
              > lax.broadcasted_iota(I32, (tw, tw), 1)).astype(BF16)

    start = pl.multiple_of(jnp.maximum(g * rows - (span - rows), 0), rows)
    cmax = None
    for hh in range(hp):
        carry, acc = _attn_span(q_ref[hh], k_ref[hh, pl.ds(start, span), :],
                                v_ref[hh, pl.ds(start, span), :], g * rows - start,
                                jnp.zeros((rows, 1), F32), suffix)
        carry_ref[hh] = carry
        acc_ref[hh] = acc
        cmax = carry if cmax is None else jnp.maximum(cmax, carry)

    @pl.when(jnp.max(cmax) > EXP_UNDERFLOW)
    def _():
        for hh in range(hp):

            def cond(st):
                j, carry, _ = st
                return jnp.logical_and(j >= 0, jnp.max(carry) > EXP_UNDERFLOW)

            def body(st, hh=hh):
                j, carry, acc = st
                ks = pl.multiple_of(j * tw, tw)
                carry, out = _attn_span(q_ref[hh], k_ref[hh, pl.ds(ks, tw), :],
                                        v_ref[hh, pl.ds(ks, tw), :], rows + tw, carry, suffix)
                return j - 1, carry, acc + out

            init = (start // tw - 1, carry_ref[hh], acc_ref[hh])
            acc_ref[hh] = lax.while_loop(cond, body, init)[2]

    o_ref[...] = jnp.concatenate([acc_ref[hh] for hh in range(hp)], axis=-1)


def _attention(qh, kh, vh, *, batch, seq, rows=256, span=512, tw=256, heads_per_step=8):
    H, T, hd = qh.shape
    ng = seq // rows
    hp = heads_per_step
    return pl.pallas_call(
        functools.partial(_attn_kernel, rows=rows, span=span, tw=tw),
        grid=(H // hp, batch, ng),
        in_specs=[
            pl.BlockSpec((hp, rows, hd), lambda h, b, i: (h, b * ng + i, 0)),
            pl.BlockSpec((hp, seq, hd), lambda h, b, i: (h, b, 0)),
            pl.BlockSpec((hp, seq, hd), lambda h, b, i: (h, b, 0)),
        ],
        out_specs=pl.BlockSpec((rows, hp * hd), lambda h, b, i: (b * ng + i, h)),
        out_shape=jax.ShapeDtypeStruct((T, H * hd), F32),
        scratch_shapes=[pltpu.VMEM((hp, rows, 1), F32),
                        pltpu.VMEM((hp, rows, hd), F32)],
        compiler_params=pltpu.CompilerParams(
            dimension_semantics=("arbitrary", "arbitrary", "arbitrary"),
            vmem_limit_bytes=VMEM_LIMIT),
        name="sb_attention",
    )(qh, kh, vh)


def _mix_kernel(x_ref, o_ref, hg_ref, hgp_ref, cw_ref, cb_ref, lg_ref, lb_ref, ag_ref,
                wo_ref, g2_ref, wqh_ref, wql_ref, h_ref, xn3_ref, qh_ref, ext_ref,
                *, blocks_per_seq):
    tm = x_ref.shape[0]
    first = (pl.program_id(0) % blocks_per_seq) == 0
    ext_ref[0:CONV_HALO, :] = jnp.where(first, 0.0, hgp_ref[...])
    ext_ref[CONV_HALO:, :] = hg_ref[...]
    conv = jnp.zeros(hg_ref.shape, F32)
    for j in range(CONV_KERNEL):
        off = CONV_HALO - (CONV_KERNEL - 1) + j
        conv = conv + cw_ref[j:j + 1, :] * ext_ref[pl.ds(off, tm), :]
    conv = conv + cb_ref[...]
    mu = jnp.mean(conv, axis=-1, keepdims=True)
    xc = conv - mu
    var = jnp.mean(xc * xc, axis=-1, keepdims=True)
    y = xc * lax.rsqrt(var + EPS) * lg_ref[...] + lb_ref[...]
    o_conv = y * _sigmoid(y)

    o = o_ref[...]
    o_attn = o * lax.rsqrt(jnp.mean(o * o, axis=-1, keepdims=True) + EPS) * ag_ref[...]
    mixed = jnp.concatenate([o_attn, o_conv], axis=-1).astype(BF16)
    h = x_ref[...] + _dot(mixed, wo_ref[...])
    h_ref[...] = h

    xn = h * lax.rsqrt(jnp.mean(h * h, axis=-1, keepdims=True) + EPS) * g2_ref[...]
    for c in range(xn3_ref.shape[1]):
        xn3_ref[:, c, :] = xn[:, c * 128:(c + 1) * 128]
    hi, lo = _split_bf16(xn)
    wqh = wqh_ref[...]
    qh_ref[...] = _dot(hi, wqh) + _dot(lo, wqh) + _dot(hi, wql_ref[...])


def _mix(x2, o, hg, conv_w, conv_b, ln_g, ln_b, attn_g, wo_bf, g2, wq_hi, wq_lo, *, seq, tm):
    T, D = x2.shape
    cwid = hg.shape[1]
    E = wq_hi.shape[1]
    halo_per_block = tm // CONV_HALO
    row = lambda i: (i, 0)
    const = lambda i: (0, 0)
    return pl.pallas_call(
        functools.partial(_mix_kernel, blocks_per_seq=seq // tm),
        grid=(T // tm,),
        in_specs=[
            pl.BlockSpec((tm, D), row),
            pl.BlockSpec((tm, o.shape[1]), row),
            pl.BlockSpec((tm, cwid), row),
            pl.BlockSpec((CONV_HALO, cwid),
                         lambda i: (jnp.maximum(i * halo_per_block - 1, 0), 0)),
            pl.BlockSpec((CONV_KERNEL, cwid), const),
            pl.BlockSpec((1, cwid), const),
            pl.BlockSpec((1, cwid), const),
            pl.BlockSpec((1, cwid), const),
            pl.BlockSpec((1, o.shape[1]), const),
            pl.BlockSpec(wo_bf.shape, const),
            pl.BlockSpec((1, D), const),
            pl.BlockSpec(wq_hi.shape, const),
            pl.BlockSpec(wq_lo.shape, const),
        ],
        out_specs=[pl.BlockSpec((tm, D), row),
                   pl.BlockSpec((tm, D // 128, 128), lambda i: (i, 0, 0)),
                   pl.BlockSpec((tm, E), row)],
        out_shape=[jax.ShapeDtypeStruct((T, D), F32),
                   jax.ShapeDtypeStruct((T, D // 128, 128), F32),
                   jax.ShapeDtypeStruct((T, E), F32)],
        scratch_shapes=[pltpu.VMEM((tm + CONV_HALO, cwid), F32)],
        compiler_params=pltpu.CompilerParams(
            dimension_semantics=("arbitrary",), vmem_limit_bytes=VMEM_LIMIT),
        name="mix_outproj",
    )(x2, o, hg, hg, conv_w, conv_b, ln_g, ln_b, attn_g, wo_bf, g2, wq_hi, wq_lo)


def _top16(s, pos=None, payload=None):
    if pos is None:
        pos = lax.broadcasted_iota(I32, s.shape, 0).astype(F32)
    vals, outs = [], []
    for _ in range(PEER_TOPK):
        m = jnp.max(s, axis=0, keepdims=True)
        idx = jnp.min(jnp.where(s == m, pos, 1e9), axis=0, keepdims=True)
        sel = pos == idx
        if payload is None:
            outs.append(idx)
        else:
            outs.append(jnp.sum(jnp.where(sel, payload, 0.0), axis=0, keepdims=True))
        s = jnp.where(sel, -jnp.inf, s)
        vals.append(m)
    return jnp.concatenate(vals, axis=0), jnp.concatenate(outs, axis=0)


def _dot3_nt(a, b):
    ah, al = _split_bf16(a)
    bh, bl = _split_bf16(b)
    return _dot_nt(ah, bh) + _dot_nt(al, bh) + _dot_nt(ah, bl)


def _pair_candidates(v1, i1, v2, i2):
    r8 = lax.broadcasted_iota(I32, (8, v1.shape[1]), 0).astype(F32)
    sc, ex, ps = [], [], []

    def add(s1, e1, s2, e2, p, keep=None):
        s = s1 + s2
        sc.append(s if keep is None else jnp.where(keep, s, -jnp.inf))
        ex.append(e1 * float(N_KEYS) + e2)
        ps.append(p if keep is None else jnp.where(keep, p, 2e9))

    for a, half in ((0, 0), (0, 1), (1, 0), (2, 0), (3, 0)):
        b = slice(8 * half, 8 * half + 8)
        add(v1[a:a + 1], i1[a:a + 1], v2[b], i2[b], r8 + float(a * PEER_TOPK + 8 * half))
    for b in range(3):
        add(v1[0:8], i1[0:8], v2[b:b + 1], i2[b:b + 1], r8 * float(PEER_TOPK) + float(b),
            keep=r8 >= 4.0)
    add(v1[8:16], i1[8:16], v2[0:1], i2[0:1], (r8 + 8.0) * float(PEER_TOPK))
    return (jnp.concatenate(sc, axis=0), jnp.concatenate(ex, axis=0),
            jnp.concatenate(ps, axis=0))


def _topk_kernel(qh_ref, k1_ref, k2_ref, idx_ref, gate_ref):
    all_experts, all_gates = [], []
    for h in range(PEER_HEADS):
        base = h * 2 * PEER_HALF
        q1 = qh_ref[:, base:base + PEER_HALF]
        q2 = qh_ref[:, base + PEER_HALF:base + 2 * PEER_HALF]
        s1 = _dot3_nt(k1_ref[h], q1)
        s2 = _dot3_nt(k2_ref[h], q2)
        v1, i1 = _top16(s1)
        v2, i2 = _top16(s2)
        cand, cexp, cpos = _pair_candidates(v1, i1, v2, i2)
        top_s, experts = _top16(cand, pos=cpos, payload=cexp)
        e = jnp.exp(top_s - top_s[0:1, :])
        all_experts.append(experts)
        all_gates.append(e / jnp.sum(e, axis=0, keepdims=True))
    idx_ref[...] = jnp.concatenate(all_experts, axis=0).T.astype(I32) * CHUNKS
    gate_ref[...] = jnp.concatenate(all_gates, axis=0).T


def _topk(qh, k1, k2, *, tm):
    T, E = qh.shape
    out_spec = pl.BlockSpec((tm, SEL), lambda i: (i, 0))
    return pl.pallas_call(
        _topk_kernel,
        grid=(T // tm,),
        in_specs=[
            pl.BlockSpec((tm, E), lambda i: (i, 0)),
            pl.BlockSpec(k1.shape, lambda i: (0, 0, 0)),
            pl.BlockSpec(k2.shape, lambda i: (0, 0, 0)),
        ],
        out_specs=[out_spec, out_spec],
        out_shape=[jax.ShapeDtypeStruct((T, SEL), I32), jax.ShapeDtypeStruct((T, SEL), F32)],
        compiler_params=pltpu.CompilerParams(
            dimension_semantics=("arbitrary",), vmem_limit_bytes=VMEM_LIMIT),
        name="peer_topk",
    )(qh, k1, k2)


CHUNKS = 4
CHUNK_STRIDE = SEL + 8


def _unpack(word):
    lo = lax.bitcast_convert_type(word << 16, F32)
    hi = lax.bitcast_convert_type(word & HI_MASK, F32)
    return lo, hi


def _gather_group(idx_ref, tab_ref, buf_ref, t0, slots, part=0, parts=1):
    ids = [idx_ref.at[0, 0, pl.ds((t0 + j) * SEL, SEL)] for j in range(len(slots))]
    for k in range(part * SEL // parts, (part + 1) * SEL // parts):
        for j, s in enumerate(slots):
            r = pl.multiple_of(ids[j][k], CHUNKS)
            buf_ref[s, pl.ds(k, CHUNKS, stride=CHUNK_STRIDE), :] = tab_ref[pl.ds(r, CHUNKS), :]


def _cols(c):
    return slice(c * 128, (c + 1) * 128)


def _chunk(buf_ref, s, c):
    return _unpack(buf_ref[s, c * CHUNK_STRIDE:c * CHUNK_STRIDE + SEL, :])


def _lane_replicated_sum(a):
    ones = jnp.ones((128, 128), BF16)
    hi, lo = _split_bf16(a)
    return _dot(hi, ones) + _dot(lo, ones)


def _diag_mask():
    return (lax.broadcasted_iota(I32, (SEL, 128), 0) == lax.broadcasted_iota(I32, (SEL, 128), 1))


GROUP = 4


def _grouped_tokens(tb, gather, reduce):
    half = (tuple(range(GROUP)), tuple(range(GROUP, 2 * GROUP)))
    gather(0, half[0], 0, 1)

    def trip(i, carry):
        t = 2 * GROUP * i
        for j in range(GROUP):
            gather(t + GROUP, half[1], j, GROUP)
            reduce(t + j, half[0][j])
        nxt = jnp.minimum(t + 2 * GROUP, tb - GROUP)
        for j in range(GROUP):
            gather(nxt, half[0], j, GROUP)
            reduce(t + GROUP + j, half[1][j])
        return carry

    lax.fori_loop(0, tb // (2 * GROUP), trip, 0)


def _peer_u_kernel(idx_ref, x_ref, gate_ref, tab_ref, w_ref, buf_ref, act_ref):
    tb = x_ref.shape[0]
    diag = _diag_mask()

    def gather(t0, slots, part, parts):
        _gather_group(idx_ref, tab_ref, buf_ref, t0, slots, part, parts)

    def reduce(t, slot):
        total = None
        for c in range(CHUNKS):
            lo, hi = _chunk(buf_ref, slot, c)
            term = lo * x_ref[t, c:c + 1, :] + hi * x_ref[t, c + CHUNKS:c + CHUNKS + 1, :]
            total = term if total is None else total + term
        act_rep = _lane_replicated_sum(total)
        act_ref[pl.ds(t, 1), :] = jnp.sum(jnp.where(diag, act_rep, 0.0), axis=0, keepdims=True)

    _grouped_tokens(tb, gather, reduce)
    act = act_ref[...]
    gelu = 0.5 * act * (1.0 + lax.erf(act * (2.0 ** -0.5)))
    w_ref[...] = gate_ref[...] * gelu


def _gather_scratch():
    return pltpu.VMEM((2 * GROUP, CHUNKS * CHUNK_STRIDE, 128), I32)


def _index_spec(tb):
    return pl.BlockSpec((1, 1, tb * SEL), lambda i: (i, 0, 0), memory_space=pltpu.SMEM)


def _peer_u(idx3, x3, gates, tab, *, tb, tokens):
    return pl.pallas_call(
        _peer_u_kernel,
        grid=(tokens // tb,),
        in_specs=[
            _index_spec(tb),
            pl.BlockSpec((tb, 2 * CHUNKS, 128), lambda i: (i, 0, 0)),
            pl.BlockSpec((tb, SEL), lambda i: (i, 0)),
            pl.BlockSpec(memory_space=pltpu.VMEM),
        ],
        out_specs=pl.BlockSpec((tb, SEL), lambda i: (i, 0)),
        out_shape=jax.ShapeDtypeStruct((tokens, SEL), F32),
        scratch_shapes=[_gather_scratch(), pltpu.VMEM((tb, SEL), F32)],
        compiler_params=pltpu.CompilerParams(
            dimension_semantics=("arbitrary",), vmem_limit_bytes=VMEM_LIMIT),
        name="peer_u",
    )(idx3, x3, gates, tab)


def _peer_v_kernel(idx_ref, w_ref, h_ref, tab_ref, y_ref, buf_ref):
    tb = h_ref.shape[0]
    diag = _diag_mask()

    def gather(t0, slots, part, parts):
        _gather_group(idx_ref, tab_ref, buf_ref, t0, slots, part, parts)

    def reduce(t, slot):
        w_rep = _lane_replicated_sum(jnp.where(diag, w_ref[pl.ds(t, 1), :], 0.0))
        lows, highs = [], []
        for c in range(CHUNKS):
            lo, hi = _chunk(buf_ref, slot, c)
            lows.append(jnp.sum(lo * w_rep, axis=0, keepdims=True))
            highs.append(jnp.sum(hi * w_rep, axis=0, keepdims=True))
        y_ref[pl.ds(t, 1), :] = h_ref[pl.ds(t, 1), :] + jnp.concatenate(lows + highs, axis=-1)

    _grouped_tokens(tb, gather, reduce)


def _peer_v(idx3, w, h2, tab, *, tb, tokens):
    D = h2.shape[1]
    return pl.pallas_call(
        _peer_v_kernel,
        grid=(tokens // tb,),
        in_specs=[
            _index_spec(tb),
            pl.BlockSpec((tb, SEL), lambda i: (i, 0)),
            pl.BlockSpec((tb, D), lambda i: (i, 0)),
            pl.BlockSpec(memory_space=pltpu.VMEM),
        ],
        out_specs=pl.BlockSpec((tb, D), lambda i: (i, 0)),
        out_shape=jax.ShapeDtypeStruct((tokens, D), F32),
        scratch_shapes=[_gather_scratch()],
        compiler_params=pltpu.CompilerParams(
            dimension_semantics=("arbitrary",), vmem_limit_bytes=VMEM_LIMIT),
        name="peer_v",
    )(idx3, w, h2, tab)


SC_CORES = 2
SC_WORKERS = 32
SC_LANES = 16
ROW_WORDS = 512
HALF_WORDS = ROW_WORDS // 2
HALF_VECS = HALF_WORDS // SC_LANES


def _sc_lanes(j):
    return pl.ds(j * SC_LANES, SC_LANES)


def _sc_unpack(word):
    pair = plsc.bitcast(word, BF16)
    return plsc.unpack(pair, format=plsc.PackFormat.INTERLEAVED, preferred_element_type=F32)


def _sc_token_pipeline(ids_hbm, tab_hbm, ids_v, rows_v, sems, *, first, per, load_extra, process):
    base = (lax.axis_index("s") * SC_CORES + lax.axis_index("c")) * per

    def gather(slot, half):
        return pltpu.make_async_copy(tab_hbm.at[ids_v.at[slot, half]], rows_v.at[half],
                                     sems.at[half])

    def load_token(i, slot):
        pltpu.sync_copy(ids_hbm.at[first + base + i], ids_v.at[slot])
        load_extra(base + i, slot)

    load_token(0, 0)
    gather(0, 0).start()

    @pl.loop(0, per)
    def _(i):
        slot = i % 2
        gather(slot, 1).start()
        gather(slot, 0).wait()
        process(base + i, slot, 0)

        @pl.when(i + 1 < per)
        def _():
            load_token(i + 1, 1 - slot)
            gather(1 - slot, 0).start()

        gather(slot, 1).wait()
        process(base + i, slot, 1)


def _peer_u_sc_kernel(ids_hbm, x_hbm, tab_hbm, act_hbm, ids_v, x_v, part_v, act_v, rows_v, sems,
                      *, first, per):
    def load_extra(local, slot):
        pltpu.sync_copy(x_hbm.at[first + local], x_v.at[slot])

    def process(local, slot, half):
        off = half * HALF_VECS
        def x_vec(v):
            return x_v[slot, v // 8, pl.ds((v % 8) * SC_LANES, SC_LANES)]

        x_lo = [x_vec(off + j) for j in range(HALF_VECS)]
        x_hi = [x_vec(2 * HALF_VECS + off + j) for j in range(HALF_VECS)]

        @pl.loop(0, SEL, step=2)
        def _(k0):
            for k in (k0, k0 + 1):
                sums = [None] * 4
                for j in range(HALF_VECS):
                    lo, hi = _sc_unpack(rows_v[half, k, _sc_lanes(j)])
                    term = lo * x_lo[j] + hi * x_hi[j]
                    sums[j % 4] = term if sums[j % 4] is None else sums[j % 4] + term
                part_v[half * SEL + k, :] = (sums[0] + sums[1]) + (sums[2] + sums[3])

        if half == 1:
            lane_ids = lax.broadcasted_iota(I32, (SC_LANES,), 0)
            for kb in range(SEL // SC_LANES):
                rows0 = lane_ids + kb * SC_LANES
                total = None
                for l in range(SC_LANES):
                    col = jnp.full((SC_LANES,), l, I32)
                    both = (plsc.load_gather(part_v, [rows0, col])
                            + plsc.load_gather(part_v, [rows0 + SEL, col]))
                    total = both if total is None else total + both
                act_v[_sc_lanes(kb)] = total
            pltpu.sync_copy(act_v, act_hbm.at[local])

    _sc_token_pipeline(ids_hbm, tab_hbm, ids_v, rows_v, sems, first=first, per=per,
                       load_extra=load_extra, process=process)


def _sc_call(body, out_cols, n, scratch, name):
    return pl.kernel(
        body,
        out_type=jax.ShapeDtypeStruct((n, out_cols), F32),
        mesh=plsc.VectorSubcoreMesh(core_axis_name="c", subcore_axis_name="s"),
        scratch_types=scratch + [pltpu.VMEM((2, SEL, HALF_WORDS), I32),
                                 pltpu.SemaphoreType.DMA((2,))],
        compiler_params=pltpu.CompilerParams(needs_layout_passes=False),
        name=name,
    )


def _peer_u_sc(half_ids, x3, tab_halves, *, first):
    n = x3.shape[0] - first
    body = functools.partial(_peer_u_sc_kernel, first=first, per=n // SC_WORKERS)
    scratch = [pltpu.VMEM((2, 2, SEL), I32), pltpu.VMEM((2,) + x3.shape[1:], F32),
               pltpu.VMEM((2 * SEL, SC_LANES), F32), pltpu.VMEM((SEL,), F32)]
    return _sc_call(body, SEL, n, scratch, "peer_u_sc")(half_ids, x3, tab_halves)


def _peer_v_sc_kernel(ids_hbm, w_hbm, h_hbm, tab_hbm, y_hbm, ids_v, w_v, h_v, y_v, rows_v, sems,
                      *, first, per):
    def load_extra(local, slot):
        pltpu.sync_copy(w_hbm.at[local], w_v.at[slot])
        pltpu.sync_copy(h_hbm.at[first + local], h_v.at[slot])

    def process(local, slot, half):
        off = half * HALF_VECS
        init = (tuple(h_v[slot, _sc_lanes(off + j)] for j in range(HALF_VECS))
                + tuple(h_v[slot, _sc_lanes(2 * HALF_VECS + off + j)] for j in range(HALF_VECS)))
        slot_vec = jnp.full((SC_LANES,), slot, I32)

        def body(k, acc):
            wk = plsc.load_gather(w_v, [slot_vec, jnp.full((SC_LANES,), k, I32)])
            lows, highs = [], []
            for j in range(HALF_VECS):
                lo, hi = _sc_unpack(rows_v[half, k, _sc_lanes(j)])
                lows.append(acc[j] + wk * lo)
                highs.append(acc[HALF_VECS + j] + wk * hi)
            return tuple(lows + highs)

        acc = lax.fori_loop(0, SEL, body, init)
        for j in range(HALF_VECS):
            y_v[_sc_lanes(off + j)] = acc[j]
            y_v[_sc_lanes(2 * HALF_VECS + off + j)] = acc[HALF_VECS + j]
        if half == 1:
            pltpu.sync_copy(y_v, y_hbm.at[local])

    _sc_token_pipeline(ids_hbm, tab_hbm, ids_v, rows_v, sems, first=first, per=per,
                       load_extra=load_extra, process=process)


def _peer_v_sc(half_ids, w_tail, h2, tab_halves, *, first):
    T, D = h2.shape
    n = T - first
    body = functools.partial(_peer_v_sc_kernel, first=first, per=n // SC_WORKERS)
    scratch = [pltpu.VMEM((2, 2, SEL), I32), pltpu.VMEM((2, SEL), F32),
               pltpu.VMEM((2, D), F32), pltpu.VMEM((D,), F32)]
    return _sc_call(body, D, n, scratch, "peer_v_sc")(half_ids, w_tail, h2, tab_halves)


def _gelu_gate_kernel(act_ref, gate_ref, after_ref, w_ref):
    del after_ref
    act = act_ref[...]
    w_ref[...] = gate_ref[...] * (0.5 * act * (1.0 + lax.erf(act * (2.0 ** -0.5))))


def _gelu_gate(act_tail, gates, after, *, first, tm):
    n = act_tail.shape[0]
    return pl.pallas_call(
        _gelu_gate_kernel,
        grid=(n // tm,),
        in_specs=[pl.BlockSpec((tm, SEL), lambda i: (i, 0)),
                  pl.BlockSpec((tm, SEL), lambda i: (i + first // tm, 0)),
                  pl.BlockSpec(memory_space=pl.ANY)],
        out_specs=pl.BlockSpec((tm, SEL), lambda i: (i, 0)),
        out_shape=jax.ShapeDtypeStruct((n, SEL), F32),
        name="peer_gelu_gate",
    )(act_tail, gates, after)


def _pack_table(tab):
    n, d = tab.shape
    bits = lax.bitcast_convert_type(tab.astype(BF16), jnp.uint16).astype(jnp.uint32)
    word = bits[:, :d // 2] | (bits[:, d // 2:] << 16)
    return lax.bitcast_convert_type(word, I32).reshape(n * 4, 128)


SHARE_UNITS = 32
SC_SHARE_U = 10
SC_SHARE_V = 13


def _layer(x2, p, *, batch, seq, tm_in, tm_mix, tm_topk, tb):
    T, D = x2.shape
    row = lambda a: a.reshape(1, -1)
    w_in_bf = p["w_in"].astype(BF16)
    gq_t = row(jnp.tile(p["q_norm_g"], ATTN_HEADS))
    gk_t = row(jnp.tile(p["k_norm_g"], ATTN_HEADS))
    qh, kh, vh, hg = _inproj(x2, row(p["norm_mix_g"]), w_in_bf, gq_t, gk_t, tm=tm_in)
    o = _attention(qh, kh, vh, batch=batch, seq=seq)

    wq = p["peer_wq"]
    wq_hi = wq.astype(BF16)
    wq_lo = (wq - wq_hi.astype(F32)).astype(BF16)
    h, xn3, pq = _mix(x2, o, hg, p["conv_w"], row(p["conv_b"]), row(p["conv_ln_g"]),
                     row(p["conv_ln_b"]), row(p["attn_out_g"]), p["w_out"].astype(BF16),
                     row(p["norm_ffn_g"]), wq_hi, wq_lo, seq=seq, tm=tm_mix)

    idx, gates = _topk(pq, p["peer_k1"], p["peer_k2"], tm=tm_topk)
    u_tab = _pack_table(p["peer_u"])
    v_tab = _pack_table(p["peer_v"])
    head_u = T - T * SC_SHARE_U // SHARE_UNITS
    head_v = T - T * SC_SHARE_V // SHARE_UNITS
    assert head_v <= head_u
    for head in (head_u, head_v):
        assert head % tb == 0 and head % tm_topk == 0 and (T - head) % SC_WORKERS == 0
    half_row = idx // (CHUNKS // 2)
    half_ids = jnp.stack([half_row, half_row + 1], axis=1)
    idx3 = (half_ids[:, 0, :] * (CHUNKS // 2)).reshape(T // tb, 1, tb * SEL)
    act_tail = _peer_u_sc(half_ids, xn3, u_tab.reshape(-1, HALF_WORDS), first=head_u)
    w_head = _peer_u(idx3, xn3, gates, u_tab, tb=tb, tokens=head_u)
    w_tail = _gelu_gate(act_tail, gates, w_head, first=head_u, tm=tm_topk)
    w_sc = jnp.concatenate([w_head[head_v:], w_tail], axis=0)
    y_tail = _peer_v_sc(half_ids, w_sc, h, v_tab.reshape(-1, HALF_WORDS), first=head_v)
    y_head = _peer_v(idx3, w_head, h, v_tab, tb=tb, tokens=head_v)
    return jnp.concatenate([y_head, y_tail], axis=0)


def kernel(x, norm_mix_g, w_in, q_norm_g, k_norm_g, attn_out_g, conv_w, conv_b, conv_ln_g,
           conv_ln_b, w_out, norm_ffn_g, peer_wq, peer_k1, peer_k2, peer_u, peer_v):
    batch, seq, d = x.shape
    stacked = dict(norm_mix_g=norm_mix_g, w_in=w_in, q_norm_g=q_norm_g, k_norm_g=k_norm_g,
                   attn_out_g=attn_out_g, conv_w=conv_w, conv_b=conv_b, conv_ln_g=conv_ln_g,
                   conv_ln_b=conv_ln_b, w_out=w_out, norm_ffn_g=norm_ffn_g, peer_wq=peer_wq,
                   peer_k1=peer_k1, peer_k2=peer_k2, peer_u=peer_u, peer_v=peer_v)
    h = x.reshape(batch * seq, d)
    for l in range(w_in.shape[0]):
        p = {name: a[l] for name, a in stacked.items()}
        h = _layer(h, p, batch=batch, seq=seq, tm_in=256, tm_mix=256,
                   tm_topk=256, tb=64)
    return h.reshape(batch, seq, d)
```
